```python
import jax, jax.numpy as jnp
from jax import lax
import numpy as np

D_MODEL = 1024
BATCH = 8
SEQ = 4096
DEPTH = 4

CHUNK = 64
Q_BLOCK = 128
PLE_DIM = 256
D_FF = 2816
CONV_DIM = 512
CONV_GROUPS = 8
CONV_K = 3
N_HEADS = 8
NOPE_DIM = 128
ROPE_DIM = 64
V_DIM = 128
Q_LORA = 384
KV_LORA = 256
ROPE_THETA = 10000.0
EPS = 1e-6
QK_DIM = NOPE_DIM + ROPE_DIM
ATTN_SCALE = QK_DIM ** -0.5
IN_SPLITS = (CONV_DIM, CONV_DIM, CONV_DIM, Q_LORA, KV_LORA, ROPE_DIM, D_MODEL, D_MODEL)
IN_COLS = sum(IN_SPLITS)

kernel_name = "hybrid_conv_mla_macaron_ple_trunk"


def rmsnorm(x, g):
    xf = x.astype(jnp.float32)
    y = xf * lax.rsqrt(jnp.mean(xf * xf, axis=-1, keepdims=True) + EPS)
    return (y * g.astype(jnp.float32)).astype(x.dtype)


def swiglu(x, w_gu, w_down):
    g, u = jnp.split(x @ w_gu, 2, axis=-1)
    return (jax.nn.silu(g) * u) @ w_down


def rope_tables(positions):
    inv_freq = ROPE_THETA ** (-jnp.arange(0, ROPE_DIM, 2, dtype=jnp.float32) / ROPE_DIM)
    ang = positions.astype(jnp.float32)[..., None] * inv_freq
    return jnp.cos(ang), jnp.sin(ang)


def apply_rope(x, cos, sin):
    half = ROPE_DIM // 2
    xf = x.astype(jnp.float32)
    x1, x2 = xf[..., :half], xf[..., half:]
    return jnp.concatenate([x1 * cos - x2 * sin, x2 * cos + x1 * sin], axis=-1).astype(x.dtype)


def short_conv_branch(b_gate, c_gate, v, conv_w, w_conv_out):
    seq = v.shape[1]
    z = c_gate * v
    zp = jnp.pad(z, ((0, 0), (CONV_K - 1, 0), (0, 0)))
    y = conv_w[0] * zp[:, 0:seq]
    for j in range(1, CONV_K):
        y = y + conv_w[j] * zp[:, j:j + seq]
    return (b_gate * y) @ w_conv_out


def block_causal_attention(q_nope, q_rope, k_nope, k_rope, v):
    seq = q_nope.shape[1]
    outs = []
    for i in range(seq // Q_BLOCK):
        q0, q1 = i * Q_BLOCK, (i + 1) * Q_BLOCK
        kn, kr, vb = k_nope[:, :q1], k_rope[:, :q1], v[:, :q1]
        s = (jnp.einsum('bqhd,bkhd->bhqk', q_nope[:, q0:q1], kn)
             + jnp.einsum('bqhd,bkd->bhqk', q_rope[:, q0:q1], kr)).astype(jnp.float32) * ATTN_SCALE
        q_chunk = (q0 + jnp.arange(Q_BLOCK)) // CHUNK
        k_chunk = jnp.arange(q1) // CHUNK
        mask = k_chunk[None, :] <= q_chunk[:, None]
        s = jnp.where(mask, s, -1e30)
        pr = jax.nn.softmax(s, axis=-1).astype(v.dtype)
        outs.append(jnp.einsum('bhqk,bkhd->bqhd', pr, vb))
    return jnp.concatenate(outs, axis=1)


def mla_branch(q_c, kv_c, k_r, q_norm_g, kv_norm_g, w_uq, w_ukv, w_mla_out, cos, sin):
    b, s, _ = q_c.shape
    q = (rmsnorm(q_c, q_norm_g) @ w_uq).reshape(b, s, N_HEADS, QK_DIM)
    q_nope = q[..., :NOPE_DIM]
    q_rope = apply_rope(q[..., NOPE_DIM:], cos[:, :, None, :], sin[:, :, None, :])
    kv = (rmsnorm(kv_c, kv_norm_g) @ w_ukv).reshape(b, s, N_HEADS, NOPE_DIM + V_DIM)
    k_nope, v = kv[..., :NOPE_DIM], kv[..., NOPE_DIM:]
    k_rope = apply_rope(k_r, cos, sin)
    o = block_causal_attention(q_nope, q_rope, k_nope, k_rope, v)
    return o.reshape(b, s, N_HEADS * V_DIM) @ w_mla_out


def setup_inputs(seed: int = 0) -> dict:
    key = jax.random.key(seed)
    ks = jax.random.split(key, 24)
    f32 = jnp.float32

    def w(k, shape, fan_in):
        return jax.random.normal(k, shape, f32) * (fan_in ** -0.5)

    def gain(k, shape):
        return 1.0 + 0.05 * jax.random.normal(k, shape, f32)

    x = jax.random.normal(ks[0], (BATCH, SEQ, D_MODEL), f32)
    p = jax.random.normal(ks[1], (DEPTH, BATCH, SEQ, PLE_DIM), f32)
    offset = jax.random.randint(ks[2], (BATCH, 1), 0, 4096, dtype=jnp.int32)
    positions = offset + jnp.arange(SEQ, dtype=jnp.int32)[None, :]
    return {
        "x": x,
        "p": p,
        "positions": positions,
        "ffn1_norm": gain(ks[3], (DEPTH, D_MODEL)),
        "ffn1_w_gu": w(ks[4], (DEPTH, D_MODEL, 2 * D_FF), D_MODEL),
        "ffn1_w_down": w(ks[5], (DEPTH, D_FF, D_MODEL), D_FF),
        "mix_norm": gain(ks[6], (DEPTH, D_MODEL)),
        "w_in": w(ks[7], (DEPTH, D_MODEL, IN_COLS), D_MODEL),
        "conv_w": w(ks[8], (DEPTH, CONV_K, CONV_DIM), CONV_K),
        "w_conv_out": w(ks[9], (DEPTH, CONV_DIM, D_MODEL), CONV_DIM),
        "q_norm": gain(ks[10], (DEPTH, Q_LORA)),
        "kv_norm": gain(ks[11], (DEPTH, KV_LORA)),
        "w_uq": w(ks[12], (DEPTH, Q_LORA, N_HEADS * QK_DIM), Q_LORA),
        "w_ukv": w(ks[13], (DEPTH, KV_LORA, N_HEADS * (NOPE_DIM + V_DIM)), KV_LORA),
        "w_mla_out": w(ks[14], (DEPTH, N_HEADS * V_DIM, D_MODEL), N_HEADS * V_DIM),
        "w_o": w(ks[15], (DEPTH, D_MODEL, D_MODEL), D_MODEL),
        "ffn2_norm": gain(ks[16], (DEPTH, D_MODEL)),
        "ffn2_w_gu": w(ks[17], (DEPTH, D_MODEL, 2 * D_FF), D_MODEL),
        "ffn2_w_down": w(ks[18], (DEPTH, D_FF, D_MODEL), D_FF),
        "ple_norm": gain(ks[19], (DEPTH, D_MODEL)),
        "w_ple_gate": w(ks[20], (DEPTH, D_MODEL, D_MODEL), D_MODEL),
        "w_ple_proj": w(ks[21], (DEPTH, PLE_DIM, D_MODEL), PLE_DIM),
        "final_norm": gain(ks[22], (D_MODEL,)),
    }


def reference(x, p, positions, ffn1_norm, ffn1_w_gu, ffn1_w_down, mix_norm, w_in, conv_w,
              w_conv_out, q_norm, kv_norm, w_uq, w_ukv, w_mla_out, w_o, ffn2_norm, ffn2_w_gu,
              ffn2_w_down, ple_norm, w_ple_gate, w_ple_proj, final_norm):
    cos, sin = rope_tables(positions)
    split_pts = list(np.cumsum(IN_SPLITS)[:-1])
    h = x
    for i in range(DEPTH):
        h = h + 0.5 * swiglu(rmsnorm(h, ffn1_norm[i]), ffn1_w_gu[i], ffn1_w_down[i])

        u = rmsnorm(h, mix_norm[i])
        b_g, c_g, v_c, q_c, kv_c, k_r, g_conv, g_mla = jnp.split(u @ w_in[i], split_pts, axis=-1)
        y_conv = short_conv_branch(b_g, c_g, v_c, conv_w[i], w_conv_out[i])
        y_mla = mla_branch(q_c, kv_c, k_r, q_norm[i], kv_norm[i], w_uq[i], w_ukv[i],
                           w_mla_out[i], cos, sin)
        merged = jax.nn.sigmoid(g_conv) * y_conv + jax.nn.sigmoid(g_mla) * y_mla
        h = h + merged @ w_o[i]

        h = h + 0.5 * swiglu(rmsnorm(h, ffn2_norm[i]), ffn2_w_gu[i], ffn2_w_down[i])

        gate = jax.nn.sigmoid(rmsnorm(h, ple_norm[i]) @ w_ple_gate[i])
        h = h + gate * (p[i] @ w_ple_proj[i])
    return rmsnorm(h, final_norm)
```

```python
import functools

import jax
import jax.numpy as jnp
from jax import lax
from jax.experimental import pallas as pl
from jax.experimental.pallas import tpu as pltpu

D_MODEL = 1024
BATCH = 8
SEQ = 4096
DEPTH = 4
CHUNK = 64
PLE_DIM = 256
D_FF = 2816
CONV_DIM = 512
CONV_K = 3
N_HEADS = 8
NOPE_DIM = 128
ROPE_DIM = 64
V_DIM = 128
Q_LORA = 384
KV_LORA = 256
ROPE_THETA = 10000.0
EPS = 1e-6
QK_DIM = NOPE_DIM + ROPE_DIM
ATTN_SCALE = QK_DIM ** -0.5
TOKENS = BATCH * SEQ

LANES = 128
QK_PAD = 2 * LANES
OFF_B, OFF_C, OFF_V = 0, CONV_DIM, 2 * CONV_DIM
OFF_Q = 3 * CONV_DIM
OFF_KV = OFF_Q + Q_LORA
OFF_KR = OFF_KV + KV_LORA
OFF_GC = OFF_KR + LANES
OFF_GM = OFF_GC + D_MODEL
IN_COLS_PAD = OFF_GM + D_MODEL

TM_FFN = 512
TM_MIX = 256
TQ = 256
TK = 256
CONV_HALO = 8
VMEM_LIMIT = 56 * 1024 * 1024

F32 = jnp.float32
BF16 = jnp.bfloat16


def _rmsnorm(x, g):
    return x * lax.rsqrt(jnp.mean(x * x, axis=-1, keepdims=True) + EPS) * g


def _dot(a, b):
    return jnp.dot(a, b, preferred_element_type=F32)


def _dot_nt(a, b):
    return lax.dot_general(a, b, (((1,), (1,)), ((), ())), preferred_element_type=F32)


def _const_spec(shape, layer):
    zeros = (0,) * len(shape)
    return pl.BlockSpec((None,) + tuple(shape), lambda *_: (layer,) + zeros,
                        pipeline_mode=pl.Buffered(1))


def _ffn_kernel(*refs, with_ple, with_final):
    x_ref, g_ref, wgu_ref, wd_ref = refs[:4]
    o_ref = refs[-1]
    x = x_ref[...]
    xn = _rmsnorm(x, g_ref[...]).astype(BF16)
    gu = _dot(xn, wgu_ref[...])
    g, u = gu[:, :D_FF], gu[:, D_FF:]
    a = (g * jax.nn.sigmoid(g) * u).astype(BF16)
    h = x + 0.5 * _dot(a, wd_ref[...])
    if with_ple:
        p_ref, pg_ref, wpg_ref, wpp_ref = refs[4:8]
        hn = _rmsnorm(h, pg_ref[...]).astype(BF16)
        gate = jax.nn.sigmoid(_dot(hn, wpg_ref[...]))
        h = h + gate * _dot(p_ref[...].astype(BF16), wpp_ref[...])
    if with_final:
        h = _rmsnorm(h, refs[8][...])
    o_ref[...] = h


def _ffn(h, layer, norm, w_gu, w_down, ple=None, final_norm=None):
    tile = pl.BlockSpec((TM_FFN, D_MODEL), lambda i: (i, 0))
    in_specs = [tile, _const_spec((1, D_MODEL), layer),
                _const_spec((D_MODEL, 2 * D_FF), layer), _const_spec((D_FF, D_MODEL), layer)]
    args = [h, norm, w_gu, w_down]
    if ple is not None:
        p, ple_norm, w_pg, w_pp = ple
        in_specs += [pl.BlockSpec((None, TM_FFN, PLE_DIM), lambda i: (layer, i, 0)),
                     _const_spec((1, D_MODEL), layer),
                     _const_spec((D_MODEL, D_MODEL), layer), _const_spec((PLE_DIM, D_MODEL), layer)]
        args += [p, ple_norm, w_pg, w_pp]
    if final_norm is not None:
        in_specs.append(pl.BlockSpec((1, D_MODEL), lambda i: (0, 0)))
        args.append(final_norm)
    return pl.pallas_call(
        functools.partial(_ffn_kernel, with_ple=ple is not None, with_final=final_norm is not None),
        grid=(TOKENS // TM_FFN,),
        in_specs=in_specs,
        out_specs=tile,
        out_shape=jax.ShapeDtypeStruct((TOKENS, D_MODEL), F32),
        compiler_params=pltpu.CompilerParams(dimension_semantics=("arbitrary",),
                                             vmem_limit_bytes=VMEM_LIMIT),
        name="ffn_ple" if ple is not None else "ffn",
    )(*args)


def _rope(x, cos_t, sin_t):
    width = x.shape[-1]
    first_half = (lax.broadcasted_iota(jnp.int32, x.shape, 1) % ROPE_DIM) < ROPE_DIM // 2
    swapped = jnp.where(first_half, pltpu.roll(x, width - ROPE_DIM // 2, 1),
                        pltpu.roll(x, ROPE_DIM // 2, 1))
    return x * cos_t + swapped * sin_t


def _mix_proj_kernel(h_ref, cos_ref, sin_ref, g_ref, win_ref, cw_ref, wco_ref, qg_ref, kvg_ref,
                     wuq_ref, wk_ref, wvt_ref,
                     q_ref, k_ref, vt_ref, gc_ref, sg_ref, z_ref):
    i = pl.program_id(1)
    u = _rmsnorm(h_ref[...], g_ref[...]).astype(BF16)
    proj = _dot(u, win_ref[...])

    @pl.when(i == 0)
    def _():
        z_ref[0:CONV_HALO, :] = jnp.zeros((CONV_HALO, CONV_DIM), F32)

    z = proj[:, OFF_C:OFF_C + CONV_DIM] * proj[:, OFF_V:OFF_V + CONV_DIM]
    z_ref[CONV_HALO:, :] = z
    cw = cw_ref[...]
    y = (cw[0:1] * z_ref[CONV_HALO - 2:CONV_HALO - 2 + TM_MIX, :]
         + cw[1:2] * z_ref[CONV_HALO - 1:CONV_HALO - 1 + TM_MIX, :]
         + cw[2:3] * z)
    z_ref[0:CONV_HALO, :] = z_ref[TM_MIX:TM_MIX + CONV_HALO, :]
    y_conv = _dot((proj[:, OFF_B:OFF_B + CONV_DIM] * y).astype(BF16), wco_ref[...])
    gc_ref[...] = jax.nn.sigmoid(proj[:, OFF_GC:OFF_GC + D_MODEL]) * y_conv
    sg_ref[...] = jax.nn.sigmoid(proj[:, OFF_GM:OFF_GM + D_MODEL])

    cos_t, sin_t = cos_ref[...], sin_ref[...]
    qn = _rmsnorm(proj[:, OFF_Q:OFF_Q + Q_LORA], qg_ref[...]).astype(BF16)
    q = _dot(qn, wuq_ref[...]) * ATTN_SCALE
    kvn = _rmsnorm(proj[:, OFF_KV:OFF_KV + KV_LORA], kvg_ref[...]).astype(BF16)
    k_nope = _dot(kvn, wk_ref[...])
    v_t = _dot_nt(wvt_ref[...], kvn)
    k_rope = _rope(proj[:, OFF_KR:OFF_KR + LANES], cos_t, sin_t)
    low_half = lax.broadcasted_iota(jnp.int32, k_rope.shape, 1) < ROPE_DIM
    k_rope_even = jnp.where(low_half, k_rope, 0.0).astype(BF16)
    k_rope_odd = jnp.where(low_half, 0.0, k_rope).astype(BF16)
    rope0 = N_HEADS * NOPE_DIM
    for hd in range(N_HEADS):
        q_ref[hd, :, 0:LANES] = q[:, hd * NOPE_DIM:(hd + 1) * NOPE_DIM].astype(BF16)
        k_ref[hd, :, 0:LANES] = k_nope[:, hd * NOPE_DIM:(hd + 1) * NOPE_DIM].astype(BF16)
        k_ref[hd, :, LANES:QK_PAD] = k_rope_odd if hd % 2 else k_rope_even
        vt_ref[hd] = v_t[hd * V_DIM:(hd + 1) * V_DIM, :].astype(BF16)
    for pair in range(N_HEADS // 2):
        qr = _rope(q[:, rope0 + pair * LANES:rope0 + (pair + 1) * LANES], cos_t, sin_t).astype(BF16)
        q_ref[2 * pair, :, LANES:QK_PAD] = qr
        q_ref[2 * pair + 1, :, LANES:QK_PAD] = qr


def _mix_proj(h, layer, cos_t, sin_t, norm, w_in, conv_w, w_conv_out, q_norm, kv_norm, w_uq, w_k, w_vt):
    n_t = SEQ // TM_MIX
    tok = lambda w: pl.BlockSpec((None, TM_MIX, w), lambda b, i: (b, i, 0))
    head = pl.BlockSpec((None, N_HEADS, TM_MIX, QK_PAD), lambda b, i: (b, 0, i, 0))
    return pl.pallas_call(
        _mix_proj_kernel,
        grid=(BATCH, n_t),
        in_specs=[tok(D_MODEL), tok(LANES), tok(LANES),
                  _const_spec((1, D_MODEL), layer), _const_spec((D_MODEL, IN_COLS_PAD), layer),
                  _const_spec((CONV_K, CONV_DIM), layer), _const_spec((CONV_DIM, D_MODEL), layer),
                  _const_spec((1, Q_LORA), layer), _const_spec((1, KV_LORA), layer),
                  _const_spec((Q_LORA, N_HEADS * QK_DIM), layer),
                  _const_spec((KV_LORA, N_HEADS * NOPE_DIM), layer),
                  _const_spec((N_HEADS * V_DIM, KV_LORA), layer)],
        out_specs=[head, head,
                   pl.BlockSpec((None, N_HEADS, None, V_DIM, TM_MIX), lambda b, i: (b, 0, i, 0, 0)),
                   tok(D_MODEL), tok(D_MODEL)],
        out_shape=[jax.ShapeDtypeStruct((BATCH, N_HEADS, SEQ, QK_PAD), BF16),
                   jax.ShapeDtypeStruct((BATCH, N_HEADS, SEQ, QK_PAD), BF16),
                   jax.ShapeDtypeStruct((BATCH, N_HEADS, SEQ // TK, V_DIM, TK), BF16),
                   jax.ShapeDtypeStruct((BATCH, SEQ, D_MODEL), F32),
                   jax.ShapeDtypeStruct((BATCH, SEQ, D_MODEL), F32)],
        scratch_shapes=[pltpu.VMEM((TM_MIX + CONV_HALO, CONV_DIM), F32)],
        compiler_params=pltpu.CompilerParams(dimension_semantics=("arbitrary", "arbitrary"),
                                             vmem_limit_bytes=VMEM_LIMIT),
        name="mix_proj",
    )(h, cos_t, sin_t, norm, w_in, conv_w, w_conv_out, q_norm, kv_norm, w_uq, w_k, w_vt)


def _attn_kernel(q_ref, k_ref, vt_ref, o_ref):
    key_chunk = lax.broadcasted_iota(jnp.int32, (TK, TQ), 0) // CHUNK
    qry_chunk = lax.broadcasted_iota(jnp.int32, (TK, TQ), 1) // CHUNK
    diag_mask = key_chunk <= qry_chunk

    def step(q, j, carry, mask):
        m, l, acc = carry
        k0 = pl.multiple_of(j * TK, TK)
        s = _dot_nt(k_ref[pl.ds(k0, TK), :], q)
        if mask is not None:
            s = jnp.where(mask, s, -1e30)
        m_new = jnp.maximum(m, jnp.max(s, axis=0, keepdims=True))
        alpha = jnp.exp(m - m_new)
        p = jnp.exp(s - m_new)
        l = alpha * l + jnp.sum(p, axis=0, keepdims=True)
        acc = alpha * acc + _dot(vt_ref[j], p.astype(BF16))
        return m_new, l, acc

    def q_tile(i, _):
        q0 = pl.multiple_of(i * TQ, TQ)
        q = q_ref[pl.ds(q0, TQ), :]
        init = (jnp.full((1, TQ), -1e30, F32), jnp.zeros((1, TQ), F32), jnp.zeros((V_DIM, TQ), F32))
        carry = lax.fori_loop(0, i, lambda j, c: step(q, j, c, None), init)
        _, l, acc = step(q, i, carry, diag_mask)
        o_ref[pl.ds(q0, TQ), :] = (acc / l).T.astype(BF16)
        return 0

    lax.fori_loop(0, SEQ // TQ, q_tile, 0)


def _attention(q, k, vt):
    qk = pl.BlockSpec((None, None, SEQ, QK_PAD), lambda b, h: (b, h, 0, 0))
    return pl.pallas_call(
        _attn_kernel,
        grid=(BATCH, N_HEADS),
        in_specs=[qk, qk,
                  pl.BlockSpec((None, None, SEQ // TK, V_DIM, TK), lambda b, h: (b, h, 0, 0, 0))],
        out_specs=pl.BlockSpec((None, SEQ, V_DIM), lambda b, h: (b, 0, h)),
        out_shape=jax.ShapeDtypeStruct((BATCH, SEQ, N_HEADS * V_DIM), BF16),
        compiler_params=pltpu.CompilerParams(dimension_semantics=("arbitrary", "arbitrary"),
                                             vmem_limit_bytes=VMEM_LIMIT),
        name="attention",
    )(q, k, vt)


def _mix_out_kernel(h_ref, o_ref, gc_ref, sg_ref, wmo_ref, wo_ref, out_ref):
    y_mla = _dot(o_ref[...], wmo_ref[...])
    merged = (gc_ref[...] + sg_ref[...] * y_mla).astype(BF16)
    out_ref[...] = h_ref[...] + _dot(merged, wo_ref[...])


def _mix_out(h, o, gc, sg, layer, w_mla_out, w_o):
    tile = pl.BlockSpec((TM_FFN, D_MODEL), lambda i: (i, 0))
    return pl.pallas_call(
        _mix_out_kernel,
        grid=(TOKENS // TM_FFN,),
        in_specs=[tile, tile, tile, tile,
                  _const_spec((D_MODEL, D_MODEL), layer), _const_spec((D_MODEL, D_MODEL), layer)],
        out_specs=tile,
        out_shape=jax.ShapeDtypeStruct((TOKENS, D_MODEL), F32),
        compiler_params=pltpu.CompilerParams(dimension_semantics=("arbitrary",),
                                             vmem_limit_bytes=VMEM_LIMIT),
        name="mix_out",
    )(h, o, gc, sg, w_mla_out, w_o)


def _rope_tables(positions):
    inv_freq = ROPE_THETA ** (-jnp.arange(0, ROPE_DIM, 2, dtype=F32) / ROPE_DIM)
    ang = positions.astype(F32)[..., None] * inv_freq
    cos, sin = jnp.cos(ang), jnp.sin(ang)
    return (jnp.concatenate([cos, cos, cos, cos], axis=-1),
            jnp.concatenate([-sin, sin, -sin, sin], axis=-1))


def kernel(x, p, positions, ffn1_norm, ffn1_w_gu, ffn1_w_down, mix_norm, w_in, conv_w, w_conv_out,
           q_norm, kv_norm, w_uq, w_ukv, w_mla_out, w_o, ffn2_norm, ffn2_w_gu, ffn2_w_down,
           ple_norm, w_ple_gate, w_ple_proj, final_norm):
    assert x.shape == (BATCH, SEQ, D_MODEL) and p.shape == (DEPTH, BATCH, SEQ, PLE_DIM)
    bf = lambda w: w.astype(BF16)
    row = lambda g: g.reshape(DEPTH, 1, -1)

    k_rot = w_in[:, :, OFF_KR:OFF_KR + ROPE_DIM]
    w_in_r = bf(jnp.concatenate([w_in[:, :, :OFF_KR], k_rot, k_rot, w_in[:, :, OFF_KR + ROPE_DIM:]], axis=-1))
    w_uq_h = w_uq.reshape(DEPTH, Q_LORA, N_HEADS, QK_DIM)
    w_uq_r = bf(jnp.concatenate([w_uq_h[..., :NOPE_DIM].reshape(DEPTH, Q_LORA, -1),
                                 w_uq_h[..., NOPE_DIM:].reshape(DEPTH, Q_LORA, -1)], axis=-1))
    w_ukv_h = w_ukv.reshape(DEPTH, KV_LORA, N_HEADS, NOPE_DIM + V_DIM)
    w_k = bf(w_ukv_h[..., :NOPE_DIM].reshape(DEPTH, KV_LORA, -1))
    w_vt = bf(jnp.swapaxes(w_ukv_h[..., NOPE_DIM:].reshape(DEPTH, KV_LORA, -1), 1, 2))
    ffn1_gu, ffn1_dn, ffn2_gu, ffn2_dn = bf(ffn1_w_gu), bf(ffn1_w_down), bf(ffn2_w_gu), bf(ffn2_w_down)
    w_co, w_mo, w_ob, w_pg, w_pp = bf(w_conv_out), bf(w_mla_out), bf(w_o), bf(w_ple_gate), bf(w_ple_proj)
    cos_t, sin_t = _rope_tables(positions)
    p2 = p.reshape(DEPTH, TOKENS, PLE_DIM)

    h = x.reshape(TOKENS, D_MODEL)
    for layer in range(DEPTH):
        h = _ffn(h, layer, row(ffn1_norm), ffn1_gu, ffn1_dn)
        q, k, vt, gc, sg = _mix_proj(h.reshape(BATCH, SEQ, D_MODEL), layer, cos_t, sin_t, row(mix_norm),
                                     w_in_r, conv_w, w_co, row(q_norm), row(kv_norm), w_uq_r, w_k, w_vt)
        o = _attention(q, k, vt)
        h = _mix_out(h, o.reshape(TOKENS, D_MODEL), gc.reshape(TOKENS, D_MODEL),
                     sg.reshape(TOKENS, D_MODEL), layer, w_mo, w_ob)
        h = _ffn(h, layer, row(ffn2_norm), ffn2_gu, ffn2_dn,
                 ple=(p2, row(ple_norm), w_pg, w_pp),
                 final_norm=final_norm.reshape(1, D_MODEL) if layer == DEPTH - 1 else None)
    return h.reshape(BATCH, SEQ, D_MODEL)
```

```python
import functools

import jax
import jax.numpy as jnp
from jax import lax
from jax.experimental import pallas as pl
from jax.experimental.pallas import tpu as pltpu

D_MODEL = 1024
BATCH = 8
SEQ = 4096
DEPTH = 4
CHUNK = 64
PLE_DIM = 256
D_FF = 2816
CONV_DIM = 512
CONV_K = 3
N_HEADS = 8
NOPE_DIM = 128
ROPE_DIM = 64
V_DIM = 128
Q_LORA = 384
KV_LORA = 256
ROPE_THETA = 10000.0
EPS = 1e-6
QK_DIM = NOPE_DIM + ROPE_DIM
ATTN_SCALE = QK_DIM ** -0.5
TOKENS = BATCH * SEQ

LANES = 128
QK_PAD = 2 * LANES
OFF_B, OFF_C, OFF_V = 0, CONV_DIM, 2 * CONV_DIM
OFF_Q = 3 * CONV_DIM
OFF_KV = OFF_Q + Q_LORA
OFF_KR = OFF_KV + KV_LORA
OFF_GC = OFF_KR + LANES
OFF_GM = OFF_GC + D_MODEL
IN_COLS_PAD = OFF_GM + D_MODEL

TM_FFN = 512
TM_MIX = 256
TQ = 512
TK = 512
HEAD_GROUP = 4
VT_BLK = TM_MIX
VT_PER_TK = TK // VT_BLK
LOG2E = 1.4426950408889634
CONV_HALO = 8
VMEM_LIMIT = 56 * 1024 * 1024

F32 = jnp.float32
BF16 = jnp.bfloat16


def _rmsnorm(x, g):
    return x * lax.rsqrt(jnp.mean(x * x, axis=-1, keepdims=True) + EPS) * g


def _dot(a, b):
    return jnp.dot(a, b, preferred_element_type=F32)


def _dot_nt(a, b):
    return lax.dot_general(a, b, (((1,), (1,)), ((), ())), preferred_element_type=F32)


def _const_spec(shape, layer):
    zeros = (0,) * len(shape)
    return pl.BlockSpec((None,) + tuple(shape), lambda *_: (layer,) + zeros,
                        pipeline_mode=pl.Buffered(1))


def _ffn_kernel(*refs, with_ple, with_final):
    x_ref, g_ref, wgu_ref, wd_ref = refs[:4]
    o_ref = refs[-1]
    x = x_ref[...]
    xn = _rmsnorm(x, g_ref[...]).astype(BF16)
    gu = _dot(xn, wgu_ref[...])
    g, u = gu[:, :D_FF], gu[:, D_FF:]
    a = (g * jax.nn.sigmoid(g) * u).astype(BF16)
    h = x + 0.5 * _dot(a, wd_ref[...])
    if with_ple:
        p_ref, pg_ref, wpg_ref, wpp_ref = refs[4:8]
        hn = _rmsnorm(h, pg_ref[...]).astype(BF16)
        gate = jax.nn.sigmoid(_dot(hn, wpg_ref[...]))
        h = h + gate * _dot(p_ref[...].astype(BF16), wpp_ref[...])
    if with_final:
        h = _rmsnorm(h, refs[8][...])
    o_ref[...] = h


def _ffn(h, layer, norm, w_gu, w_down, ple=None, final_norm=None):
    tile = pl.BlockSpec((TM_FFN, D_MODEL), lambda i: (i, 0))
    in_specs = [tile, _const_spec((1, D_MODEL), layer),
                _const_spec((D_MODEL, 2 * D_FF), layer), _const_spec((D_FF, D_MODEL), layer)]
    args = [h, norm, w_gu, w_down]
    if ple is not None:
        p, ple_norm, w_pg, w_pp = ple
        in_specs += [pl.BlockSpec((None, TM_FFN, PLE_DIM), lambda i: (layer, i, 0)),
                     _const_spec((1, D_MODEL), layer),
                     _const_spec((D_MODEL, D_MODEL), layer), _const_spec((PLE_DIM, D_MODEL), layer)]
        args += [p, ple_norm, w_pg, w_pp]
    if final_norm is not None:
        in_specs.append(pl.BlockSpec((1, D_MODEL), lambda i: (0, 0)))
        args.append(final_norm)
    return pl.pallas_call(
        functools.partial(_ffn_kernel, with_ple=ple is not None, with_final=final_norm is not None),
        grid=(TOKENS // TM_FFN,),
        in_specs=in_specs,
        out_specs=tile,
        out_shape=jax.ShapeDtypeStruct((TOKENS, D_MODEL), F32),
        compiler_params=pltpu.CompilerParams(dimension_semantics=("arbitrary",),
                                             vmem_limit_bytes=VMEM_LIMIT),
        name="ffn_ple" if ple is not None else "ffn",
    )(*args)


def _rope(x, cos_t, sin_t):
    width = x.shape[-1]
    first_half = (lax.broadcasted_iota(jnp.int32, x.shape, 1) % ROPE_DIM) < ROPE_DIM // 2
    swapped = jnp.where(first_half, pltpu.roll(x, width - ROPE_DIM // 2, 1),
                        pltpu.roll(x, ROPE_DIM // 2, 1))
    return x * cos_t + swapped * sin_t


def _mix_proj_kernel(h_ref, cos_ref, sin_ref, g_ref, win_ref, cw_ref, wco_ref, qg_ref, kvg_ref,
                     wuq_ref, wk_ref, wvt_ref,
                     q_ref, k_ref, vt_ref, gc_ref, sg_ref, z_ref):
    i = pl.program_id(1)
    u = _rmsnorm(h_ref[...], g_ref[...]).astype(BF16)
    proj = _dot(u, win_ref[...])

    @pl.when(i == 0)
    def _():
        z_ref[0:CONV_HALO, :] = jnp.zeros((CONV_HALO, CONV_DIM), F32)

    z = proj[:, OFF_C:OFF_C + CONV_DIM] * proj[:, OFF_V:OFF_V + CONV_DIM]
    z_ref[CONV_HALO:, :] = z
    cw = cw_ref[...]
    y = (cw[0:1] * z_ref[CONV_HALO - 2:CONV_HALO - 2 + TM_MIX, :]
         + cw[1:2] * z_ref[CONV_HALO - 1:CONV_HALO - 1 + TM_MIX, :]
         + cw[2:3] * z)
    z_ref[0:CONV_HALO, :] = z_ref[TM_MIX:TM_MIX + CONV_HALO, :]
    y_conv = _dot((proj[:, OFF_B:OFF_B + CONV_DIM] * y).astype(BF16), wco_ref[...])
    gc_ref[...] = jax.nn.sigmoid(proj[:, OFF_GC:OFF_GC + D_MODEL]) * y_conv
    sg_ref[...] = jax.nn.sigmoid(proj[:, OFF_GM:OFF_GM + D_MODEL])

    cos_t, sin_t = cos_ref[...], sin_ref[...]
    qn = _rmsnorm(proj[:, OFF_Q:OFF_Q + Q_LORA], qg_ref[...]).astype(BF16)
    q = _dot(qn, wuq_ref[...]) * (ATTN_SCALE * LOG2E)
    kvn = _rmsnorm(proj[:, OFF_KV:OFF_KV + KV_LORA], kvg_ref[...]).astype(BF16)
    k_nope = _dot(kvn, wk_ref[...])
    v_t = _dot_nt(wvt_ref[...], kvn)
    k_rope = _rope(proj[:, OFF_KR:OFF_KR + LANES], cos_t, sin_t)
    low_half = lax.broadcasted_iota(jnp.int32, k_rope.shape, 1) < ROPE_DIM
    k_rope_even = jnp.where(low_half, k_rope, 0.0).astype(BF16)
    k_rope_odd = jnp.where(low_half, 0.0, k_rope).astype(BF16)
    rope0 = N_HEADS * NOPE_DIM
    for hd in range(N_HEADS):
        q_ref[hd, :, 0:LANES] = q[:, hd * NOPE_DIM:(hd + 1) * NOPE_DIM].astype(BF16)
        k_ref[hd, :, 0:LANES] = k_nope[:, hd * NOPE_DIM:(hd + 1) * NOPE_DIM].astype(BF16)
        k_ref[hd, :, LANES:QK_PAD] = k_rope_odd if hd % 2 else k_rope_even
        vt_ref[hd] = v_t[hd * V_DIM:(hd + 1) * V_DIM, :].astype(BF16)
    for pair in range(N_HEADS // 2):
        qr = _rope(q[:, rope0 + pair * LANES:rope0 + (pair + 1) * LANES], cos_t, sin_t).astype(BF16)
        q_ref[2 * pair, :, LANES:QK_PAD] = qr
        q_ref[2 * pair + 1, :, LANES:QK_PAD] = qr


def _mix_proj(h, layer, cos_t, sin_t, norm, w_in, conv_w, w_conv_out, q_norm, kv_norm, w_uq, w_k, w_vt):
    n_t = SEQ // TM_MIX
    tok = lambda w: pl.BlockSpec((None, TM_MIX, w), lambda b, i: (b, i, 0))
    head = pl.BlockSpec((None, N_HEADS, TM_MIX, QK_PAD), lambda b, i: (b, 0, i, 0))
    return pl.pallas_call(
        _mix_proj_kernel,
        grid=(BATCH, n_t),
        in_specs=[tok(D_MODEL), tok(LANES), tok(LANES),
                  _const_spec((1, D_MODEL), layer), _const_spec((D_MODEL, IN_COLS_PAD), layer),
                  _const_spec((CONV_K, CONV_DIM), layer), _const_spec((CONV_DIM, D_MODEL), layer),
                  _const_spec((1, Q_LORA), layer), _const_spec((1, KV_LORA), layer),
                  _const_spec((Q_LORA, N_HEADS * QK_DIM), layer),
                  _const_spec((KV_LORA, N_HEADS * NOPE_DIM), layer),
                  _const_spec((N_HEADS * V_DIM, KV_LORA), layer)],
        out_specs=[head, head,
                   pl.BlockSpec((None, N_HEADS, None, V_DIM, TM_MIX), lambda b, i: (b, 0, i, 0, 0)),
                   tok(D_MODEL), tok(D_MODEL)],
        out_shape=[jax.ShapeDtypeStruct((BATCH, N_HEADS, SEQ, QK_PAD), BF16),
                   jax.ShapeDtypeStruct((BATCH, N_HEADS, SEQ, QK_PAD), BF16),
                   jax.ShapeDtypeStruct((BATCH, N_HEADS, SEQ // VT_BLK, V_DIM, VT_BLK), BF16),
                   jax.ShapeDtypeStruct((BATCH, SEQ, D_MODEL), F32),
                   jax.ShapeDtypeStruct((BATCH, SEQ, D_MODEL), F32)],
        scratch_shapes=[pltpu.VMEM((TM_MIX + CONV_HALO, CONV_DIM), F32)],
        compiler_params=pltpu.CompilerParams(dimension_semantics=("arbitrary", "arbitrary"),
                                             vmem_limit_bytes=VMEM_LIMIT),
        name="mix_proj",
    )(h, cos_t, sin_t, norm, w_in, conv_w, w_conv_out, q_norm, kv_norm, w_uq, w_k, w_vt)


def _attn_kernel(q_ref, k_ref, vt_ref, o_ref, m_ref, l_ref, acc_ref):
    i = pl.program_id(2)
    n_full = (i * TQ) // TK
    key_chunk = (n_full * TK + lax.broadcasted_iota(jnp.int32, (TK, TQ), 0)) // CHUNK
    qry_chunk = (i * TQ + lax.broadcasted_iota(jnp.int32, (TK, TQ), 1)) // CHUNK
    diag_mask = key_chunk <= qry_chunk

    heads = range(HEAD_GROUP)
    m_ref[...] = jnp.full(m_ref.shape, -1e30, F32)
    l_ref[...] = jnp.zeros(l_ref.shape, F32)
    acc_ref[...] = jnp.zeros(acc_ref.shape, F32)

    def step(j, mask):
        k0 = pl.multiple_of(j * TK, TK)
        ss = [_dot_nt(k_ref[hd, pl.ds(k0, TK), :], q_ref[hd]) for hd in heads]
        if mask is not None:
            ss = [jnp.where(mask, s, -1e30) for s in ss]
        ps, alphas = [], []
        for hd in heads:
            m = m_ref[hd]
            m_new = jnp.maximum(m, jnp.max(ss[hd], axis=0, keepdims=True))
            alpha = jnp.exp2(m - m_new)
            p = jnp.exp2(ss[hd] - m_new)
            l_ref[hd] = alpha * l_ref[hd] + jnp.sum(p, axis=0, keepdims=True)
            m_ref[hd] = m_new
            ps.append(p.astype(BF16))
            alphas.append(alpha)
        for hd in heads:
            pv = _dot(vt_ref[hd, j * VT_PER_TK], ps[hd][0:VT_BLK])
            for c in range(1, VT_PER_TK):
                pv += _dot(vt_ref[hd, j * VT_PER_TK + c], ps[hd][c * VT_BLK:(c + 1) * VT_BLK])
            acc_ref[hd] = alphas[hd] * acc_ref[hd] + pv

    @pl.loop(0, n_full)
    def _(j):
        step(j, None)

    step(n_full, diag_mask)
    for hd in heads:
        o_ref[:, hd * V_DIM:(hd + 1) * V_DIM] = (acc_ref[hd] / l_ref[hd]).T.astype(BF16)


def _attention(q, k, vt):
    return pl.pallas_call(
        _attn_kernel,
        grid=(BATCH, N_HEADS // HEAD_GROUP, SEQ // TQ),
        in_specs=[pl.BlockSpec((None, HEAD_GROUP, TQ, QK_PAD), lambda b, g, i: (b, g, i, 0)),
                  pl.BlockSpec((None, HEAD_GROUP, SEQ, QK_PAD), lambda b, g, i: (b, g, 0, 0)),
                  pl.BlockSpec((None, HEAD_GROUP, SEQ // VT_BLK, V_DIM, VT_BLK),
                               lambda b, g, i: (b, g, 0, 0, 0))],
        out_specs=pl.BlockSpec((None, TQ, HEAD_GROUP * V_DIM), lambda b, g, i: (b, i, g)),
        out_shape=jax.ShapeDtypeStruct((BATCH, SEQ, N_HEADS * V_DIM), BF16),
        scratch_shapes=[pltpu.VMEM((HEAD_GROUP, 1, TQ), F32), pltpu.VMEM((HEAD_GROUP, 1, TQ), F32),
                        pltpu.VMEM((HEAD_GROUP, V_DIM, TQ), F32)],
        compiler_params=pltpu.CompilerParams(dimension_semantics=("arbitrary",) * 3,
                                             vmem_limit_bytes=VMEM_LIMIT),
        name="attention",
    )(q, k, vt)


def _mix_out_kernel(h_ref, o_ref, gc_ref, sg_ref, wmo_ref, wo_ref, out_ref):
    y_mla = _dot(o_ref[...], wmo_ref[...])
    merged = (gc_ref[...] + sg_ref[...] * y_mla).astype(BF16)
    out_ref[...] = h_ref[...] + _dot(merged, wo_ref[...])


def _mix_out(h, o, gc, sg, layer, w_mla_out, w_o):
    tile = pl.BlockSpec((TM_FFN, D_MODEL), lambda i: (i, 0))
    return pl.pallas_call(
        _mix_out_kernel,
        grid=(TOKENS // TM_FFN,),
        in_specs=[tile, tile, tile, tile,
                  _const_spec((D_MODEL, D_MODEL), layer), _const_spec((D_MODEL, D_MODEL), layer)],
        out_specs=tile,
        out_shape=jax.ShapeDtypeStruct((TOKENS, D_MODEL), F32),
        compiler_params=pltpu.CompilerParams(dimension_semantics=("arbitrary",),
                                             vmem_limit_bytes=VMEM_LIMIT),
        name="mix_out",
    )(h, o, gc, sg, w_mla_out, w_o)


def _rope_tables(positions):
    inv_freq = ROPE_THETA ** (-jnp.arange(0, ROPE_DIM, 2, dtype=F32) / ROPE_DIM)
    ang = positions.astype(F32)[..., None] * inv_freq
    cos, sin = jnp.cos(ang), jnp.sin(ang)
    return (jnp.concatenate([cos, cos, cos, cos], axis=-1),
            jnp.concatenate([-sin, sin, -sin, sin], axis=-1))


def kernel(x, p, positions, ffn1_norm, ffn1_w_gu, ffn1_w_down, mix_norm, w_in, conv_w, w_conv_out,
           q_norm, kv_norm, w_uq, w_ukv, w_mla_out, w_o, ffn2_norm, ffn2_w_gu, ffn2_w_down,
           ple_norm, w_ple_gate, w_ple_proj, final_norm):
    assert x.shape == (BATCH, SEQ, D_MODEL) and p.shape == (DEPTH, BATCH, SEQ, PLE_DIM)
    bf = lambda w: w.astype(BF16)
    row = lambda g: g.reshape(DEPTH, 1, -1)

    k_rot = w_in[:, :, OFF_KR:OFF_KR + ROPE_DIM]
    w_in_r = bf(jnp.concatenate([w_in[:, :, :OFF_KR], k_rot, k_rot, w_in[:, :, OFF_KR + ROPE_DIM:]], axis=-1))
    w_uq_h = w_uq.reshape(DEPTH, Q_LORA, N_HEADS, QK_DIM)
    w_uq_r = bf(jnp.concatenate([w_uq_h[..., :NOPE_DIM].reshape(DEPTH, Q_LORA, -1),
                                 w_uq_h[..., NOPE_DIM:].reshape(DEPTH, Q_LORA, -1)], axis=-1))
    w_ukv_h = w_ukv.reshape(DEPTH, KV_LORA, N_HEADS, NOPE_DIM + V_DIM)
    w_k = bf(w_ukv_h[..., :NOPE_DIM].reshape(DEPTH, KV_LORA, -1))
    w_vt = bf(jnp.swapaxes(w_ukv_h[..., NOPE_DIM:].reshape(DEPTH, KV_LORA, -1), 1, 2))
    ffn1_gu, ffn1_dn, ffn2_gu, ffn2_dn = bf(ffn1_w_gu), bf(ffn1_w_down), bf(ffn2_w_gu), bf(ffn2_w_down)
    w_co, w_mo, w_ob, w_pg, w_pp = bf(w_conv_out), bf(w_mla_out), bf(w_o), bf(w_ple_gate), bf(w_ple_proj)
    cos_t, sin_t = _rope_tables(positions)
    p2 = p.reshape(DEPTH, TOKENS, PLE_DIM)

    h = x.reshape(TOKENS, D_MODEL)
    for layer in range(DEPTH):
        h = _ffn(h, layer, row(ffn1_norm), ffn1_gu, ffn1_dn)
        q, k, vt, gc, sg = _mix_proj(h.reshape(BATCH, SEQ, D_MODEL), layer, cos_t, sin_t, row(mix_norm),
                                     w_in_r, conv_w, w_co, row(q_norm), row(kv_norm), w_uq_r, w_k, w_vt)
        o = _attention(q, k, vt)
        h = _mix_out(h, o.reshape(TOKENS, D_MODEL), gc.reshape(TOKENS, D_MODEL),
                     sg.reshape(TOKENS, D_MODEL), layer, w_mo, w_ob)
        h = _ffn(h, layer, row(ffn2_norm), ffn2_gu, ffn2_dn,
                 ple=(p2, row(ple_norm), w_pg, w_pp),
                 final_norm=final_norm.reshape(1, D_MODEL) if layer == DEPTH - 1 else None)
    return h.reshape(BATCH, SEQ, D_MODEL)
```

```python
import functools

import jax
import jax.numpy as jnp
from jax import lax
from jax.experimental import pallas as pl
from jax.experimental.pallas import tpu as pltpu

D_MODEL = 1024
BATCH = 8
SEQ = 4096
DEPTH = 4
CHUNK = 64
PLE_DIM = 256
D_FF = 2816
CONV_DIM = 512
CONV_K = 3
N_HEADS = 8
NOPE_DIM = 128
ROPE_DIM = 64
V_DIM = 128
Q_LORA = 384
KV_LORA = 256
ROPE_THETA = 10000.0
EPS = 1e-6
QK_DIM = NOPE_DIM + ROPE_DIM
ATTN_SCALE = QK_DIM ** -0.5
TOKENS = BATCH * SEQ

LANES = 128
QK_PAD = 2 * LANES
OFF_B, OFF_C, OFF_V = 0, CONV_DIM, 2 * CONV_DIM
OFF_Q = 3 * CONV_DIM
OFF_KV = OFF_Q + Q_LORA
OFF_KR = OFF_KV + KV_LORA
OFF_GC = OFF_KR + LANES
OFF_GM = OFF_GC + D_MODEL
IN_COLS_PAD = OFF_GM + D_MODEL

TM_FFN = 512
TM_MIX = 256
TQ = 512
TK = 512
HEAD_GROUP = 4
VT_BLK = TM_MIX
VT_PER_TK = TK // VT_BLK
LOG2E = 1.4426950408889634
CONV_HALO = 8
VMEM_LIMIT = 56 * 1024 * 1024

F32 = jnp.float32
BF16 = jnp.bfloat16


def _rmsnorm(x, g):
    return x * lax.rsqrt(jnp.mean(x * x, axis=-1, keepdims=True) + EPS) * g


def _dot(a, b):
    return jnp.dot(a, b, preferred_element_type=F32)


def _dot_nt(a, b):
    return lax.dot_general(a, b, (((1,), (1,)), ((), ())), preferred_element_type=F32)


def _const_spec(shape, layer):
    zeros = (0,) * len(shape)
    return pl.BlockSpec((None,) + tuple(shape), lambda *_: (layer,) + zeros,
                        pipeline_mode=pl.Buffered(1))


def _ffn_kernel(*refs, with_ple, with_final):
    x_ref, g_ref, wgu_ref, wd_ref = refs[:4]
    o_ref = refs[-1]
    x = x_ref[...]
    xn = _rmsnorm(x, g_ref[...]).astype(BF16)
    gu = _dot(xn, wgu_ref[...])
    g, u = gu[:, :D_FF], gu[:, D_FF:]
    a = (g * jax.nn.sigmoid(g) * u).astype(BF16)
    h = x + 0.5 * _dot(a, wd_ref[...])
    if with_ple:
        p_ref, pg_ref, wpg_ref, wpp_ref = refs[4:8]
        hn = _rmsnorm(h, pg_ref[...]).astype(BF16)
        gate = jax.nn.sigmoid(_dot(hn, wpg_ref[...]))
        h = h + gate * _dot(p_ref[...].astype(BF16), wpp_ref[...])
    if with_final:
        h = _rmsnorm(h, refs[8][...])
    o_ref[...] = h


def _ffn(h, layer, norm, w_gu, w_down, ple=None, final_norm=None):
    tile = pl.BlockSpec((TM_FFN, D_MODEL), lambda i: (i, 0))
    in_specs = [tile, _const_spec((1, D_MODEL), layer),
                _const_spec((D_MODEL, 2 * D_FF), layer), _const_spec((D_FF, D_MODEL), layer)]
    args = [h, norm, w_gu, w_down]
    if ple is not None:
        p, ple_norm, w_pg, w_pp = ple
        in_specs += [pl.BlockSpec((None, TM_FFN, PLE_DIM), lambda i: (layer, i, 0)),
                     _const_spec((1, D_MODEL), layer),
                     _const_spec((D_MODEL, D_MODEL), layer), _const_spec((PLE_DIM, D_MODEL), layer)]
        args += [p, ple_norm, w_pg, w_pp]
    if final_norm is not None:
        in_specs.append(pl.BlockSpec((1, D_MODEL), lambda i: (0, 0)))
        args.append(final_norm)
    return pl.pallas_call(
        functools.partial(_ffn_kernel, with_ple=ple is not None, with_final=final_norm is not None),
        grid=(TOKENS // TM_FFN,),
        in_specs=in_specs,
        out_specs=tile,
        out_shape=jax.ShapeDtypeStruct((TOKENS, D_MODEL), F32),
        compiler_params=pltpu.CompilerParams(dimension_semantics=("arbitrary",),
                                             vmem_limit_bytes=VMEM_LIMIT),
        name="ffn_ple" if ple is not None else "ffn",
    )(*args)


def _rope(x, cos_t, sin_t):
    width = x.shape[-1]
    first_half = (lax.broadcasted_iota(jnp.int32, x.shape, 1) % ROPE_DIM) < ROPE_DIM // 2
    swapped = jnp.where(first_half, pltpu.roll(x, width - ROPE_DIM // 2, 1),
                        pltpu.roll(x, ROPE_DIM // 2, 1))
    return x * cos_t + swapped * sin_t


def _mix_proj_kernel(h_ref, cos_ref, sin_ref, g_ref, win_ref, cw_ref, wco_ref, qg_ref, kvg_ref,
                     wuq_ref, wk_ref, wvt_ref,
                     q_ref, k_ref, vt_ref, gc_ref, sg_ref, z_ref):
    i = pl.program_id(1)
    u = _rmsnorm(h_ref[...], g_ref[...]).astype(BF16)
    proj = _dot(u, win_ref[...])

    @pl.when(i == 0)
    def _():
        z_ref[0:CONV_HALO, :] = jnp.zeros((CONV_HALO, CONV_DIM), F32)

    z = proj[:, OFF_C:OFF_C + CONV_DIM] * proj[:, OFF_V:OFF_V + CONV_DIM]
    z_ref[CONV_HALO:, :] = z
    cw = cw_ref[...]
    y = (cw[0:1] * z_ref[CONV_HALO - 2:CONV_HALO - 2 + TM_MIX, :]
         + cw[1:2] * z_ref[CONV_HALO - 1:CONV_HALO - 1 + TM_MIX, :]
         + cw[2:3] * z)
    z_ref[0:CONV_HALO, :] = z_ref[TM_MIX:TM_MIX + CONV_HALO, :]
    y_conv = _dot((proj[:, OFF_B:OFF_B + CONV_DIM] * y).astype(BF16), wco_ref[...])
    gc_ref[...] = jax.nn.sigmoid(proj[:, OFF_GC:OFF_GC + D_MODEL]) * y_conv
    sg_ref[...] = jax.nn.sigmoid(proj[:, OFF_GM:OFF_GM + D_MODEL])

    cos_t, sin_t = cos_ref[...], sin_ref[...]
    qn = _rmsnorm(proj[:, OFF_Q:OFF_Q + Q_LORA], qg_ref[...]).astype(BF16)
    q = _dot(qn, wuq_ref[...]) * (ATTN_SCALE * LOG2E)
    kvn = _rmsnorm(proj[:, OFF_KV:OFF_KV + KV_LORA], kvg_ref[...]).astype(BF16)
    k_nope = _dot(kvn, wk_ref[...])
    v_t = _dot_nt(wvt_ref[...], kvn)
    k_rope = _rope(proj[:, OFF_KR:OFF_KR + LANES], cos_t, sin_t)
    low_half = lax.broadcasted_iota(jnp.int32, k_rope.shape, 1) < ROPE_DIM
    k_rope_even = jnp.where(low_half, k_rope, 0.0).astype(BF16)
    k_rope_odd = jnp.where(low_half, 0.0, k_rope).astype(BF16)
    rope0 = N_HEADS * NOPE_DIM
    for hd in range(N_HEADS):
        q_ref[hd, :, 0:LANES] = q[:, hd * NOPE_DIM:(hd + 1) * NOPE_DIM].astype(BF16)
        k_ref[hd, :, 0:LANES] = k_nope[:, hd * NOPE_DIM:(hd + 1) * NOPE_DIM].astype(BF16)
        k_ref[hd, :, LANES:QK_PAD] = k_rope_odd if hd % 2 else k_rope_even
        vt_ref[hd] = v_t[hd * V_DIM:(hd + 1) * V_DIM, :].astype(BF16)
    for pair in range(N_HEADS // 2):
        qr = _rope(q[:, rope0 + pair * LANES:rope0 + (pair + 1) * LANES], cos_t, sin_t).astype(BF16)
        q_ref[2 * pair, :, LANES:QK_PAD] = qr
        q_ref[2 * pair + 1, :, LANES:QK_PAD] = qr


def _mix_proj(h, layer, cos_t, sin_t, norm, w_in, conv_w, w_conv_out, q_norm, kv_norm, w_uq, w_k, w_vt):
    n_t = SEQ // TM_MIX
    tok = lambda w: pl.BlockSpec((None, TM_MIX, w), lambda b, i: (b, i, 0))
    head = pl.BlockSpec((None, N_HEADS, TM_MIX, QK_PAD), lambda b, i: (b, 0, i, 0))
    return pl.pallas_call(
        _mix_proj_kernel,
        grid=(BATCH, n_t),
        in_specs=[tok(D_MODEL), tok(LANES), tok(LANES),
                  _const_spec((1, D_MODEL), layer), _const_spec((D_MODEL, IN_COLS_PAD), layer),
                  _const_spec((CONV_K, CONV_DIM), layer), _const_spec((CONV_DIM, D_MODEL), layer),
                  _const_spec((1, Q_LORA), layer), _const_spec((1, KV_LORA), layer),
                  _const_spec((Q_LORA, N_HEADS * QK_DIM), layer),
                  _const_spec((KV_LORA, N_HEADS * NOPE_DIM), layer),
                  _const_spec((N_HEADS * V_DIM, KV_LORA), layer)],
        out_specs=[head, head,
                   pl.BlockSpec((None, N_HEADS, None, V_DIM, TM_MIX), lambda b, i: (b, 0, i, 0, 0)),
                   tok(D_MODEL), tok(D_MODEL)],
        out_shape=[jax.ShapeDtypeStruct((BATCH, N_HEADS, SEQ, QK_PAD), BF16),
                   jax.ShapeDtypeStruct((BATCH, N_HEADS, SEQ, QK_PAD), BF16),
                   jax.ShapeDtypeStruct((BATCH, N_HEADS, SEQ // VT_BLK, V_DIM, VT_BLK), BF16),
                   jax.ShapeDtypeStruct((BATCH, SEQ, D_MODEL), F32),
                   jax.ShapeDtypeStruct((BATCH, SEQ, D_MODEL), F32)],
        scratch_shapes=[pltpu.VMEM((TM_MIX + CONV_HALO, CONV_DIM), F32)],
        compiler_params=pltpu.CompilerParams(dimension_semantics=("arbitrary", "arbitrary"),
                                             vmem_limit_bytes=VMEM_LIMIT),
        name="mix_proj",
    )(h, cos_t, sin_t, norm, w_in, conv_w, w_conv_out, q_norm, kv_norm, w_uq, w_k, w_vt)


def _attn_kernel(q_ref, k_ref, vt_ref, o_ref, m_ref, l_ref, acc_ref, s_ref, tmax_ref):
    i = pl.program_id(2)
    n_full = (i * TQ) // TK
    key_chunk = (n_full * TK + lax.broadcasted_iota(jnp.int32, (TK, TQ), 0)) // CHUNK
    qry_chunk = (i * TQ + lax.broadcasted_iota(jnp.int32, (TK, TQ), 1)) // CHUNK
    diag_mask = key_chunk <= qry_chunk

    heads = range(HEAD_GROUP)
    m_ref[...] = jnp.full(m_ref.shape, -1e30, F32)
    l_ref[...] = jnp.zeros(l_ref.shape, F32)
    acc_ref[...] = jnp.zeros(acc_ref.shape, F32)

    def scores(hd, j):
        k0 = pl.multiple_of(j * TK, TK)
        s = _dot_nt(k_ref[hd, pl.ds(k0, TK), :], q_ref[hd])
        s_ref[hd] = s
        tmax_ref[hd] = jnp.max(s, axis=0, keepdims=True)

    def softmax(hd, mask):
        s = s_ref[hd]
        if mask is None:
            tmax = tmax_ref[hd]
        else:
            s = jnp.where(mask, s, -1e30)
            tmax = jnp.max(s, axis=0, keepdims=True)
        m = m_ref[hd]
        m_new = jnp.maximum(m, tmax)
        alpha = jnp.exp2(m - m_new)
        p = jnp.exp2(s - m_new)
        l_ref[hd] = alpha * l_ref[hd] + jnp.sum(p, axis=0, keepdims=True)
        m_ref[hd] = m_new
        return p.astype(BF16), alpha

    def weighted_values(hd, j, p, alpha):
        pv = _dot(vt_ref[hd, j * VT_PER_TK], p[0:VT_BLK])
        for c in range(1, VT_PER_TK):
            pv += _dot(vt_ref[hd, j * VT_PER_TK + c], p[c * VT_BLK:(c + 1) * VT_BLK])
        acc_ref[hd] = alpha * acc_ref[hd] + pv

    for hd in heads:
        scores(hd, 0)

    @pl.loop(0, n_full)
    def _(j):
        for hd in heads:
            p, alpha = softmax(hd, None)
            scores(hd, j + 1)
            weighted_values(hd, j, p, alpha)

    for hd in heads:
        p, alpha = softmax(hd, diag_mask)
        weighted_values(hd, n_full, p, alpha)
    for hd in heads:
        o_ref[:, hd * V_DIM:(hd + 1) * V_DIM] = (acc_ref[hd] / l_ref[hd]).T.astype(BF16)


def _attention(q, k, vt):
    return pl.pallas_call(
        _attn_kernel,
        grid=(BATCH, N_HEADS // HEAD_GROUP, SEQ // TQ),
        in_specs=[pl.BlockSpec((None, HEAD_GROUP, TQ, QK_PAD), lambda b, g, i: (b, g, i, 0)),
                  pl.BlockSpec((None, HEAD_GROUP, SEQ, QK_PAD), lambda b, g, i: (b, g, 0, 0)),
                  pl.BlockSpec((None, HEAD_GROUP, SEQ // VT_BLK, V_DIM, VT_BLK),
                               lambda b, g, i: (b, g, 0, 0, 0))],
        out_specs=pl.BlockSpec((None, TQ, HEAD_GROUP * V_DIM), lambda b, g, i: (b, i, g)),
        out_shape=jax.ShapeDtypeStruct((BATCH, SEQ, N_HEADS * V_DIM), BF16),
        scratch_shapes=[pltpu.VMEM((HEAD_GROUP, 1, TQ), F32), pltpu.VMEM((HEAD_GROUP, 1, TQ), F32),
                        pltpu.VMEM((HEAD_GROUP, V_DIM, TQ), F32),
                        pltpu.VMEM((HEAD_GROUP, TK, TQ), F32), pltpu.VMEM((HEAD_GROUP, 1, TQ), F32)],
        compiler_params=pltpu.CompilerParams(dimension_semantics=("arbitrary",) * 3,
                                             vmem_limit_bytes=VMEM_LIMIT),
        name="attention",
    )(q, k, vt)


def _mix_out_kernel(h_ref, o_ref, gc_ref, sg_ref, wmo_ref, wo_ref, out_ref):
    y_mla = _dot(o_ref[...], wmo_ref[...])
    merged = (gc_ref[...] + sg_ref[...] * y_mla).astype(BF16)
    out_ref[...] = h_ref[...] + _dot(merged, wo_ref[...])


def _mix_out(h, o, gc, sg, layer, w_mla_out, w_o):
    tile = pl.BlockSpec((TM_FFN, D_MODEL), lambda i: (i, 0))
    return pl.pallas_call(
        _mix_out_kernel,
        grid=(TOKENS // TM_FFN,),
        in_specs=[tile, tile, tile, tile,
                  _const_spec((D_MODEL, D_MODEL), layer), _const_spec((D_MODEL, D_MODEL), layer)],
        out_specs=tile,
        out_shape=jax.ShapeDtypeStruct((TOKENS, D_MODEL), F32),
        compiler_params=pltpu.CompilerParams(dimension_semantics=("arbitrary",),
                                             vmem_limit_bytes=VMEM_LIMIT),
        name="mix_out",
    )(h, o, gc, sg, w_mla_out, w_o)


def _rope_tables(positions):
    inv_freq = ROPE_THETA ** (-jnp.arange(0, ROPE_DIM, 2, dtype=F32) / ROPE_DIM)
    ang = positions.astype(F32)[..., None] * inv_freq
    cos, sin = jnp.cos(ang), jnp.sin(ang)
    return (jnp.concatenate([cos, cos, cos, cos], axis=-1),
            jnp.concatenate([-sin, sin, -sin, sin], axis=-1))


def kernel(x, p, positions, ffn1_norm, ffn1_w_gu, ffn1_w_down, mix_norm, w_in, conv_w, w_conv_out,
           q_norm, kv_norm, w_uq, w_ukv, w_mla_out, w_o, ffn2_norm, ffn2_w_gu, ffn2_w_down,
           ple_norm, w_ple_gate, w_ple_proj, final_norm):
    assert x.shape == (BATCH, SEQ, D_MODEL) and p.shape == (DEPTH, BATCH, SEQ, PLE_DIM)
    bf = lambda w: w.astype(BF16)
    row = lambda g: g.reshape(DEPTH, 1, -1)

    k_rot = w_in[:, :, OFF_KR:OFF_KR + ROPE_DIM]
    w_in_r = bf(jnp.concatenate([w_in[:, :, :OFF_KR], k_rot, k_rot, w_in[:, :, OFF_KR + ROPE_DIM:]], axis=-1))
    w_uq_h = w_uq.reshape(DEPTH, Q_LORA, N_HEADS, QK_DIM)
    w_uq_r = bf(jnp.concatenate([w_uq_h[..., :NOPE_DIM].reshape(DEPTH, Q_LORA, -1),
                                 w_uq_h[..., NOPE_DIM:].reshape(DEPTH, Q_LORA, -1)], axis=-1))
    w_ukv_h = w_ukv.reshape(DEPTH, KV_LORA, N_HEADS, NOPE_DIM + V_DIM)
    w_k = bf(w_ukv_h[..., :NOPE_DIM].reshape(DEPTH, KV_LORA, -1))
    w_vt = bf(jnp.swapaxes(w_ukv_h[..., NOPE_DIM:].reshape(DEPTH, KV_LORA, -1), 1, 2))
    ffn1_gu, ffn1_dn, ffn2_gu, ffn2_dn = bf(ffn1_w_gu), bf(ffn1_w_down), bf(ffn2_w_gu), bf(ffn2_w_down)
    w_co, w_mo, w_ob, w_pg, w_pp = bf(w_conv_out), bf(w_mla_out), bf(w_o), bf(w_ple_gate), bf(w_ple_proj)
    cos_t, sin_t = _rope_tables(positions)
    p2 = p.reshape(DEPTH, TOKENS, PLE_DIM)

    h = x.reshape(TOKENS, D_MODEL)
    for layer in range(DEPTH):
        h = _ffn(h, layer, row(ffn1_norm), ffn1_gu, ffn1_dn)
        q, k, vt, gc, sg = _mix_proj(h.reshape(BATCH, SEQ, D_MODEL), layer, cos_t, sin_t, row(mix_norm),
                                     w_in_r, conv_w, w_co, row(q_norm), row(kv_norm), w_uq_r, w_k, w_vt)
        o = _attention(q, k, vt)
        h = _mix_out(h, o.reshape(TOKENS, D_MODEL), gc.reshape(TOKENS, D_MODEL),
                     sg.reshape(TOKENS, D_MODEL), layer, w_mo, w_ob)
        h = _ffn(h, layer, row(ffn2_norm), ffn2_gu, ffn2_dn,
                 ple=(p2, row(ple_norm), w_pg, w_pp),
                 final_norm=final_norm.reshape(1, D_MODEL) if layer == DEPTH - 1 else None)
    return h.reshape(BATCH, SEQ, D_MODEL)
```

```python
import functools

import jax
import jax.numpy as jnp
from jax import lax
from jax.experimental import pallas as pl
from jax.experimental.pallas import tpu as pltpu

D_MODEL = 1024
BATCH = 8
SEQ = 4096
DEPTH = 4
CHUNK = 64
PLE_DIM = 256
D_FF = 2816
CONV_DIM = 512
CONV_K = 3
N_HEADS = 8
NOPE_DIM = 128
ROPE_DIM = 64
V_DIM = 128
Q_LORA = 384
KV_LORA = 256
ROPE_THETA = 10000.0
EPS = 1e-6
QK_DIM = NOPE_DIM + ROPE_DIM
ATTN_SCALE = QK_DIM ** -0.5
TOKENS = BATCH * SEQ

LANES = 128
QK_PAD = 2 * LANES
OFF_B, OFF_C, OFF_V = 0, CONV_DIM, 2 * CONV_DIM
OFF_Q = 3 * CONV_DIM
OFF_KV = OFF_Q + Q_LORA
OFF_KR = OFF_KV + KV_LORA
OFF_GC = OFF_KR + LANES
OFF_GM = OFF_GC + D_MODEL
IN_COLS_PAD = OFF_GM + D_MODEL

TM_FFN = 512
TM_MIX = 512
TQ = 512
TK = 512
HEAD_GROUP = 4
VT_BLK = 256
VT_PER_TK = TK // VT_BLK
LOG2E = 1.4426950408889634
CONV_HALO = 8
VMEM_LIMIT = 56 * 1024 * 1024

F32 = jnp.float32
BF16 = jnp.bfloat16


def _rmsnorm(x, g):
    return x * lax.rsqrt(jnp.mean(x * x, axis=-1, keepdims=True) + EPS) * g


def _dot(a, b):
    return jnp.dot(a, b, preferred_element_type=F32)


def _dot_nt(a, b):
    return lax.dot_general(a, b, (((1,), (1,)), ((), ())), preferred_element_type=F32)


def _const_spec(shape, layer):
    zeros = (0,) * len(shape)
    return pl.BlockSpec((None,) + tuple(shape), lambda *_: (layer,) + zeros,
                        pipeline_mode=pl.Buffered(1))


def _ffn_kernel(*refs, with_ple, with_final):
    x_ref, g_ref, wgu_ref, wd_ref = refs[:4]
    o_ref = refs[-1]
    x = x_ref[...]
    xn = _rmsnorm(x, g_ref[...]).astype(BF16)
    gu = _dot(xn, wgu_ref[...])
    g, u = gu[:, :D_FF], gu[:, D_FF:]
    a = (g * jax.nn.sigmoid(g) * u).astype(BF16)
    h = x + 0.5 * _dot(a, wd_ref[...])
    if with_ple:
        p_ref, pg_ref, wpg_ref, wpp_ref = refs[4:8]
        hn = _rmsnorm(h, pg_ref[...]).astype(BF16)
        gate = jax.nn.sigmoid(_dot(hn, wpg_ref[...]))
        h = h + gate * _dot(p_ref[...].astype(BF16), wpp_ref[...])
    if with_final:
        h = _rmsnorm(h, refs[8][...])
    o_ref[...] = h


def _ffn(h, layer, norm, w_gu, w_down, ple=None, final_norm=None):
    tile = pl.BlockSpec((TM_FFN, D_MODEL), lambda i: (i, 0))
    in_specs = [tile, _const_spec((1, D_MODEL), layer),
                _const_spec((D_MODEL, 2 * D_FF), layer), _const_spec((D_FF, D_MODEL), layer)]
    args = [h, norm, w_gu, w_down]
    if ple is not None:
        p, ple_norm, w_pg, w_pp = ple
        in_specs += [pl.BlockSpec((None, TM_FFN, PLE_DIM), lambda i: (layer, i, 0)),
                     _const_spec((1, D_MODEL), layer),
                     _const_spec((D_MODEL, D_MODEL), layer), _const_spec((PLE_DIM, D_MODEL), layer)]
        args += [p, ple_norm, w_pg, w_pp]
    if final_norm is not None:
        in_specs.append(pl.BlockSpec((1, D_MODEL), lambda i: (0, 0)))
        args.append(final_norm)
    return pl.pallas_call(
        functools.partial(_ffn_kernel, with_ple=ple is not None, with_final=final_norm is not None),
        grid=(TOKENS // TM_FFN,),
        in_specs=in_specs,
        out_specs=tile,
        out_shape=jax.ShapeDtypeStruct((TOKENS, D_MODEL), F32),
        compiler_params=pltpu.CompilerParams(dimension_semantics=("arbitrary",),
                                             vmem_limit_bytes=VMEM_LIMIT),
        name="ffn_ple" if ple is not None else "ffn",
    )(*args)


def _rope(x, cos_t, sin_t):
    width = x.shape[-1]
    first_half = (lax.broadcasted_iota(jnp.int32, x.shape, 1) % ROPE_DIM) < ROPE_DIM // 2
    swapped = jnp.where(first_half, pltpu.roll(x, width - ROPE_DIM // 2, 1),
                        pltpu.roll(x, ROPE_DIM // 2, 1))
    return x * cos_t + swapped * sin_t


def _mix_proj_kernel(h_ref, cos_ref, sin_ref, g_ref, win_ref, cw_ref, wco_ref, qg_ref, kvg_ref,
                     wuq_ref, wk_ref, wvt_ref,
                     q_ref, k_ref, vt_ref, gc_ref, sg_ref, z_ref):
    groups = [slice(r * VT_BLK, (r + 1) * VT_BLK) for r in range(TM_MIX // VT_BLK)]

    @pl.when(pl.program_id(1) == 0)
    def _():
        z_ref[0:CONV_HALO, :] = jnp.zeros((CONV_HALO, CONV_DIM), F32)

    us = [_rmsnorm(h_ref[g, :], g_ref[...]).astype(BF16) for g in groups]
    projs = [_dot(u, win_ref[...]) for u in us]
    cw = cw_ref[...]
    rope0 = N_HEADS * NOPE_DIM
    for r, (g, proj) in enumerate(zip(groups, projs)):
        z0 = CONV_HALO + r * VT_BLK
        z = proj[:, OFF_C:OFF_C + CONV_DIM] * proj[:, OFF_V:OFF_V + CONV_DIM]
        z_ref[z0:z0 + VT_BLK, :] = z
        y = (cw[0:1] * z_ref[z0 - 2:z0 - 2 + VT_BLK, :] + cw[1:2] * z_ref[z0 - 1:z0 - 1 + VT_BLK, :]
             + cw[2:3] * z)
        y_conv = _dot((proj[:, OFF_B:OFF_B + CONV_DIM] * y).astype(BF16), wco_ref[...])
        gc_ref[g, :] = (jax.nn.sigmoid(proj[:, OFF_GC:OFF_GC + D_MODEL]) * y_conv).astype(BF16)
        sg_ref[g, :] = jax.nn.sigmoid(proj[:, OFF_GM:OFF_GM + D_MODEL]).astype(BF16)

        cos_t, sin_t = cos_ref[g, :], sin_ref[g, :]
        qn = _rmsnorm(proj[:, OFF_Q:OFF_Q + Q_LORA], qg_ref[...]).astype(BF16)
        q = _dot(qn, wuq_ref[...]) * (ATTN_SCALE * LOG2E)
        kvn = _rmsnorm(proj[:, OFF_KV:OFF_KV + KV_LORA], kvg_ref[...]).astype(BF16)
        k_nope = _dot(kvn, wk_ref[...])
        v_t = _dot_nt(wvt_ref[...], kvn)
        k_rope = _rope(proj[:, OFF_KR:OFF_KR + LANES], cos_t, sin_t)
        low_half = lax.broadcasted_iota(jnp.int32, k_rope.shape, 1) < ROPE_DIM
        k_rope_even = jnp.where(low_half, k_rope, 0.0).astype(BF16)
        k_rope_odd = jnp.where(low_half, 0.0, k_rope).astype(BF16)
        for hd in range(N_HEADS):
            q_ref[hd, g, 0:LANES] = q[:, hd * NOPE_DIM:(hd + 1) * NOPE_DIM].astype(BF16)
            k_ref[hd, g, 0:LANES] = k_nope[:, hd * NOPE_DIM:(hd + 1) * NOPE_DIM].astype(BF16)
            k_ref[hd, g, LANES:QK_PAD] = k_rope_odd if hd % 2 else k_rope_even
            vt_ref[hd, r] = v_t[hd * V_DIM:(hd + 1) * V_DIM, :].astype(BF16)
        for pair in range(N_HEADS // 2):
            qr = _rope(q[:, rope0 + pair * LANES:rope0 + (pair + 1) * LANES], cos_t, sin_t).astype(BF16)
            q_ref[2 * pair, g, LANES:QK_PAD] = qr
            q_ref[2 * pair + 1, g, LANES:QK_PAD] = qr
    z_ref[0:CONV_HALO, :] = z_ref[TM_MIX:TM_MIX + CONV_HALO, :]


def _mix_proj(h, layer, cos_t, sin_t, norm, w_in, conv_w, w_conv_out, q_norm, kv_norm, w_uq, w_k, w_vt):
    n_t = SEQ // TM_MIX
    tok = lambda w: pl.BlockSpec((None, TM_MIX, w), lambda b, i: (b, i, 0))
    head = pl.BlockSpec((None, N_HEADS, TM_MIX, QK_PAD), lambda b, i: (b, 0, i, 0))
    return pl.pallas_call(
        _mix_proj_kernel,
        grid=(BATCH, n_t),
        in_specs=[tok(D_MODEL), tok(LANES), tok(LANES),
                  _const_spec((1, D_MODEL), layer), _const_spec((D_MODEL, IN_COLS_PAD), layer),
                  _const_spec((CONV_K, CONV_DIM), layer), _const_spec((CONV_DIM, D_MODEL), layer),
                  _const_spec((1, Q_LORA), layer), _const_spec((1, KV_LORA), layer),
                  _const_spec((Q_LORA, N_HEADS * QK_DIM), layer),
                  _const_spec((KV_LORA, N_HEADS * NOPE_DIM), layer),
                  _const_spec((N_HEADS * V_DIM, KV_LORA), layer)],
        out_specs=[head, head,
                   pl.BlockSpec((None, N_HEADS, TM_MIX // VT_BLK, V_DIM, VT_BLK),
                                lambda b, i: (b, 0, i, 0, 0)),
                   tok(D_MODEL), tok(D_MODEL)],
        out_shape=[jax.ShapeDtypeStruct((BATCH, N_HEADS, SEQ, QK_PAD), BF16),
                   jax.ShapeDtypeStruct((BATCH, N_HEADS, SEQ, QK_PAD), BF16),
                   jax.ShapeDtypeStruct((BATCH, N_HEADS, SEQ // VT_BLK, V_DIM, VT_BLK), BF16),
                   jax.ShapeDtypeStruct((BATCH, SEQ, D_MODEL), BF16),
                   jax.ShapeDtypeStruct((BATCH, SEQ, D_MODEL), BF16)],
        scratch_shapes=[pltpu.VMEM((TM_MIX + CONV_HALO, CONV_DIM), F32)],
        compiler_params=pltpu.CompilerParams(dimension_semantics=("arbitrary", "arbitrary"),
                                             vmem_limit_bytes=VMEM_LIMIT),
        name="mix_proj",
    )(h, cos_t, sin_t, norm, w_in, conv_w, w_conv_out, q_norm, kv_norm, w_uq, w_k, w_vt)


def _attn_kernel(q_ref, k_ref, vt_ref, o_ref, m_ref, l_ref, acc_ref, s_ref, tmax_ref):
    n_full = pl.program_id(2)
    half = TQ // 2
    blk_mask = (lax.broadcasted_iota(jnp.int32, (half, half), 0) // CHUNK
                <= lax.broadcasted_iota(jnp.int32, (half, half), 1) // CHUNK)

    heads = range(HEAD_GROUP)
    m_ref[...] = jnp.full(m_ref.shape, -1e30, F32)
    l_ref[...] = jnp.zeros(l_ref.shape, F32)
    acc_ref[...] = jnp.zeros(acc_ref.shape, F32)

    def scores(hd, j):
        k0 = pl.multiple_of(j * TK, TK)
        s = _dot_nt(k_ref[hd, pl.ds(k0, TK), :], q_ref[hd])
        s_ref[hd] = s
        tmax_ref[hd] = jnp.max(s, axis=0, keepdims=True)

    def softmax(hd, s, tmax, cols=slice(None)):
        m = m_ref[hd, :, cols]
        m_new = jnp.maximum(m, tmax)
        alpha = jnp.exp2(m - m_new)
        p = jnp.exp2(s - m_new)
        l_ref[hd, :, cols] = alpha * l_ref[hd, :, cols] + jnp.sum(p, axis=0, keepdims=True)
        m_ref[hd, :, cols] = m_new
        return p.astype(BF16), alpha

    def weighted_values(hd, j, p, alpha, cols=slice(None)):
        pv = _dot(vt_ref[hd, j * VT_PER_TK], p[0:VT_BLK])
        for c in range(1, p.shape[0] // VT_BLK):
            pv += _dot(vt_ref[hd, j * VT_PER_TK + c], p[c * VT_BLK:(c + 1) * VT_BLK])
        acc_ref[hd, :, cols] = alpha * acc_ref[hd, :, cols] + pv

    for hd in heads:
        scores(hd, 0)

    @pl.loop(0, n_full)
    def _(j):
        for hd in heads:
            s, tmax = s_ref[hd], tmax_ref[hd]
            scores(hd, j + 1)
            p, alpha = softmax(hd, s, tmax)
            weighted_values(hd, j, p, alpha)

    lo, hi = slice(0, half), slice(half, TQ)
    for hd in heads:
        s = jnp.where(blk_mask, s_ref[hd, lo, lo], -1e30)
        p, alpha = softmax(hd, s, jnp.max(s, axis=0, keepdims=True), lo)
        weighted_values(hd, n_full, p, alpha, lo)
        s = jnp.concatenate([s_ref[hd, lo, hi], jnp.where(blk_mask, s_ref[hd, hi, hi], -1e30)], axis=0)
        p, alpha = softmax(hd, s, jnp.max(s, axis=0, keepdims=True), hi)
        weighted_values(hd, n_full, p, alpha, hi)
    for hd in heads:
        o_ref[:, hd * V_DIM:(hd + 1) * V_DIM] = (acc_ref[hd] / l_ref[hd]).T.astype(BF16)


def _attention(q, k, vt):
    assert TQ == TK and (TQ // 2) % VT_BLK == 0 and (TQ // 2) % CHUNK == 0
    return pl.pallas_call(
        _attn_kernel,
        grid=(BATCH, N_HEADS // HEAD_GROUP, SEQ // TQ),
        in_specs=[pl.BlockSpec((None, HEAD_GROUP, TQ, QK_PAD), lambda b, g, i: (b, g, i, 0)),
                  pl.BlockSpec((None, HEAD_GROUP, SEQ, QK_PAD), lambda b, g, i: (b, g, 0, 0)),
                  pl.BlockSpec((None, HEAD_GROUP, SEQ // VT_BLK, V_DIM, VT_BLK),
                               lambda b, g, i: (b, g, 0, 0, 0))],
        out_specs=pl.BlockSpec((None, TQ, HEAD_GROUP * V_DIM), lambda b, g, i: (b, i, g)),
        out_shape=jax.ShapeDtypeStruct((BATCH, SEQ, N_HEADS * V_DIM), BF16),
        scratch_shapes=[pltpu.VMEM((HEAD_GROUP, 1, TQ), F32), pltpu.VMEM((HEAD_GROUP, 1, TQ), F32),
                        pltpu.VMEM((HEAD_GROUP, V_DIM, TQ), F32),
                        pltpu.VMEM((HEAD_GROUP, TK, TQ), F32), pltpu.VMEM((HEAD_GROUP, 1, TQ), F32)],
        compiler_params=pltpu.CompilerParams(dimension_semantics=("arbitrary",) * 3,
                                             vmem_limit_bytes=VMEM_LIMIT),
        name="attention",
    )(q, k, vt)


def _mix_out_kernel(h_ref, o_ref, gc_ref, sg_ref, wmo_ref, wo_ref, out_ref):
    y_mla = _dot(o_ref[...], wmo_ref[...])
    merged = (gc_ref[...] + sg_ref[...] * y_mla).astype(BF16)
    out_ref[...] = h_ref[...] + _dot(merged, wo_ref[...])


def _mix_out(h, o, gc, sg, layer, w_mla_out, w_o):
    tile = pl.BlockSpec((TM_FFN, D_MODEL), lambda i: (i, 0))
    return pl.pallas_call(
        _mix_out_kernel,
        grid=(TOKENS // TM_FFN,),
        in_specs=[tile, tile, tile, tile,
                  _const_spec((D_MODEL, D_MODEL), layer), _const_spec((D_MODEL, D_MODEL), layer)],
        out_specs=tile,
        out_shape=jax.ShapeDtypeStruct((TOKENS, D_MODEL), F32),
        compiler_params=pltpu.CompilerParams(dimension_semantics=("arbitrary",),
                                             vmem_limit_bytes=VMEM_LIMIT),
        name="mix_out",
    )(h, o, gc, sg, w_mla_out, w_o)


def _rope_tables(positions):
    inv_freq = ROPE_THETA ** (-jnp.arange(0, ROPE_DIM, 2, dtype=F32) / ROPE_DIM)
    ang = positions.astype(F32)[..., None] * inv_freq
    cos, sin = jnp.cos(ang), jnp.sin(ang)
    return (jnp.concatenate([cos, cos, cos, cos], axis=-1),
            jnp.concatenate([-sin, sin, -sin, sin], axis=-1))


def kernel(x, p, positions, ffn1_norm, ffn1_w_gu, ffn1_w_down, mix_norm, w_in, conv_w, w_conv_out,
           q_norm, kv_norm, w_uq, w_ukv, w_mla_out, w_o, ffn2_norm, ffn2_w_gu, ffn2_w_down,
           ple_norm, w_ple_gate, w_ple_proj, final_norm):
    assert x.shape == (BATCH, SEQ, D_MODEL) and p.shape == (DEPTH, BATCH, SEQ, PLE_DIM)
    bf = lambda w: w.astype(BF16)
    row = lambda g: g.reshape(DEPTH, 1, -1)

    k_rot = w_in[:, :, OFF_KR:OFF_KR + ROPE_DIM]
    w_in_r = bf(jnp.concatenate([w_in[:, :, :OFF_KR], k_rot, k_rot, w_in[:, :, OFF_KR + ROPE_DIM:]], axis=-1))
    w_uq_h = w_uq.reshape(DEPTH, Q_LORA, N_HEADS, QK_DIM)
    w_uq_r = bf(jnp.concatenate([w_uq_h[..., :NOPE_DIM].reshape(DEPTH, Q_LORA, -1),
                                 w_uq_h[..., NOPE_DIM:].reshape(DEPTH, Q_LORA, -1)], axis=-1))
    w_ukv_h = w_ukv.reshape(DEPTH, KV_LORA, N_HEADS, NOPE_DIM + V_DIM)
    w_k = bf(w_ukv_h[..., :NOPE_DIM].reshape(DEPTH, KV_LORA, -1))
    w_vt = bf(jnp.swapaxes(w_ukv_h[..., NOPE_DIM:].reshape(DEPTH, KV_LORA, -1), 1, 2))
    ffn1_gu, ffn1_dn, ffn2_gu, ffn2_dn = bf(ffn1_w_gu), bf(ffn1_w_down), bf(ffn2_w_gu), bf(ffn2_w_down)
    w_co, w_mo, w_ob, w_pg, w_pp = bf(w_conv_out), bf(w_mla_out), bf(w_o), bf(w_ple_gate), bf(w_ple_proj)
    cos_t, sin_t = _rope_tables(positions)
    p2 = p.reshape(DEPTH, TOKENS, PLE_DIM)

    h = x.reshape(TOKENS, D_MODEL)
    for layer in range(DEPTH):
        h = _ffn(h, layer, row(ffn1_norm), ffn1_gu, ffn1_dn)
        q, k, vt, gc, sg = _mix_proj(h.reshape(BATCH, SEQ, D_MODEL), layer, cos_t, sin_t, row(mix_norm),
                                     w_in_r, conv_w, w_co, row(q_norm), row(kv_norm), w_uq_r, w_k, w_vt)
        o = _attention(q, k, vt)
        h = _mix_out(h, o.reshape(TOKENS, D_MODEL), gc.reshape(TOKENS, D_MODEL),
                     sg.reshape(TOKENS, D_MODEL), layer, w_mo, w_ob)
        h = _ffn(h, layer, row(ffn2_norm), ffn2_gu, ffn2_dn,
                 ple=(p2, row(ple_norm), w_pg, w_pp),
                 final_norm=final_norm.reshape(1, D_MODEL) if layer == DEPTH - 1 else None)
    return h.reshape(BATCH, SEQ, D_MODEL)
```

```python
import functools

import jax
import jax.numpy as jnp
from jax import lax
from jax.experimental import pallas as pl
from jax.experimental.pallas import tpu as pltpu

D_MODEL = 1024
BATCH = 8
SEQ = 4096
DEPTH = 4
CHUNK = 64
PLE_DIM = 256
D_FF = 2816
CONV_DIM = 512
CONV_K = 3
N_HEADS = 8
NOPE_DIM = 128
ROPE_DIM = 64
V_DIM = 128
Q_LORA = 384
KV_LORA = 256
ROPE_THETA = 10000.0
EPS = 1e-6
QK_DIM = NOPE_DIM + ROPE_DIM
ATTN_SCALE = QK_DIM ** -0.5
TOKENS = BATCH * SEQ

LANES = 128
QK_PAD = 2 * LANES
OFF_B, OFF_C, OFF_V = 0, CONV_DIM, 2 * CONV_DIM
OFF_Q = 3 * CONV_DIM
OFF_KV = OFF_Q + Q_LORA
OFF_KR = OFF_KV + KV_LORA
OFF_GC = OFF_KR + LANES
OFF_GM = OFF_GC + D_MODEL
IN_COLS_PAD = OFF_GM + D_MODEL

TM_FFN = 512
TM_MIX = 512
TQ = 512
TK = 512
HEAD_GROUP = 4
VT_BLK = 256
VT_PER_TK = TK // VT_BLK
V_ROWS = V_DIM + 16
LOG2E = 1.4426950408889634
CONV_HALO = 8
VMEM_LIMIT = 56 * 1024 * 1024

F32 = jnp.float32
BF16 = jnp.bfloat16


def _rmsnorm(x, g):
    return x * lax.rsqrt(jnp.mean(x * x, axis=-1, keepdims=True) + EPS) * g


def _dot(a, b):
    return jnp.dot(a, b, preferred_element_type=F32)


def _dot_nt(a, b):
    return lax.dot_general(a, b, (((1,), (1,)), ((), ())), preferred_element_type=F32)


def _const_spec(shape, layer):
    zeros = (0,) * len(shape)
    return pl.BlockSpec((None,) + tuple(shape), lambda *_: (layer,) + zeros,
                        pipeline_mode=pl.Buffered(1))


def _ffn_kernel(*refs, with_ple, with_final):
    x_ref, g_ref, wgu_ref, wd_ref = refs[:4]
    o_ref = refs[-1]
    x = x_ref[...]
    xn = _rmsnorm(x, g_ref[...]).astype(BF16)
    gu = _dot(xn, wgu_ref[...])
    g, u = gu[:, :D_FF], gu[:, D_FF:]
    a = (g * jax.nn.sigmoid(g) * u).astype(BF16)
    h = x + 0.5 * _dot(a, wd_ref[...])
    if with_ple:
        p_ref, pg_ref, wpg_ref, wpp_ref = refs[4:8]
        hn = _rmsnorm(h, pg_ref[...]).astype(BF16)
        gate = jax.nn.sigmoid(_dot(hn, wpg_ref[...]))
        h = h + gate * _dot(p_ref[...].astype(BF16), wpp_ref[...])
    if with_final:
        h = _rmsnorm(h, refs[8][...])
    o_ref[...] = h


def _ffn(h, layer, norm, w_gu, w_down, ple=None, final_norm=None):
    tile = pl.BlockSpec((TM_FFN, D_MODEL), lambda i: (i, 0))
    in_specs = [tile, _const_spec((1, D_MODEL), layer),
                _const_spec((D_MODEL, 2 * D_FF), layer), _const_spec((D_FF, D_MODEL), layer)]
    args = [h, norm, w_gu, w_down]
    if ple is not None:
        p, ple_norm, w_pg, w_pp = ple
        in_specs += [pl.BlockSpec((None, TM_FFN, PLE_DIM), lambda i: (layer, i, 0)),
                     _const_spec((1, D_MODEL), layer),
                     _const_spec((D_MODEL, D_MODEL), layer), _const_spec((PLE_DIM, D_MODEL), layer)]
        args += [p, ple_norm, w_pg, w_pp]
    if final_norm is not None:
        in_specs.append(pl.BlockSpec((1, D_MODEL), lambda i: (0, 0)))
        args.append(final_norm)
    return pl.pallas_call(
        functools.partial(_ffn_kernel, with_ple=ple is not None, with_final=final_norm is not None),
        grid=(TOKENS // TM_FFN,),
        in_specs=in_specs,
        out_specs=tile,
        out_shape=jax.ShapeDtypeStruct((TOKENS, D_MODEL), F32),
        compiler_params=pltpu.CompilerParams(dimension_semantics=("arbitrary",),
                                             vmem_limit_bytes=VMEM_LIMIT),
        name="ffn_ple" if ple is not None else "ffn",
    )(*args)


def _rope(x, cos_t, sin_t):
    width = x.shape[-1]
    first_half = (lax.broadcasted_iota(jnp.int32, x.shape, 1) % ROPE_DIM) < ROPE_DIM // 2
    swapped = jnp.where(first_half, pltpu.roll(x, width - ROPE_DIM // 2, 1),
                        pltpu.roll(x, ROPE_DIM // 2, 1))
    return x * cos_t + swapped * sin_t


def _mix_proj_kernel(h_ref, cos_ref, sin_ref, g_ref, win_ref, cw_ref, wco_ref, qg_ref, kvg_ref,
                     wuq_ref, wk_ref, wvt_ref,
                     q_ref, k_ref, vt_ref, gc_ref, sg_ref, z_ref):
    groups = [slice(r * VT_BLK, (r + 1) * VT_BLK) for r in range(TM_MIX // VT_BLK)]

    @pl.when(pl.program_id(1) == 0)
    def _():
        z_ref[0:CONV_HALO, :] = jnp.zeros((CONV_HALO, CONV_DIM), F32)

    us = [_rmsnorm(h_ref[g, :], g_ref[...]).astype(BF16) for g in groups]
    projs = [_dot(u, win_ref[...]) for u in us]
    cw = cw_ref[...]
    rope0 = N_HEADS * NOPE_DIM
    ones_rows = (lax.broadcasted_iota(jnp.int32, (V_ROWS - V_DIM, VT_BLK), 0) == 0).astype(BF16)
    for r, (g, proj) in enumerate(zip(groups, projs)):
        z0 = CONV_HALO + r * VT_BLK
        z = proj[:, OFF_C:OFF_C + CONV_DIM] * proj[:, OFF_V:OFF_V + CONV_DIM]
        z_ref[z0:z0 + VT_BLK, :] = z
        y = (cw[0:1] * z_ref[z0 - 2:z0 - 2 + VT_BLK, :] + cw[1:2] * z_ref[z0 - 1:z0 - 1 + VT_BLK, :]
             + cw[2:3] * z)
        y_conv = _dot((proj[:, OFF_B:OFF_B + CONV_DIM] * y).astype(BF16), wco_ref[...])
        gc_ref[g, :] = (jax.nn.sigmoid(proj[:, OFF_GC:OFF_GC + D_MODEL]) * y_conv).astype(BF16)
        sg_ref[g, :] = jax.nn.sigmoid(proj[:, OFF_GM:OFF_GM + D_MODEL]).astype(BF16)

        cos_t, sin_t = cos_ref[g, :], sin_ref[g, :]
        qn = _rmsnorm(proj[:, OFF_Q:OFF_Q + Q_LORA], qg_ref[...]).astype(BF16)
        q = _dot(qn, wuq_ref[...]) * (ATTN_SCALE * LOG2E)
        kvn = _rmsnorm(proj[:, OFF_KV:OFF_KV + KV_LORA], kvg_ref[...]).astype(BF16)
        k_nope = _dot(kvn, wk_ref[...])
        v_t = _dot_nt(wvt_ref[...], kvn)
        k_rope = _rope(proj[:, OFF_KR:OFF_KR + LANES], cos_t, sin_t)
        low_half = lax.broadcasted_iota(jnp.int32, k_rope.shape, 1) < ROPE_DIM
        k_rope_even = jnp.where(low_half, k_rope, 0.0).astype(BF16)
        k_rope_odd = jnp.where(low_half, 0.0, k_rope).astype(BF16)
        for hd in range(N_HEADS):
            q_ref[hd, g, 0:LANES] = q[:, hd * NOPE_DIM:(hd + 1) * NOPE_DIM].astype(BF16)
            k_ref[hd, g, 0:LANES] = k_nope[:, hd * NOPE_DIM:(hd + 1) * NOPE_DIM].astype(BF16)
            k_ref[hd, g, LANES:QK_PAD] = k_rope_odd if hd % 2 else k_rope_even
            vt_ref[hd, r, 0:V_DIM, :] = v_t[hd * V_DIM:(hd + 1) * V_DIM, :].astype(BF16)
            vt_ref[hd, r, V_DIM:V_ROWS, :] = ones_rows
        for pair in range(N_HEADS // 2):
            qr = _rope(q[:, rope0 + pair * LANES:rope0 + (pair + 1) * LANES], cos_t, sin_t).astype(BF16)
            q_ref[2 * pair, g, LANES:QK_PAD] = qr
            q_ref[2 * pair + 1, g, LANES:QK_PAD] = qr
    z_ref[0:CONV_HALO, :] = z_ref[TM_MIX:TM_MIX + CONV_HALO, :]


def _mix_proj(h, layer, cos_t, sin_t, norm, w_in, conv_w, w_conv_out, q_norm, kv_norm, w_uq, w_k, w_vt):
    n_t = SEQ // TM_MIX
    tok = lambda w: pl.BlockSpec((None, TM_MIX, w), lambda b, i: (b, i, 0))
    head = pl.BlockSpec((None, N_HEADS, TM_MIX, QK_PAD), lambda b, i: (b, 0, i, 0))
    return pl.pallas_call(
        _mix_proj_kernel,
        grid=(BATCH, n_t),
        in_specs=[tok(D_MODEL), tok(LANES), tok(LANES),
                  _const_spec((1, D_MODEL), layer), _const_spec((D_MODEL, IN_COLS_PAD), layer),
                  _const_spec((CONV_K, CONV_DIM), layer), _const_spec((CONV_DIM, D_MODEL), layer),
                  _const_spec((1, Q_LORA), layer), _const_spec((1, KV_LORA), layer),
                  _const_spec((Q_LORA, N_HEADS * QK_DIM), layer),
                  _const_spec((KV_LORA, N_HEADS * NOPE_DIM), layer),
                  _const_spec((N_HEADS * V_DIM, KV_LORA), layer)],
        out_specs=[head, head,
                   pl.BlockSpec((None, N_HEADS, TM_MIX // VT_BLK, V_ROWS, VT_BLK),
                                lambda b, i: (b, 0, i, 0, 0)),
                   tok(D_MODEL), tok(D_MODEL)],
        out_shape=[jax.ShapeDtypeStruct((BATCH, N_HEADS, SEQ, QK_PAD), BF16),
                   jax.ShapeDtypeStruct((BATCH, N_HEADS, SEQ, QK_PAD), BF16),
                   jax.ShapeDtypeStruct((BATCH, N_HEADS, SEQ // VT_BLK, V_ROWS, VT_BLK), BF16),
                   jax.ShapeDtypeStruct((BATCH, SEQ, D_MODEL), BF16),
                   jax.ShapeDtypeStruct((BATCH, SEQ, D_MODEL), BF16)],
        scratch_shapes=[pltpu.VMEM((TM_MIX + CONV_HALO, CONV_DIM), F32)],
        compiler_params=pltpu.CompilerParams(dimension_semantics=("arbitrary", "arbitrary"),
                                             vmem_limit_bytes=VMEM_LIMIT),
        name="mix_proj",
    )(h, cos_t, sin_t, norm, w_in, conv_w, w_conv_out, q_norm, kv_norm, w_uq, w_k, w_vt)


def _attn_kernel(q_ref, k_ref, vt_ref, o_ref, m_ref, acc_ref, s_ref, tmax_ref):
    n_full = pl.program_id(2)
    half = TQ // 2
    blk_mask = (lax.broadcasted_iota(jnp.int32, (half, half), 0) // CHUNK
                <= lax.broadcasted_iota(jnp.int32, (half, half), 1) // CHUNK)

    heads = range(HEAD_GROUP)
    m_ref[...] = jnp.full(m_ref.shape, -1e30, F32)
    acc_ref[...] = jnp.zeros(acc_ref.shape, F32)

    def scores(hd, j):
        k0 = pl.multiple_of(j * TK, TK)
        s = _dot_nt(k_ref[hd, pl.ds(k0, TK), :], q_ref[hd])
        s_ref[hd] = s
        tmax_ref[hd] = jnp.max(s, axis=0, keepdims=True)

    def softmax(hd, s, tmax, cols=slice(None)):
        m = m_ref[hd, :, cols]
        m_new = jnp.maximum(m, tmax)
        alpha = jnp.exp2(m - m_new)
        m_ref[hd, :, cols] = m_new
        return jnp.exp2((s - m_new).astype(BF16)), alpha

    def weighted_values(hd, j, p, alpha, cols=slice(None)):
        pv = _dot(vt_ref[hd, j * VT_PER_TK], p[0:VT_BLK])
        for c in range(1, p.shape[0] // VT_BLK):
            pv += _dot(vt_ref[hd, j * VT_PER_TK + c], p[c * VT_BLK:(c + 1) * VT_BLK])
        acc_ref[hd, :, cols] = alpha * acc_ref[hd, :, cols] + pv

    for hd in heads:
        scores(hd, 0)

    @pl.loop(0, n_full)
    def _(j):
        for hd in heads:
            s, tmax = s_ref[hd], tmax_ref[hd]
            scores(hd, j + 1)
            p, alpha = softmax(hd, s, tmax)
            weighted_values(hd, j, p, alpha)

    lo, hi = slice(0, half), slice(half, TQ)
    for hd in heads:
        s = jnp.where(blk_mask, s_ref[hd, lo, lo], -1e30)
        p, alpha = softmax(hd, s, jnp.max(s, axis=0, keepdims=True), lo)
        weighted_values(hd, n_full, p, alpha, lo)
        s = jnp.concatenate([s_ref[hd, lo, hi], jnp.where(blk_mask, s_ref[hd, hi, hi], -1e30)], axis=0)
        p, alpha = softmax(hd, s, jnp.max(s, axis=0, keepdims=True), hi)
        weighted_values(hd, n_full, p, alpha, hi)
    for hd in heads:
        out = acc_ref[hd, 0:V_DIM, :] / acc_ref[hd, V_DIM:V_DIM + 1, :]
        o_ref[:, hd * V_DIM:(hd + 1) * V_DIM] = out.T.astype(BF16)


def _attention(q, k, vt):
    assert TQ == TK and (TQ // 2) % VT_BLK == 0 and (TQ // 2) % CHUNK == 0
    return pl.pallas_call(
        _attn_kernel,
        grid=(BATCH, N_HEADS // HEAD_GROUP, SEQ // TQ),
        in_specs=[pl.BlockSpec((None, HEAD_GROUP, TQ, QK_PAD), lambda b, g, i: (b, g, i, 0)),
                  pl.BlockSpec((None, HEAD_GROUP, SEQ, QK_PAD), lambda b, g, i: (b, g, 0, 0)),
                  pl.BlockSpec((None, HEAD_GROUP, SEQ // VT_BLK, V_ROWS, VT_BLK),
                               lambda b, g, i: (b, g, 0, 0, 0))],
        out_specs=pl.BlockSpec((None, TQ, HEAD_GROUP * V_DIM), lambda b, g, i: (b, i, g)),
        out_shape=jax.ShapeDtypeStruct((BATCH, SEQ, N_HEADS * V_DIM), BF16),
        scratch_shapes=[pltpu.VMEM((HEAD_GROUP, 1, TQ), F32),
                        pltpu.VMEM((HEAD_GROUP, V_ROWS, TQ), F32),
                        pltpu.VMEM((HEAD_GROUP, TK, TQ), F32), pltpu.VMEM((HEAD_GROUP, 1, TQ), F32)],
        compiler_params=pltpu.CompilerParams(dimension_semantics=("arbitrary",) * 3,
                                             vmem_limit_bytes=VMEM_LIMIT),
        name="attention",
    )(q, k, vt)


def _mix_out_kernel(h_ref, o_ref, gc_ref, sg_ref, wmo_ref, wo_ref, out_ref):
    y_mla = _dot(o_ref[...], wmo_ref[...])
    merged = (gc_ref[...] + sg_ref[...] * y_mla).astype(BF16)
    out_ref[...] = h_ref[...] + _dot(merged, wo_ref[...])


def _mix_out(h, o, gc, sg, layer, w_mla_out, w_o):
    tile = pl.BlockSpec((TM_FFN, D_MODEL), lambda i: (i, 0))
    return pl.pallas_call(
        _mix_out_kernel,
        grid=(TOKENS // TM_FFN,),
        in_specs=[tile, tile, tile, tile,
                  _const_spec((D_MODEL, D_MODEL), layer), _const_spec((D_MODEL, D_MODEL), layer)],
        out_specs=tile,
        out_shape=jax.ShapeDtypeStruct((TOKENS, D_MODEL), F32),
        compiler_params=pltpu.CompilerParams(dimension_semantics=("arbitrary",),
                                             vmem_limit_bytes=VMEM_LIMIT),
        name="mix_out",
    )(h, o, gc, sg, w_mla_out, w_o)


def _rope_tables(positions):
    inv_freq = ROPE_THETA ** (-jnp.arange(0, ROPE_DIM, 2, dtype=F32) / ROPE_DIM)
    ang = positions.astype(F32)[..., None] * inv_freq
    cos, sin = jnp.cos(ang), jnp.sin(ang)
    return (jnp.concatenate([cos, cos, cos, cos], axis=-1),
            jnp.concatenate([-sin, sin, -sin, sin], axis=-1))


def kernel(x, p, positions, ffn1_norm, ffn1_w_gu, ffn1_w_down, mix_norm, w_in, conv_w, w_conv_out,
           q_norm, kv_norm, w_uq, w_ukv, w_mla_out, w_o, ffn2_norm, ffn2_w_gu, ffn2_w_down,
           ple_norm, w_ple_gate, w_ple_proj, final_norm):
    assert x.shape == (BATCH, SEQ, D_MODEL) and p.shape == (DEPTH, BATCH, SEQ, PLE_DIM)
    bf = lambda w: w.astype(BF16)
    row = lambda g: g.reshape(DEPTH, 1, -1)

    k_rot = w_in[:, :, OFF_KR:OFF_KR + ROPE_DIM]
    w_in_r = bf(jnp.concatenate([w_in[:, :, :OFF_KR], k_rot, k_rot, w_in[:, :, OFF_KR + ROPE_DIM:]], axis=-1))
    w_uq_h = w_uq.reshape(DEPTH, Q_LORA, N_HEADS, QK_DIM)
    w_uq_r = bf(jnp.concatenate([w_uq_h[..., :NOPE_DIM].reshape(DEPTH, Q_LORA, -1),
                                 w_uq_h[..., NOPE_DIM:].reshape(DEPTH, Q_LORA, -1)], axis=-1))
    w_ukv_h = w_ukv.reshape(DEPTH, KV_LORA, N_HEADS, NOPE_DIM + V_DIM)
    w_k = bf(w_ukv_h[..., :NOPE_DIM].reshape(DEPTH, KV_LORA, -1))
    w_vt = bf(jnp.swapaxes(w_ukv_h[..., NOPE_DIM:].reshape(DEPTH, KV_LORA, -1), 1, 2))
    ffn1_gu, ffn1_dn, ffn2_gu, ffn2_dn = bf(ffn1_w_gu), bf(ffn1_w_down), bf(ffn2_w_gu), bf(ffn2_w_down)
    w_co, w_mo, w_ob, w_pg, w_pp = bf(w_conv_out), bf(w_mla_out), bf(w_o), bf(w_ple_gate), bf(w_ple_proj)
    cos_t, sin_t = _rope_tables(positions)
    p2 = p.reshape(DEPTH, TOKENS, PLE_DIM)

    h = x.reshape(TOKENS, D_MODEL)
    for layer in range(DEPTH):
        h = _ffn(h, layer, row(ffn1_norm), ffn1_gu, ffn1_dn)
        q, k, vt, gc, sg = _mix_proj(h.reshape(BATCH, SEQ, D_MODEL), layer, cos_t, sin_t, row(mix_norm),
                                     w_in_r, conv_w, w_co, row(q_norm), row(kv_norm), w_uq_r, w_k, w_vt)
        o = _attention(q, k, vt)
        h = _mix_out(h, o.reshape(TOKENS, D_MODEL), gc.reshape(TOKENS, D_MODEL),
                     sg.reshape(TOKENS, D_MODEL), layer, w_mo, w_ob)
        h = _ffn(h, layer, row(ffn2_norm), ffn2_gu, ffn2_dn,
                 ple=(p2, row(ple_norm), w_pg, w_pp),
                 final_norm=final_norm.reshape(1, D_MODEL) if layer == DEPTH - 1 else None)
    return h.reshape(BATCH, SEQ, D_MODEL)
```

```python
import functools

import jax
import jax.numpy as jnp
from jax import lax
from jax.experimental import pallas as pl
from jax.experimental.pallas import tpu as pltpu

D_MODEL = 1024
BATCH = 8
SEQ = 4096
DEPTH = 4
CHUNK = 64
PLE_DIM = 256
D_FF = 2816
CONV_DIM = 512
CONV_K = 3
N_HEADS = 8
NOPE_DIM = 128
ROPE_DIM = 64
V_DIM = 128
Q_LORA = 384
KV_LORA = 256
ROPE_THETA = 10000.0
EPS = 1e-6
QK_DIM = NOPE_DIM + ROPE_DIM
ATTN_SCALE = QK_DIM ** -0.5
TOKENS = BATCH * SEQ

LANES = 128
QK_PAD = 2 * LANES
OFF_B, OFF_C, OFF_V = 0, CONV_DIM, 2 * CONV_DIM
OFF_Q = 3 * CONV_DIM
OFF_KV = OFF_Q + Q_LORA
OFF_KR = OFF_KV + KV_LORA
OFF_GC = OFF_KR + LANES
OFF_GM = OFF_GC + D_MODEL
IN_COLS_PAD = OFF_GM + D_MODEL

TM_FFN = 512
TM_MIX = 512
TQ = 512
TK = 512
HEAD_GROUP = 4
VT_BLK = 256
VT_PER_TK = TK // VT_BLK
V_ROWS = V_DIM + 16
LOG2E = 1.4426950408889634
CONV_HALO = 8
VMEM_LIMIT = 56 * 1024 * 1024

F32 = jnp.float32
BF16 = jnp.bfloat16


def _rmsnorm(x, g):
    return x * lax.rsqrt(jnp.mean(x * x, axis=-1, keepdims=True) + EPS) * g


def _dot(a, b):
    return jnp.dot(a, b, preferred_element_type=F32)


def _dot_nt(a, b):
    return lax.dot_general(a, b, (((1,), (1,)), ((), ())), preferred_element_type=F32)


def _const_spec(shape, layer):
    zeros = (0,) * len(shape)
    return pl.BlockSpec((None,) + tuple(shape), lambda *_: (layer,) + zeros,
                        pipeline_mode=pl.Buffered(1))


def _ffn_kernel(*refs, with_ple, with_final):
    x_ref, g_ref, wgu_ref, wd_ref = refs[:4]
    o_ref = refs[-1]
    x = x_ref[...]
    xn = _rmsnorm(x, g_ref[...]).astype(BF16)
    gu = _dot(xn, wgu_ref[...])
    g, u = gu[:, :D_FF], gu[:, D_FF:]
    a = (g * jax.nn.sigmoid(g) * u).astype(BF16)
    h = x + 0.5 * _dot(a, wd_ref[...])
    if with_ple:
        p_ref, pg_ref, wpg_ref, wpp_ref = refs[4:8]
        hn = _rmsnorm(h, pg_ref[...]).astype(BF16)
        gate = jax.nn.sigmoid(_dot(hn, wpg_ref[...]))
        h = h + gate * _dot(p_ref[...].astype(BF16), wpp_ref[...])
    if with_final:
        h = _rmsnorm(h, refs[8][...])
    o_ref[...] = h


def _ffn(h, layer, norm, w_gu, w_down, ple=None, final_norm=None):
    tile = pl.BlockSpec((TM_FFN, D_MODEL), lambda i: (i, 0))
    in_specs = [tile, _const_spec((1, D_MODEL), layer),
                _const_spec((D_MODEL, 2 * D_FF), layer), _const_spec((D_FF, D_MODEL), layer)]
    args = [h, norm, w_gu, w_down]
    if ple is not None:
        p, ple_norm, w_pg, w_pp = ple
        in_specs += [pl.BlockSpec((None, TM_FFN, PLE_DIM), lambda i: (layer, i, 0)),
                     _const_spec((1, D_MODEL), layer),
                     _const_spec((D_MODEL, D_MODEL), layer), _const_spec((PLE_DIM, D_MODEL), layer)]
        args += [p, ple_norm, w_pg, w_pp]
    if final_norm is not None:
        in_specs.append(pl.BlockSpec((1, D_MODEL), lambda i: (0, 0)))
        args.append(final_norm)
    return pl.pallas_call(
        functools.partial(_ffn_kernel, with_ple=ple is not None, with_final=final_norm is not None),
        grid=(TOKENS // TM_FFN,),
        in_specs=in_specs,
        out_specs=tile,
        out_shape=jax.ShapeDtypeStruct((TOKENS, D_MODEL), F32),
        compiler_params=pltpu.CompilerParams(dimension_semantics=("arbitrary",),
                                             vmem_limit_bytes=VMEM_LIMIT),
        name="ffn_ple" if ple is not None else "ffn",
    )(*args)


def _rope(x, cos_t, sin_t):
    width = x.shape[-1]
    first_half = (lax.broadcasted_iota(jnp.int32, x.shape, 1) % ROPE_DIM) < ROPE_DIM // 2
    swapped = jnp.where(first_half, pltpu.roll(x, width - ROPE_DIM // 2, 1),
                        pltpu.roll(x, ROPE_DIM // 2, 1))
    return x * cos_t + swapped * sin_t


def _mix_proj_kernel(h_ref, cos_ref, sin_ref, cost_ref, sint_ref, g_ref, win_ref, cw_ref, wco_ref,
                     qg_ref, kvg_ref, wuqt_ref, wk_ref, wvt_ref,
                     q_ref, k_ref, vt_ref, gc_ref, sg_ref, z_ref):
    groups = [slice(r * VT_BLK, (r + 1) * VT_BLK) for r in range(TM_MIX // VT_BLK)]

    @pl.when(pl.program_id(1) == 0)
    def _():
        z_ref[0:CONV_HALO, :] = jnp.zeros((CONV_HALO, CONV_DIM), F32)

    us = [_rmsnorm(h_ref[g, :], g_ref[...]).astype(BF16) for g in groups]
    projs = [_dot(u, win_ref[...]) for u in us]
    cw = cw_ref[...]
    rope0, half_rope = N_HEADS * NOPE_DIM, ROPE_DIM // 2
    ones_rows = (lax.broadcasted_iota(jnp.int32, (V_ROWS - V_DIM, VT_BLK), 0) == 0).astype(BF16)
    for r, (g, proj) in enumerate(zip(groups, projs)):
        z0 = CONV_HALO + r * VT_BLK
        z = proj[:, OFF_C:OFF_C + CONV_DIM] * proj[:, OFF_V:OFF_V + CONV_DIM]
        z_ref[z0:z0 + VT_BLK, :] = z
        y = (cw[0:1] * z_ref[z0 - 2:z0 - 2 + VT_BLK, :] + cw[1:2] * z_ref[z0 - 1:z0 - 1 + VT_BLK, :]
             + cw[2:3] * z)
        y_conv = _dot((proj[:, OFF_B:OFF_B + CONV_DIM] * y).astype(BF16), wco_ref[...])
        gc_ref[g, :] = (jax.nn.sigmoid(proj[:, OFF_GC:OFF_GC + D_MODEL]) * y_conv).astype(BF16)
        sg_ref[g, :] = jax.nn.sigmoid(proj[:, OFF_GM:OFF_GM + D_MODEL]).astype(BF16)

        qn = _rmsnorm(proj[:, OFF_Q:OFF_Q + Q_LORA], qg_ref[...]).astype(BF16)
        q_t = _dot_nt(wuqt_ref[...], qn) * (ATTN_SCALE * LOG2E)
        kvn = _rmsnorm(proj[:, OFF_KV:OFF_KV + KV_LORA], kvg_ref[...]).astype(BF16)
        k_nope = _dot(kvn, wk_ref[...])
        v_t = _dot_nt(wvt_ref[...], kvn)
        k_rope = _rope(proj[:, OFF_KR:OFF_KR + LANES], cos_ref[g, :], sin_ref[g, :])
        low_half = lax.broadcasted_iota(jnp.int32, k_rope.shape, 1) < ROPE_DIM
        k_rope = jnp.where(low_half, k_rope, 0.0).astype(BF16)
        cos_c, sin_c = cost_ref[:, g], sint_ref[:, g]
        for hd in range(N_HEADS):
            q_ref[hd, 0:NOPE_DIM, g] = q_t[hd * NOPE_DIM:(hd + 1) * NOPE_DIM, :].astype(BF16)
            x1 = q_t[rope0 + hd * ROPE_DIM:rope0 + hd * ROPE_DIM + half_rope, :]
            x2 = q_t[rope0 + hd * ROPE_DIM + half_rope:rope0 + (hd + 1) * ROPE_DIM, :]
            q_ref[hd, NOPE_DIM:NOPE_DIM + half_rope, g] = (x1 * cos_c - x2 * sin_c).astype(BF16)
            q_ref[hd, NOPE_DIM + half_rope:QK_DIM, g] = (x2 * cos_c + x1 * sin_c).astype(BF16)
            q_ref[hd, QK_DIM:QK_PAD, g] = jnp.zeros((QK_PAD - QK_DIM, VT_BLK), BF16)
            k_ref[hd, g, 0:LANES] = k_nope[:, hd * NOPE_DIM:(hd + 1) * NOPE_DIM].astype(BF16)
            k_ref[hd, g, LANES:QK_PAD] = k_rope
            vt_ref[hd, r, 0:V_DIM, :] = v_t[hd * V_DIM:(hd + 1) * V_DIM, :].astype(BF16)
            vt_ref[hd, r, V_DIM:V_ROWS, :] = ones_rows
    z_ref[0:CONV_HALO, :] = z_ref[TM_MIX:TM_MIX + CONV_HALO, :]


def _mix_proj(h, layer, rope, norm, w_in, conv_w, w_conv_out, q_norm, kv_norm, w_uqt, w_k, w_vt):
    n_t = SEQ // TM_MIX
    tok = lambda w: pl.BlockSpec((None, TM_MIX, w), lambda b, i: (b, i, 0))
    tok_t = lambda w: pl.BlockSpec((None, w, TM_MIX), lambda b, i: (b, 0, i))
    return pl.pallas_call(
        _mix_proj_kernel,
        grid=(BATCH, n_t),
        in_specs=[tok(D_MODEL), tok(LANES), tok(LANES), tok_t(ROPE_DIM // 2), tok_t(ROPE_DIM // 2),
                  _const_spec((1, D_MODEL), layer), _const_spec((D_MODEL, IN_COLS_PAD), layer),
                  _const_spec((CONV_K, CONV_DIM), layer), _const_spec((CONV_DIM, D_MODEL), layer),
                  _const_spec((1, Q_LORA), layer), _const_spec((1, KV_LORA), layer),
                  _const_spec((N_HEADS * QK_DIM, Q_LORA), layer),
                  _const_spec((KV_LORA, N_HEADS * NOPE_DIM), layer),
                  _const_spec((N_HEADS * V_DIM, KV_LORA), layer)],
        out_specs=[pl.BlockSpec((None, N_HEADS, QK_PAD, TM_MIX), lambda b, i: (b, 0, 0, i)),
                   pl.BlockSpec((None, N_HEADS, TM_MIX, QK_PAD), lambda b, i: (b, 0, i, 0)),
                   pl.BlockSpec((None, N_HEADS, TM_MIX // VT_BLK, V_ROWS, VT_BLK),
                                lambda b, i: (b, 0, i, 0, 0)),
                   tok(D_MODEL), tok(D_MODEL)],
        out_shape=[jax.ShapeDtypeStruct((BATCH, N_HEADS, QK_PAD, SEQ), BF16),
                   jax.ShapeDtypeStruct((BATCH, N_HEADS, SEQ, QK_PAD), BF16),
                   jax.ShapeDtypeStruct((BATCH, N_HEADS, SEQ // VT_BLK, V_ROWS, VT_BLK), BF16),
                   jax.ShapeDtypeStruct((BATCH, SEQ, D_MODEL), BF16),
                   jax.ShapeDtypeStruct((BATCH, SEQ, D_MODEL), BF16)],
        scratch_shapes=[pltpu.VMEM((TM_MIX + CONV_HALO, CONV_DIM), F32)],
        compiler_params=pltpu.CompilerParams(dimension_semantics=("arbitrary", "arbitrary"),
                                             vmem_limit_bytes=VMEM_LIMIT),
        name="mix_proj",
    )(h, *rope, norm, w_in, conv_w, w_conv_out, q_norm, kv_norm, w_uqt, w_k, w_vt)


def _attn_kernel(q_ref, k_ref, vt_ref, o_ref, m_ref, acc_ref, s_ref, tmax_ref):
    n_full = pl.program_id(2)
    half = TQ // 2
    blk_mask = (lax.broadcasted_iota(jnp.int32, (half, half), 0) // CHUNK
                <= lax.broadcasted_iota(jnp.int32, (half, half), 1) // CHUNK)

    heads = range(HEAD_GROUP)
    m_ref[...] = jnp.full(m_ref.shape, -1e30, F32)
    acc_ref[...] = jnp.zeros(acc_ref.shape, F32)

    def scores(hd, j):
        k0 = pl.multiple_of(j * TK, TK)
        s = _dot(k_ref[hd, pl.ds(k0, TK), :], q_ref[hd])
        s_ref[hd] = s
        tmax_ref[hd] = jnp.max(s, axis=0, keepdims=True)

    def softmax(hd, s, tmax, cols=slice(None)):
        m = m_ref[hd, :, cols]
        m_new = jnp.maximum(m, tmax)
        alpha = jnp.exp2(m - m_new)
        m_ref[hd, :, cols] = m_new
        return jnp.exp2((s - m_new).astype(BF16)), alpha

    def weighted_values(hd, j, p, alpha, cols=slice(None)):
        pv = _dot(vt_ref[hd, j * VT_PER_TK], p[0:VT_BLK])
        for c in range(1, p.shape[0] // VT_BLK):
            pv += _dot(vt_ref[hd, j * VT_PER_TK + c], p[c * VT_BLK:(c + 1) * VT_BLK])
        acc_ref[hd, :, cols] = alpha * acc_ref[hd, :, cols] + pv

    for hd in heads:
        scores(hd, 0)

    @pl.loop(0, n_full)
    def _(j):
        for hd in heads:
            s, tmax = s_ref[hd], tmax_ref[hd]
            scores(hd, j + 1)
            p, alpha = softmax(hd, s, tmax)
            weighted_values(hd, j, p, alpha)

    lo, hi = slice(0, half), slice(half, TQ)
    for hd in heads:
        s = jnp.where(blk_mask, s_ref[hd, lo, lo], -1e30)
        p, alpha = softmax(hd, s, jnp.max(s, axis=0, keepdims=True), lo)
        weighted_values(hd, n_full, p, alpha, lo)
        s = jnp.concatenate([s_ref[hd, lo, hi], jnp.where(blk_mask, s_ref[hd, hi, hi], -1e30)], axis=0)
        p, alpha = softmax(hd, s, jnp.max(s, axis=0, keepdims=True), hi)
        weighted_values(hd, n_full, p, alpha, hi)
    for hd in heads:
        out = acc_ref[hd, 0:V_DIM, :] / acc_ref[hd, V_DIM:V_DIM + 1, :]
        o_ref[:, hd * V_DIM:(hd + 1) * V_DIM] = out.T.astype(BF16)


def _attention(q, k, vt):
    assert TQ == TK and (TQ // 2) % VT_BLK == 0 and (TQ // 2) % CHUNK == 0
    return pl.pallas_call(
        _attn_kernel,
        grid=(BATCH, N_HEADS // HEAD_GROUP, SEQ // TQ),
        in_specs=[pl.BlockSpec((None, HEAD_GROUP, QK_PAD, TQ), lambda b, g, i: (b, g, 0, i)),
                  pl.BlockSpec((None, HEAD_GROUP, SEQ, QK_PAD), lambda b, g, i: (b, g, 0, 0)),
                  pl.BlockSpec((None, HEAD_GROUP, SEQ // VT_BLK, V_ROWS, VT_BLK),
                               lambda b, g, i: (b, g, 0, 0, 0))],
        out_specs=pl.BlockSpec((None, TQ, HEAD_GROUP * V_DIM), lambda b, g, i: (b, i, g)),
        out_shape=jax.ShapeDtypeStruct((BATCH, SEQ, N_HEADS * V_DIM), BF16),
        scratch_shapes=[pltpu.VMEM((HEAD_GROUP, 1, TQ), F32),
                        pltpu.VMEM((HEAD_GROUP, V_ROWS, TQ), F32),
                        pltpu.VMEM((HEAD_GROUP, TK, TQ), F32), pltpu.VMEM((HEAD_GROUP, 1, TQ), F32)],
        compiler_params=pltpu.CompilerParams(dimension_semantics=("arbitrary",) * 3,
                                             vmem_limit_bytes=VMEM_LIMIT),
        name="attention",
    )(q, k, vt)


def _mix_out_kernel(h_ref, o_ref, gc_ref, sg_ref, wmo_ref, wo_ref, out_ref):
    y_mla = _dot(o_ref[...], wmo_ref[...])
    merged = (gc_ref[...] + sg_ref[...] * y_mla).astype(BF16)
    out_ref[...] = h_ref[...] + _dot(merged, wo_ref[...])


def _mix_out(h, o, gc, sg, layer, w_mla_out, w_o):
    tile = pl.BlockSpec((TM_FFN, D_MODEL), lambda i: (i, 0))
    return pl.pallas_call(
        _mix_out_kernel,
        grid=(TOKENS // TM_FFN,),
        in_specs=[tile, tile, tile, tile,
                  _const_spec((D_MODEL, D_MODEL), layer), _const_spec((D_MODEL, D_MODEL), layer)],
        out_specs=tile,
        out_shape=jax.ShapeDtypeStruct((TOKENS, D_MODEL), F32),
        compiler_params=pltpu.CompilerParams(dimension_semantics=("arbitrary",),
                                             vmem_limit_bytes=VMEM_LIMIT),
        name="mix_out",
    )(h, o, gc, sg, w_mla_out, w_o)


def _rope_tables(positions):
    inv_freq = ROPE_THETA ** (-jnp.arange(0, ROPE_DIM, 2, dtype=F32) / ROPE_DIM)
    ang = positions.astype(F32)[..., None] * inv_freq
    cos, sin = jnp.cos(ang), jnp.sin(ang)
    return (jnp.concatenate([cos, cos, cos, cos], axis=-1),
            jnp.concatenate([-sin, sin, -sin, sin], axis=-1),
            jnp.swapaxes(cos, 1, 2), jnp.swapaxes(sin, 1, 2))


def kernel(x, p, positions, ffn1_norm, ffn1_w_gu, ffn1_w_down, mix_norm, w_in, conv_w, w_conv_out,
           q_norm, kv_norm, w_uq, w_ukv, w_mla_out, w_o, ffn2_norm, ffn2_w_gu, ffn2_w_down,
           ple_norm, w_ple_gate, w_ple_proj, final_norm):
    assert x.shape == (BATCH, SEQ, D_MODEL) and p.shape == (DEPTH, BATCH, SEQ, PLE_DIM)
    bf = lambda w: w.astype(BF16)
    row = lambda g: g.reshape(DEPTH, 1, -1)

    k_rot = w_in[:, :, OFF_KR:OFF_KR + ROPE_DIM]
    w_in_r = bf(jnp.concatenate([w_in[:, :, :OFF_KR], k_rot, k_rot, w_in[:, :, OFF_KR + ROPE_DIM:]], axis=-1))
    w_uq_h = w_uq.reshape(DEPTH, Q_LORA, N_HEADS, QK_DIM)
    w_uq_t = bf(jnp.swapaxes(jnp.concatenate([w_uq_h[..., :NOPE_DIM].reshape(DEPTH, Q_LORA, -1),
                                              w_uq_h[..., NOPE_DIM:].reshape(DEPTH, Q_LORA, -1)],
                                             axis=-1), 1, 2))
    w_ukv_h = w_ukv.reshape(DEPTH, KV_LORA, N_HEADS, NOPE_DIM + V_DIM)
    w_k = bf(w_ukv_h[..., :NOPE_DIM].reshape(DEPTH, KV_LORA, -1))
    w_vt = bf(jnp.swapaxes(w_ukv_h[..., NOPE_DIM:].reshape(DEPTH, KV_LORA, -1), 1, 2))
    ffn1_gu, ffn1_dn, ffn2_gu, ffn2_dn = bf(ffn1_w_gu), bf(ffn1_w_down), bf(ffn2_w_gu), bf(ffn2_w_down)
    w_co, w_mo, w_ob, w_pg, w_pp = bf(w_conv_out), bf(w_mla_out), bf(w_o), bf(w_ple_gate), bf(w_ple_proj)
    rope = _rope_tables(positions)
    p2 = p.reshape(DEPTH, TOKENS, PLE_DIM)

    h = x.reshape(TOKENS, D_MODEL)
    for layer in range(DEPTH):
        h = _ffn(h, layer, row(ffn1_norm), ffn1_gu, ffn1_dn)
        q, k, vt, gc, sg = _mix_proj(h.reshape(BATCH, SEQ, D_MODEL), layer, rope, row(mix_norm),
                                     w_in_r, conv_w, w_co, row(q_norm), row(kv_norm), w_uq_t, w_k, w_vt)
        o = _attention(q, k, vt)
        h = _mix_out(h, o.reshape(TOKENS, D_MODEL), gc.reshape(TOKENS, D_MODEL),
                     sg.reshape(TOKENS, D_MODEL), layer, w_mo, w_ob)
        h = _ffn(h, layer, row(ffn2_norm), ffn2_gu, ffn2_dn,
                 ple=(p2, row(ple_norm), w_pg, w_pp),
                 final_norm=final_norm.reshape(1, D_MODEL) if layer == DEPTH - 1 else None)
    return h.reshape(BATCH, SEQ, D_MODEL)
```

```python
import functools

import jax
import jax.numpy as jnp
from jax import lax
from jax.experimental import pallas as pl
from jax.experimental.pallas import tpu as pltpu

D_MODEL = 1024
BATCH = 8
SEQ = 4096
DEPTH = 4
CHUNK = 64
PLE_DIM = 256
D_FF = 2816
CONV_DIM = 512
CONV_K = 3
N_HEADS = 8
NOPE_DIM = 128
ROPE_DIM = 64
V_DIM = 128
Q_LORA = 384
KV_LORA = 256
ROPE_THETA = 10000.0
EPS = 1e-6
QK_DIM = NOPE_DIM + ROPE_DIM
ATTN_SCALE = QK_DIM ** -0.5
TOKENS = BATCH * SEQ

LANES = 128
QK_PAD = 2 * LANES
OFF_B, OFF_C, OFF_V = 0, CONV_DIM, 2 * CONV_DIM
OFF_Q = 3 * CONV_DIM
OFF_KV = OFF_Q + Q_LORA
OFF_KR = OFF_KV + KV_LORA
OFF_GC = OFF_KR + LANES
OFF_GM = OFF_GC + D_MODEL
IN_COLS_PAD = OFF_GM + D_MODEL

TM_FFN = 512
TM_MIX = 512
TQ = 512
TK = 512
HEAD_GROUP = 4
VT_BLK = 256
VT_PER_TK = TK // VT_BLK
V_ROWS = V_DIM + 16
LOG2E = 1.4426950408889634
CONV_HALO = 8
VMEM_LIMIT = 56 * 1024 * 1024

F32 = jnp.float32
BF16 = jnp.bfloat16


def _rmsnorm(x, g):
    return x * lax.rsqrt(jnp.mean(x * x, axis=-1, keepdims=True) + EPS) * g


def _dot(a, b):
    return jnp.dot(a, b, preferred_element_type=F32)


def _dot_nt(a, b):
    return lax.dot_general(a, b, (((1,), (1,)), ((), ())), preferred_element_type=F32)


def _const_spec(shape, layer):
    zeros = (0,) * len(shape)
    return pl.BlockSpec((None,) + tuple(shape), lambda *_: (layer,) + zeros,
                        pipeline_mode=pl.Buffered(1))


def _ffn_kernel(*refs, with_ple, with_final):
    x_ref, g_ref, wgu_ref, wd_ref = refs[:4]
    o_ref = refs[-1]
    x = x_ref[...]
    xn = _rmsnorm(x, g_ref[...]).astype(BF16)
    gu = _dot(xn, wgu_ref[...])
    g, u = gu[:, :D_FF], gu[:, D_FF:]
    a = (g * jax.nn.sigmoid(g) * u).astype(BF16)
    h = x + 0.5 * _dot(a, wd_ref[...])
    if with_ple:
        p_ref, pg_ref, wpg_ref, wpp_ref = refs[4:8]
        hn = _rmsnorm(h, pg_ref[...]).astype(BF16)
        gate = jax.nn.sigmoid(_dot(hn, wpg_ref[...]))
        h = h + gate * _dot(p_ref[...].astype(BF16), wpp_ref[...])
    if with_final:
        h = _rmsnorm(h, refs[8][...])
    o_ref[...] = h


def _ffn(h, layer, norm, w_gu, w_down, ple=None, final_norm=None):
    tile = pl.BlockSpec((TM_FFN, D_MODEL), lambda i: (i, 0))
    in_specs = [tile, _const_spec((1, D_MODEL), layer),
                _const_spec((D_MODEL, 2 * D_FF), layer), _const_spec((D_FF, D_MODEL), layer)]
    args = [h, norm, w_gu, w_down]
    if ple is not None:
        p, ple_norm, w_pg, w_pp = ple
        in_specs += [pl.BlockSpec((None, TM_FFN, PLE_DIM), lambda i: (layer, i, 0)),
                     _const_spec((1, D_MODEL), layer),
                     _const_spec((D_MODEL, D_MODEL), layer), _const_spec((PLE_DIM, D_MODEL), layer)]
        args += [p, ple_norm, w_pg, w_pp]
    if final_norm is not None:
        in_specs.append(pl.BlockSpec((1, D_MODEL), lambda i: (0, 0)))
        args.append(final_norm)
    return pl.pallas_call(
        functools.partial(_ffn_kernel, with_ple=ple is not None, with_final=final_norm is not None),
        grid=(TOKENS // TM_FFN,),
        in_specs=in_specs,
        out_specs=tile,
        out_shape=jax.ShapeDtypeStruct((TOKENS, D_MODEL), F32),
        compiler_params=pltpu.CompilerParams(dimension_semantics=("arbitrary",),
                                             vmem_limit_bytes=VMEM_LIMIT),
        name="ffn_ple" if ple is not None else "ffn",
    )(*args)


def _rope(x, cos_t, sin_t):
    width = x.shape[-1]
    first_half = (lax.broadcasted_iota(jnp.int32, x.shape, 1) % ROPE_DIM) < ROPE_DIM // 2
    swapped = jnp.where(first_half, pltpu.roll(x, width - ROPE_DIM // 2, 1),
                        pltpu.roll(x, ROPE_DIM // 2, 1))
    return x * cos_t + swapped * sin_t


def _mix_proj_kernel(h_ref, cos_ref, sin_ref, cost_ref, sint_ref, g_ref, win_ref, cw_ref, wco_ref,
                     qg_ref, kvg_ref, wuqt_ref, wk_ref, wvt_ref,
                     q_ref, k_ref, vt_ref, gc_ref, sg_ref, z_ref):
    groups = [slice(r * VT_BLK, (r + 1) * VT_BLK) for r in range(TM_MIX // VT_BLK)]

    @pl.when(pl.program_id(1) == 0)
    def _():
        z_ref[0:CONV_HALO, :] = jnp.zeros((CONV_HALO, CONV_DIM), F32)

    us = [_rmsnorm(h_ref[g, :], g_ref[...]).astype(BF16) for g in groups]
    projs = [_dot(u, win_ref[...]) for u in us]
    cw = cw_ref[...]
    rope0, half_rope = N_HEADS * NOPE_DIM, ROPE_DIM // 2
    ones_rows = (lax.broadcasted_iota(jnp.int32, (V_ROWS - V_DIM, VT_BLK), 0) == 0).astype(BF16)
    for r, (g, proj) in enumerate(zip(groups, projs)):
        z0 = CONV_HALO + r * VT_BLK
        z = proj[:, OFF_C:OFF_C + CONV_DIM] * proj[:, OFF_V:OFF_V + CONV_DIM]
        z_ref[z0:z0 + VT_BLK, :] = z
        y = (cw[0:1] * z_ref[z0 - 2:z0 - 2 + VT_BLK, :] + cw[1:2] * z_ref[z0 - 1:z0 - 1 + VT_BLK, :]
             + cw[2:3] * z)
        y_conv = _dot((proj[:, OFF_B:OFF_B + CONV_DIM] * y).astype(BF16), wco_ref[...])
        gc_ref[g, :] = (jax.nn.sigmoid(proj[:, OFF_GC:OFF_GC + D_MODEL]) * y_conv).astype(BF16)
        sg_ref[g, :] = jax.nn.sigmoid(proj[:, OFF_GM:OFF_GM + D_MODEL]).astype(BF16)

        qn = _rmsnorm(proj[:, OFF_Q:OFF_Q + Q_LORA], qg_ref[...]).astype(BF16)
        q_t = _dot_nt(wuqt_ref[...], qn) * (ATTN_SCALE * LOG2E)
        kvn = _rmsnorm(proj[:, OFF_KV:OFF_KV + KV_LORA], kvg_ref[...]).astype(BF16)
        k_nope = _dot(kvn, wk_ref[...])
        v_t = _dot_nt(wvt_ref[...], kvn)
        k_rope = _rope(proj[:, OFF_KR:OFF_KR + LANES], cos_ref[g, :], sin_ref[g, :])
        low_half = lax.broadcasted_iota(jnp.int32, k_rope.shape, 1) < ROPE_DIM
        k_rope = jnp.where(low_half, k_rope, 0.0).astype(BF16)
        cos_c, sin_c = cost_ref[:, g], sint_ref[:, g]
        for hd in range(N_HEADS):
            q_ref[hd, 0:NOPE_DIM, g] = q_t[hd * NOPE_DIM:(hd + 1) * NOPE_DIM, :].astype(BF16)
            x1 = q_t[rope0 + hd * ROPE_DIM:rope0 + hd * ROPE_DIM + half_rope, :]
            x2 = q_t[rope0 + hd * ROPE_DIM + half_rope:rope0 + (hd + 1) * ROPE_DIM, :]
            q_ref[hd, NOPE_DIM:NOPE_DIM + half_rope, g] = (x1 * cos_c - x2 * sin_c).astype(BF16)
            q_ref[hd, NOPE_DIM + half_rope:QK_DIM, g] = (x2 * cos_c + x1 * sin_c).astype(BF16)
            q_ref[hd, QK_DIM:QK_PAD, g] = jnp.zeros((QK_PAD - QK_DIM, VT_BLK), BF16)
            k_ref[hd, g, 0:LANES] = k_nope[:, hd * NOPE_DIM:(hd + 1) * NOPE_DIM].astype(BF16)
            k_ref[hd, g, LANES:QK_PAD] = k_rope
            vt_ref[hd, r, 0:V_DIM, :] = v_t[hd * V_DIM:(hd + 1) * V_DIM, :].astype(BF16)
            vt_ref[hd, r, V_DIM:V_ROWS, :] = ones_rows
    z_ref[0:CONV_HALO, :] = z_ref[TM_MIX:TM_MIX + CONV_HALO, :]


def _mix_proj(h, layer, rope, norm, w_in, conv_w, w_conv_out, q_norm, kv_norm, w_uqt, w_k, w_vt):
    n_t = SEQ // TM_MIX
    tok = lambda w: pl.BlockSpec((None, TM_MIX, w), lambda b, i: (b, i, 0))
    tok_t = lambda w: pl.BlockSpec((None, w, TM_MIX), lambda b, i: (b, 0, i))
    return pl.pallas_call(
        _mix_proj_kernel,
        grid=(BATCH, n_t),
        in_specs=[tok(D_MODEL), tok(LANES), tok(LANES), tok_t(ROPE_DIM // 2), tok_t(ROPE_DIM // 2),
                  _const_spec((1, D_MODEL), layer), _const_spec((D_MODEL, IN_COLS_PAD), layer),
                  _const_spec((CONV_K, CONV_DIM), layer), _const_spec((CONV_DIM, D_MODEL), layer),
                  _const_spec((1, Q_LORA), layer), _const_spec((1, KV_LORA), layer),
                  _const_spec((N_HEADS * QK_DIM, Q_LORA), layer),
                  _const_spec((KV_LORA, N_HEADS * NOPE_DIM), layer),
                  _const_spec((N_HEADS * V_DIM, KV_LORA), layer)],
        out_specs=[pl.BlockSpec((None, N_HEADS, QK_PAD, TM_MIX), lambda b, i: (b, 0, 0, i)),
                   pl.BlockSpec((None, N_HEADS, TM_MIX, QK_PAD), lambda b, i: (b, 0, i, 0)),
                   pl.BlockSpec((None, N_HEADS, TM_MIX // VT_BLK, V_ROWS, VT_BLK),
                                lambda b, i: (b, 0, i, 0, 0)),
                   tok(D_MODEL), tok(D_MODEL)],
        out_shape=[jax.ShapeDtypeStruct((BATCH, N_HEADS, QK_PAD, SEQ), BF16),
                   jax.ShapeDtypeStruct((BATCH, N_HEADS, SEQ, QK_PAD), BF16),
                   jax.ShapeDtypeStruct((BATCH, N_HEADS, SEQ // VT_BLK, V_ROWS, VT_BLK), BF16),
                   jax.ShapeDtypeStruct((BATCH, SEQ, D_MODEL), BF16),
                   jax.ShapeDtypeStruct((BATCH, SEQ, D_MODEL), BF16)],
        scratch_shapes=[pltpu.VMEM((TM_MIX + CONV_HALO, CONV_DIM), F32)],
        compiler_params=pltpu.CompilerParams(dimension_semantics=("arbitrary", "arbitrary"),
                                             vmem_limit_bytes=VMEM_LIMIT),
        name="mix_proj",
    )(h, *rope, norm, w_in, conv_w, w_conv_out, q_norm, kv_norm, w_uqt, w_k, w_vt)


def _attn_kernel(q_ref, qnext_ref, k_ref, vt_ref, o_ref, m_ref, acc_ref, s_ref, tmax_ref):
    n_full = pl.program_id(2)
    half = TQ // 2
    blk_mask = (lax.broadcasted_iota(jnp.int32, (half, half), 0) // CHUNK
                <= lax.broadcasted_iota(jnp.int32, (half, half), 1) // CHUNK)

    heads = range(HEAD_GROUP)
    m_ref[...] = jnp.full(m_ref.shape, -1e30, F32)
    acc_ref[...] = jnp.zeros(acc_ref.shape, F32)

    def scores(hd, j, queries=q_ref):
        k0 = pl.multiple_of(j * TK, TK)
        s = _dot(k_ref[hd, pl.ds(k0, TK), :], queries[hd])
        s_ref[hd] = s
        tmax_ref[hd] = jnp.max(s, axis=0, keepdims=True)

    def softmax(hd, s, tmax, cols=slice(None)):
        m = m_ref[hd, :, cols]
        m_new = jnp.maximum(m, tmax)
        alpha = jnp.exp2(m - m_new)
        m_ref[hd, :, cols] = m_new
        return jnp.exp2((s - m_new).astype(BF16)), alpha

    def weighted_values(hd, j, p, alpha, cols=slice(None)):
        pv = _dot(vt_ref[hd, j * VT_PER_TK], p[0:VT_BLK])
        for c in range(1, p.shape[0] // VT_BLK):
            pv += _dot(vt_ref[hd, j * VT_PER_TK + c], p[c * VT_BLK:(c + 1) * VT_BLK])
        acc_ref[hd, :, cols] = alpha * acc_ref[hd, :, cols] + pv

    @pl.when(n_full == 0)
    def _():
        for hd in heads:
            scores(hd, 0)

    @pl.loop(0, n_full)
    def _(j):
        for hd in heads:
            s, tmax = s_ref[hd], tmax_ref[hd]
            scores(hd, j + 1)
            p, alpha = softmax(hd, s, tmax)
            weighted_values(hd, j, p, alpha)

    lo, hi = slice(0, half), slice(half, TQ)

    def diagonal_tile(prefetch_next):
        for hd in heads:
            s_lo = jnp.where(blk_mask, s_ref[hd, lo, lo], -1e30)
            s_hi = jnp.concatenate([s_ref[hd, lo, hi], jnp.where(blk_mask, s_ref[hd, hi, hi], -1e30)],
                                   axis=0)
            if prefetch_next:
                scores(hd, 0, qnext_ref)
            p, alpha = softmax(hd, s_lo, jnp.max(s_lo, axis=0, keepdims=True), lo)
            weighted_values(hd, n_full, p, alpha, lo)
            p, alpha = softmax(hd, s_hi, jnp.max(s_hi, axis=0, keepdims=True), hi)
            weighted_values(hd, n_full, p, alpha, hi)

    last = pl.num_programs(2) - 1
    pl.when(n_full < last)(lambda: diagonal_tile(True))
    pl.when(n_full == last)(lambda: diagonal_tile(False))
    for hd in heads:
        out = acc_ref[hd, 0:V_DIM, :] / acc_ref[hd, V_DIM:V_DIM + 1, :]
        o_ref[:, hd * V_DIM:(hd + 1) * V_DIM] = out.T.astype(BF16)


def _attention(q, k, vt):
    assert TQ == TK and (TQ // 2) % VT_BLK == 0 and (TQ // 2) % CHUNK == 0
    return pl.pallas_call(
        _attn_kernel,
        grid=(BATCH, N_HEADS // HEAD_GROUP, SEQ // TQ),
        in_specs=[pl.BlockSpec((None, HEAD_GROUP, QK_PAD, TQ), lambda b, g, i: (b, g, 0, i)),
                  pl.BlockSpec((None, HEAD_GROUP, QK_PAD, TQ),
                               lambda b, g, i: (b, g, 0, jnp.minimum(i + 1, SEQ // TQ - 1))),
                  pl.BlockSpec((None, HEAD_GROUP, SEQ, QK_PAD), lambda b, g, i: (b, g, 0, 0)),
                  pl.BlockSpec((None, HEAD_GROUP, SEQ // VT_BLK, V_ROWS, VT_BLK),
                               lambda b, g, i: (b, g, 0, 0, 0))],
        out_specs=pl.BlockSpec((None, TQ, HEAD_GROUP * V_DIM), lambda b, g, i: (b, i, g)),
        out_shape=jax.ShapeDtypeStruct((BATCH, SEQ, N_HEADS * V_DIM), BF16),
        scratch_shapes=[pltpu.VMEM((HEAD_GROUP, 1, TQ), F32),
                        pltpu.VMEM((HEAD_GROUP, V_ROWS, TQ), F32),
                        pltpu.VMEM((HEAD_GROUP, TK, TQ), F32), pltpu.VMEM((HEAD_GROUP, 1, TQ), F32)],
        compiler_params=pltpu.CompilerParams(dimension_semantics=("arbitrary",) * 3,
                                             vmem_limit_bytes=VMEM_LIMIT),
        name="attention",
    )(q, q, k, vt)


def _mix_out_kernel(h_ref, o_ref, gc_ref, sg_ref, wmo_ref, wo_ref, out_ref):
    y_mla = _dot(o_ref[...], wmo_ref[...])
    merged = (gc_ref[...] + sg_ref[...] * y_mla).astype(BF16)
    out_ref[...] = h_ref[...] + _dot(merged, wo_ref[...])


def _mix_out(h, o, gc, sg, layer, w_mla_out, w_o):
    tile = pl.BlockSpec((TM_FFN, D_MODEL), lambda i: (i, 0))
    return pl.pallas_call(
        _mix_out_kernel,
        grid=(TOKENS // TM_FFN,),
        in_specs=[tile, tile, tile, tile,
                  _const_spec((D_MODEL, D_MODEL), layer), _const_spec((D_MODEL, D_MODEL), layer)],
        out_specs=tile,
        out_shape=jax.ShapeDtypeStruct((TOKENS, D_MODEL), F32),
        compiler_params=pltpu.CompilerParams(dimension_semantics=("arbitrary",),
                                             vmem_limit_bytes=VMEM_LIMIT),
        name="mix_out",
    )(h, o, gc, sg, w_mla_out, w_o)


def _rope_tables(positions):
    inv_freq = ROPE_THETA ** (-jnp.arange(0, ROPE_DIM, 2, dtype=F32) / ROPE_DIM)
    ang = positions.astype(F32)[..., None] * inv_freq
    cos, sin = lax.optimization_barrier((jnp.cos(ang), jnp.sin(ang)))
    return (jnp.concatenate([cos, cos, cos, cos], axis=-1),
            jnp.concatenate([-sin, sin, -sin, sin], axis=-1),
            jnp.swapaxes(cos, 1, 2), jnp.swapaxes(sin, 1, 2))


def kernel(x, p, positions, ffn1_norm, ffn1_w_gu, ffn1_w_down, mix_norm, w_in, conv_w, w_conv_out,
           q_norm, kv_norm, w_uq, w_ukv, w_mla_out, w_o, ffn2_norm, ffn2_w_gu, ffn2_w_down,
           ple_norm, w_ple_gate, w_ple_proj, final_norm):
    assert x.shape == (BATCH, SEQ, D_MODEL) and p.shape == (DEPTH, BATCH, SEQ, PLE_DIM)
    bf = lambda w: w.astype(BF16)
    row = lambda g: g.reshape(DEPTH, 1, -1)

    k_rot = w_in[:, :, OFF_KR:OFF_KR + ROPE_DIM]
    w_in_r = bf(jnp.concatenate([w_in[:, :, :OFF_KR], k_rot, k_rot, w_in[:, :, OFF_KR + ROPE_DIM:]], axis=-1))
    w_uq_h = w_uq.reshape(DEPTH, Q_LORA, N_HEADS, QK_DIM)
    w_uq_t = bf(jnp.swapaxes(jnp.concatenate([w_uq_h[..., :NOPE_DIM].reshape(DEPTH, Q_LORA, -1),
                                              w_uq_h[..., NOPE_DIM:].reshape(DEPTH, Q_LORA, -1)],
                                             axis=-1), 1, 2))
    w_ukv_h = w_ukv.reshape(DEPTH, KV_LORA, N_HEADS, NOPE_DIM + V_DIM)
    w_k = bf(w_ukv_h[..., :NOPE_DIM].reshape(DEPTH, KV_LORA, -1))
    w_vt = bf(jnp.swapaxes(w_ukv_h[..., NOPE_DIM:].reshape(DEPTH, KV_LORA, -1), 1, 2))
    ffn1_gu, ffn1_dn, ffn2_gu, ffn2_dn = bf(ffn1_w_gu), bf(ffn1_w_down), bf(ffn2_w_gu), bf(ffn2_w_down)
    w_co, w_mo, w_ob, w_pg, w_pp = bf(w_conv_out), bf(w_mla_out), bf(w_o), bf(w_ple_gate), bf(w_ple_proj)
    rope = _rope_tables(positions)
    p2 = p.reshape(DEPTH, TOKENS, PLE_DIM)

    h = x.reshape(TOKENS, D_MODEL)
    for layer in range(DEPTH):
        h = _ffn(h, layer, row(ffn1_norm), ffn1_gu, ffn1_dn)
        q, k, vt, gc, sg = _mix_proj(h.reshape(BATCH, SEQ, D_MODEL), layer, rope, row(mix_norm),
                                     w_in_r, conv_w, w_co, row(q_norm), row(kv_norm), w_uq_t, w_k, w_vt)
        o = _attention(q, k, vt)
        h = _mix_out(h, o.reshape(TOKENS, D_MODEL), gc.reshape(TOKENS, D_MODEL),
                     sg.reshape(TOKENS, D_MODEL), layer, w_mo, w_ob)
        h = _ffn(h, layer, row(ffn2_norm), ffn2_gu, ffn2_dn,
                 ple=(p2, row(ple_norm), w_pg, w_pp),
                 final_norm=final_norm.reshape(1, D_MODEL) if layer == DEPTH - 1 else None)
    return h.reshape(BATCH, SEQ, D_MODEL)
```

```python
import functools

import jax
import jax.numpy as jnp
from jax import lax
from jax.experimental import pallas as pl
from jax.experimental.pallas import tpu as pltpu

D_MODEL = 1024
BATCH = 8
SEQ = 4096
DEPTH = 4
CHUNK = 64
PLE_DIM = 256
D_FF = 2816
CONV_DIM = 512
CONV_K = 3
N_HEADS = 8
NOPE_DIM = 128
ROPE_DIM = 64
V_DIM = 128
Q_LORA = 384
KV_LORA = 256
ROPE_THETA = 10000.0
EPS = 1e-6
QK_DIM = NOPE_DIM + ROPE_DIM
ATTN_SCALE = QK_DIM ** -0.5
TOKENS = BATCH * SEQ

LANES = 128
QK_PAD = 2 * LANES
OFF_B, OFF_C, OFF_V = 0, CONV_DIM, 2 * CONV_DIM
OFF_Q = 3 * CONV_DIM
OFF_KV = OFF_Q + Q_LORA
OFF_KR = OFF_KV + KV_LORA
OFF_GC = OFF_KR + LANES
OFF_GM = OFF_GC + D_MODEL
IN_COLS_PAD = OFF_GM + D_MODEL

TM_FFN = 512
FFN_GROUP = 256
TM_MIX = 512
TQ = 512
TK = 512
HEAD_GROUP = 4
VT_BLK = 256
VT_PER_TK = TK // VT_BLK
V_ROWS = V_DIM + 16
LOG2E = 1.4426950408889634
CONV_HALO = 8
VMEM_LIMIT = 56 * 1024 * 1024

F32 = jnp.float32
BF16 = jnp.bfloat16


def _rmsnorm(x, g):
    return x * lax.rsqrt(jnp.mean(x * x, axis=-1, keepdims=True) + EPS) * g


def _dot(a, b):
    return jnp.dot(a, b, preferred_element_type=F32)


def _dot_nt(a, b):
    return lax.dot_general(a, b, (((1,), (1,)), ((), ())), preferred_element_type=F32)


def _const_spec(shape, layer):
    zeros = (0,) * len(shape)
    return pl.BlockSpec((None,) + tuple(shape), lambda *_: (layer,) + zeros,
                        pipeline_mode=pl.Buffered(1))


def _ffn_kernel(*refs, with_ple, with_final):
    x_ref, g_ref, wgu_ref, wd_ref = refs[:4]
    o_ref = refs[-1]
    groups = [slice(r * FFN_GROUP, (r + 1) * FFN_GROUP) for r in range(TM_FFN // FFN_GROUP)]
    xs = [x_ref[g, :] for g in groups]
    xns = [_rmsnorm(x, g_ref[...]).astype(BF16) for x in xs]
    gus = [_dot(xn, wgu_ref[...]) for xn in xns]
    acts = [(gu[:, :D_FF] * jax.nn.sigmoid(gu[:, :D_FF]) * gu[:, D_FF:]).astype(BF16) for gu in gus]
    hs = [x + 0.5 * _dot(a, wd_ref[...]) for x, a in zip(xs, acts)]
    if with_ple:
        p_ref, pg_ref, wpg_ref, wpp_ref = refs[4:8]
        hns = [_rmsnorm(h, pg_ref[...]).astype(BF16) for h in hs]
        gates = [jax.nn.sigmoid(_dot(hn, wpg_ref[...])) for hn in hns]
        hs = [h + gate * _dot(p_ref[g, :].astype(BF16), wpp_ref[...])
              for h, gate, g in zip(hs, gates, groups)]
    if with_final:
        hs = [_rmsnorm(h, refs[8][...]) for h in hs]
    for g, h in zip(groups, hs):
        o_ref[g, :] = h


def _ffn(h, layer, norm, w_gu, w_down, ple=None, final_norm=None):
    tile = pl.BlockSpec((TM_FFN, D_MODEL), lambda i: (i, 0))
    in_specs = [tile, _const_spec((1, D_MODEL), layer),
                _const_spec((D_MODEL, 2 * D_FF), layer), _const_spec((D_FF, D_MODEL), layer)]
    args = [h, norm, w_gu, w_down]
    if ple is not None:
        p, ple_norm, w_pg, w_pp = ple
        in_specs += [pl.BlockSpec((None, TM_FFN, PLE_DIM), lambda i: (layer, i, 0)),
                     _const_spec((1, D_MODEL), layer),
                     _const_spec((D_MODEL, D_MODEL), layer), _const_spec((PLE_DIM, D_MODEL), layer)]
        args += [p, ple_norm, w_pg, w_pp]
    if final_norm is not None:
        in_specs.append(pl.BlockSpec((1, D_MODEL), lambda i: (0, 0)))
        args.append(final_norm)
    return pl.pallas_call(
        functools.partial(_ffn_kernel, with_ple=ple is not None, with_final=final_norm is not None),
        grid=(TOKENS // TM_FFN,),
        in_specs=in_specs,
        out_specs=tile,
        out_shape=jax.ShapeDtypeStruct((TOKENS, D_MODEL), F32),
        compiler_params=pltpu.CompilerParams(dimension_semantics=("arbitrary",),
                                             vmem_limit_bytes=VMEM_LIMIT),
        name="ffn_ple" if ple is not None else "ffn",
    )(*args)


def _rope(x, cos_t, sin_t):
    width = x.shape[-1]
    first_half = (lax.broadcasted_iota(jnp.int32, x.shape, 1) % ROPE_DIM) < ROPE_DIM // 2
    swapped = jnp.where(first_half, pltpu.roll(x, width - ROPE_DIM // 2, 1),
                        pltpu.roll(x, ROPE_DIM // 2, 1))
    return x * cos_t + swapped * sin_t


def _mix_proj_kernel(h_ref, cos_ref, sin_ref, cost_ref, sint_ref, g_ref, win_ref, cw_ref, wco_ref,
                     qg_ref, kvg_ref, wuqt_ref, wk_ref, wvt_ref,
                     q_ref, k_ref, vt_ref, gc_ref, sg_ref, z_ref):
    groups = [slice(r * VT_BLK, (r + 1) * VT_BLK) for r in range(TM_MIX // VT_BLK)]

    @pl.when(pl.program_id(1) == 0)
    def _():
        z_ref[0:CONV_HALO, :] = jnp.zeros((CONV_HALO, CONV_DIM), F32)

    us = [_rmsnorm(h_ref[g, :], g_ref[...]).astype(BF16) for g in groups]
    projs = [_dot(u, win_ref[...]) for u in us]
    cw = cw_ref[...]
    rope0, half_rope = N_HEADS * NOPE_DIM, ROPE_DIM // 2
    ones_rows = (lax.broadcasted_iota(jnp.int32, (V_ROWS - V_DIM, VT_BLK), 0) == 0).astype(BF16)
    for r, (g, proj) in enumerate(zip(groups, projs)):
        z0 = CONV_HALO + r * VT_BLK
        z = proj[:, OFF_C:OFF_C + CONV_DIM] * proj[:, OFF_V:OFF_V + CONV_DIM]
        z_ref[z0:z0 + VT_BLK, :] = z
        y = (cw[0:1] * z_ref[z0 - 2:z0 - 2 + VT_BLK, :] + cw[1:2] * z_ref[z0 - 1:z0 - 1 + VT_BLK, :]
             + cw[2:3] * z)
        y_conv = _dot((proj[:, OFF_B:OFF_B + CONV_DIM] * y).astype(BF16), wco_ref[...])
        gc_ref[g, :] = (jax.nn.sigmoid(proj[:, OFF_GC:OFF_GC + D_MODEL]) * y_conv).astype(BF16)
        sg_ref[g, :] = jax.nn.sigmoid(proj[:, OFF_GM:OFF_GM + D_MODEL]).astype(BF16)

        qn = _rmsnorm(proj[:, OFF_Q:OFF_Q + Q_LORA], qg_ref[...]).astype(BF16)
        q_t = _dot_nt(wuqt_ref[...], qn) * (ATTN_SCALE * LOG2E)
        kvn = _rmsnorm(proj[:, OFF_KV:OFF_KV + KV_LORA], kvg_ref[...]).astype(BF16)
        k_nope = _dot(kvn, wk_ref[...])
        v_t = _dot_nt(wvt_ref[...], kvn)
        k_rope = _rope(proj[:, OFF_KR:OFF_KR + LANES], cos_ref[g, :], sin_ref[g, :])
        low_half = lax.broadcasted_iota(jnp.int32, k_rope.shape, 1) < ROPE_DIM
        k_rope = jnp.where(low_half, k_rope, 0.0).astype(BF16)
        cos_c, sin_c = cost_ref[:, g], sint_ref[:, g]
        for hd in range(N_HEADS):
            q_ref[hd, 0:NOPE_DIM, g] = q_t[hd * NOPE_DIM:(hd + 1) * NOPE_DIM, :].astype(BF16)
            x1 = q_t[rope0 + hd * ROPE_DIM:rope0 + hd * ROPE_DIM + half_rope, :]
            x2 = q_t[rope0 + hd * ROPE_DIM + half_rope:rope0 + (hd + 1) * ROPE_DIM, :]
            q_ref[hd, NOPE_DIM:NOPE_DIM + half_rope, g] = (x1 * cos_c - x2 * sin_c).astype(BF16)
            q_ref[hd, NOPE_DIM + half_rope:QK_DIM, g] = (x2 * cos_c + x1 * sin_c).astype(BF16)
            q_ref[hd, QK_DIM:QK_PAD, g] = jnp.zeros((QK_PAD - QK_DIM, VT_BLK), BF16)
            k_ref[hd, g, 0:LANES] = k_nope[:, hd * NOPE_DIM:(hd + 1) * NOPE_DIM].astype(BF16)
            k_ref[hd, g, LANES:QK_PAD] = k_rope
            vt_ref[hd, r, 0:V_DIM, :] = v_t[hd * V_DIM:(hd + 1) * V_DIM, :].astype(BF16)
            vt_ref[hd, r, V_DIM:V_ROWS, :] = ones_rows
    z_ref[0:CONV_HALO, :] = z_ref[TM_MIX:TM_MIX + CONV_HALO, :]


def _mix_proj(h, layer, rope, norm, w_in, conv_w, w_conv_out, q_norm, kv_norm, w_uqt, w_k, w_vt):
    n_t = SEQ // TM_MIX
    tok = lambda w: pl.BlockSpec((None, TM_MIX, w), lambda b, i: (b, i, 0))
    tok_t = lambda w: pl.BlockSpec((None, w, TM_MIX), lambda b, i: (b, 0, i))
    return pl.pallas_call(
        _mix_proj_kernel,
        grid=(BATCH, n_t),
        in_specs=[tok(D_MODEL), tok(LANES), tok(LANES), tok_t(ROPE_DIM // 2), tok_t(ROPE_DIM // 2),
                  _const_spec((1, D_MODEL), layer), _const_spec((D_MODEL, IN_COLS_PAD), layer),
                  _const_spec((CONV_K, CONV_DIM), layer), _const_spec((CONV_DIM, D_MODEL), layer),
                  _const_spec((1, Q_LORA), layer), _const_spec((1, KV_LORA), layer),
                  _const_spec((N_HEADS * QK_DIM, Q_LORA), layer),
                  _const_spec((KV_LORA, N_HEADS * NOPE_DIM), layer),
                  _const_spec((N_HEADS * V_DIM, KV_LORA), layer)],
        out_specs=[pl.BlockSpec((None, N_HEADS, QK_PAD, TM_MIX), lambda b, i: (b, 0, 0, i)),
                   pl.BlockSpec((None, N_HEADS, TM_MIX, QK_PAD), lambda b, i: (b, 0, i, 0)),
                   pl.BlockSpec((None, N_HEADS, TM_MIX // VT_BLK, V_ROWS, VT_BLK),
                                lambda b, i: (b, 0, i, 0, 0)),
                   tok(D_MODEL), tok(D_MODEL)],
        out_shape=[jax.ShapeDtypeStruct((BATCH, N_HEADS, QK_PAD, SEQ), BF16),
                   jax.ShapeDtypeStruct((BATCH, N_HEADS, SEQ, QK_PAD), BF16),
                   jax.ShapeDtypeStruct((BATCH, N_HEADS, SEQ // VT_BLK, V_ROWS, VT_BLK), BF16),
                   jax.ShapeDtypeStruct((BATCH, SEQ, D_MODEL), BF16),
                   jax.ShapeDtypeStruct((BATCH, SEQ, D_MODEL), BF16)],
        scratch_shapes=[pltpu.VMEM((TM_MIX + CONV_HALO, CONV_DIM), F32)],
        compiler_params=pltpu.CompilerParams(dimension_semantics=("arbitrary", "arbitrary"),
                                             vmem_limit_bytes=VMEM_LIMIT),
        name="mix_proj",
    )(h, *rope, norm, w_in, conv_w, w_conv_out, q_norm, kv_norm, w_uqt, w_k, w_vt)


def _attn_kernel(q_ref, qnext_ref, k_ref, vt_ref, o_ref, m_ref, acc_ref, s_ref, tmax_ref):
    n_full = pl.program_id(2)
    half = TQ // 2
    blk_mask = (lax.broadcasted_iota(jnp.int32, (half, half), 0) // CHUNK
                <= lax.broadcasted_iota(jnp.int32, (half, half), 1) // CHUNK)

    heads = range(HEAD_GROUP)
    m_ref[...] = jnp.full(m_ref.shape, -1e30, F32)
    acc_ref[...] = jnp.zeros(acc_ref.shape, F32)

    def scores(hd, j, queries=q_ref):
        k0 = pl.multiple_of(j * TK, TK)
        s = _dot(k_ref[hd, pl.ds(k0, TK), :], queries[hd])
        s_ref[hd] = s
        tmax_ref[hd] = jnp.max(s, axis=0, keepdims=True)

    def softmax(hd, s, tmax, cols=slice(None)):
        m = m_ref[hd, :, cols]
        m_new = jnp.maximum(m, tmax)
        alpha = jnp.exp2(m - m_new)
        m_ref[hd, :, cols] = m_new
        return jnp.exp2((s - m_new).astype(BF16)), alpha

    def weighted_values(hd, j, p, alpha, cols=slice(None)):
        pv = _dot(vt_ref[hd, j * VT_PER_TK], p[0:VT_BLK])
        for c in range(1, p.shape[0] // VT_BLK):
            pv += _dot(vt_ref[hd, j * VT_PER_TK + c], p[c * VT_BLK:(c + 1) * VT_BLK])
        acc_ref[hd, :, cols] = alpha * acc_ref[hd, :, cols] + pv

    @pl.when(n_full == 0)
    def _():
        for hd in heads:
            scores(hd, 0)

    @pl.loop(0, n_full)
    def _(j):
        for hd in heads:
            s, tmax = s_ref[hd], tmax_ref[hd]
            scores(hd, j + 1)
            p, alpha = softmax(hd, s, tmax)
            weighted_values(hd, j, p, alpha)

    lo, hi = slice(0, half), slice(half, TQ)

    def diagonal_tile(prefetch_next):
        for hd in heads:
            s_lo = jnp.where(blk_mask, s_ref[hd, lo, lo], -1e30)
            s_hi = jnp.concatenate([s_ref[hd, lo, hi], jnp.where(blk_mask, s_ref[hd, hi, hi], -1e30)],
                                   axis=0)
            if prefetch_next:
                scores(hd, 0, qnext_ref)
            p, alpha = softmax(hd, s_lo, jnp.max(s_lo, axis=0, keepdims=True), lo)
            weighted_values(hd, n_full, p, alpha, lo)
            p, alpha = softmax(hd, s_hi, jnp.max(s_hi, axis=0, keepdims=True), hi)
            weighted_values(hd, n_full, p, alpha, hi)

    last = pl.num_programs(2) - 1
    pl.when(n_full < last)(lambda: diagonal_tile(True))
    pl.when(n_full == last)(lambda: diagonal_tile(False))
    for hd in heads:
        out = acc_ref[hd, 0:V_DIM, :] / acc_ref[hd, V_DIM:V_DIM + 1, :]
        o_ref[:, hd * V_DIM:(hd + 1) * V_DIM] = out.T.astype(BF16)


def _attention(q, k, vt):
    assert TQ == TK and (TQ // 2) % VT_BLK == 0 and (TQ // 2) % CHUNK == 0
    return pl.pallas_call(
        _attn_kernel,
        grid=(BATCH, N_HEADS // HEAD_GROUP, SEQ // TQ),
        in_specs=[pl.BlockSpec((None, HEAD_GROUP, QK_PAD, TQ), lambda b, g, i: (b, g, 0, i)),
                  pl.BlockSpec((None, HEAD_GROUP, QK_PAD, TQ),
                               lambda b, g, i: (b, g, 0, jnp.minimum(i + 1, SEQ // TQ - 1))),
                  pl.BlockSpec((None, HEAD_GROUP, SEQ, QK_PAD), lambda b, g, i: (b, g, 0, 0)),
                  pl.BlockSpec((None, HEAD_GROUP, SEQ // VT_BLK, V_ROWS, VT_BLK),
                               lambda b, g, i: (b, g, 0, 0, 0))],
        out_specs=pl.BlockSpec((None, TQ, HEAD_GROUP * V_DIM), lambda b, g, i: (b, i, g)),
        out_shape=jax.ShapeDtypeStruct((BATCH, SEQ, N_HEADS * V_DIM), BF16),
        scratch_shapes=[pltpu.VMEM((HEAD_GROUP, 1, TQ), F32),
                        pltpu.VMEM((HEAD_GROUP, V_ROWS, TQ), F32),
                        pltpu.VMEM((HEAD_GROUP, TK, TQ), F32), pltpu.VMEM((HEAD_GROUP, 1, TQ), F32)],
        compiler_params=pltpu.CompilerParams(dimension_semantics=("arbitrary",) * 3,
                                             vmem_limit_bytes=VMEM_LIMIT),
        name="attention",
    )(q, q, k, vt)


def _mix_out_kernel(h_ref, o_ref, gc_ref, sg_ref, wmo_ref, wo_ref, out_ref):
    y_mla = _dot(o_ref[...], wmo_ref[...])
    merged = (gc_ref[...] + sg_ref[...] * y_mla).astype(BF16)
    out_ref[...] = h_ref[...] + _dot(merged, wo_ref[...])


def _mix_out(h, o, gc, sg, layer, w_mla_out, w_o):
    tile = pl.BlockSpec((TM_FFN, D_MODEL), lambda i: (i, 0))
    return pl.pallas_call(
        _mix_out_kernel,
        grid=(TOKENS // TM_FFN,),
        in_specs=[tile, tile, tile, tile,
                  _const_spec((D_MODEL, D_MODEL), layer), _const_spec((D_MODEL, D_MODEL), layer)],
        out_specs=tile,
        out_shape=jax.ShapeDtypeStruct((TOKENS, D_MODEL), F32),
        compiler_params=pltpu.CompilerParams(dimension_semantics=("arbitrary",),
                                             vmem_limit_bytes=VMEM_LIMIT),
        name="mix_out",
    )(h, o, gc, sg, w_mla_out, w_o)


def _rope_tables(positions):
    inv_freq = ROPE_THETA ** (-jnp.arange(0, ROPE_DIM, 2, dtype=F32) / ROPE_DIM)
    ang = positions.astype(F32)[..., None] * inv_freq
    cos, sin = lax.optimization_barrier((jnp.cos(ang), jnp.sin(ang)))
    return (jnp.concatenate([cos, cos, cos, cos], axis=-1),
            jnp.concatenate([-sin, sin, -sin, sin], axis=-1),
            jnp.swapaxes(cos, 1, 2), jnp.swapaxes(sin, 1, 2))


def kernel(x, p, positions, ffn1_norm, ffn1_w_gu, ffn1_w_down, mix_norm, w_in, conv_w, w_conv_out,
           q_norm, kv_norm, w_uq, w_ukv, w_mla_out, w_o, ffn2_norm, ffn2_w_gu, ffn2_w_down,
           ple_norm, w_ple_gate, w_ple_proj, final_norm):
    assert x.shape == (BATCH, SEQ, D_MODEL) and p.shape == (DEPTH, BATCH, SEQ, PLE_DIM)
    bf = lambda w: w.astype(BF16)
    row = lambda g: g.reshape(DEPTH, 1, -1)

    k_rot = w_in[:, :, OFF_KR:OFF_KR + ROPE_DIM]
    w_in_r = bf(jnp.concatenate([w_in[:, :, :OFF_KR], k_rot, k_rot, w_in[:, :, OFF_KR + ROPE_DIM:]], axis=-1))
    w_uq_h = w_uq.reshape(DEPTH, Q_LORA, N_HEADS, QK_DIM)
    w_uq_t = bf(jnp.swapaxes(jnp.concatenate([w_uq_h[..., :NOPE_DIM].reshape(DEPTH, Q_LORA, -1),
                                              w_uq_h[..., NOPE_DIM:].reshape(DEPTH, Q_LORA, -1)],
                                             axis=-1), 1, 2))
    w_ukv_h = w_ukv.reshape(DEPTH, KV_LORA, N_HEADS, NOPE_DIM + V_DIM)
    w_k = bf(w_ukv_h[..., :NOPE_DIM].reshape(DEPTH, KV_LORA, -1))
    w_vt = bf(jnp.swapaxes(w_ukv_h[..., NOPE_DIM:].reshape(DEPTH, KV_LORA, -1), 1, 2))
    ffn1_gu, ffn1_dn, ffn2_gu, ffn2_dn = bf(ffn1_w_gu), bf(ffn1_w_down), bf(ffn2_w_gu), bf(ffn2_w_down)
    w_co, w_mo, w_ob, w_pg, w_pp = bf(w_conv_out), bf(w_mla_out), bf(w_o), bf(w_ple_gate), bf(w_ple_proj)
    rope = _rope_tables(positions)
    p2 = p.reshape(DEPTH, TOKENS, PLE_DIM)

    h = x.reshape(TOKENS, D_MODEL)
    for layer in range(DEPTH):
        h = _ffn(h, layer, row(ffn1_norm), ffn1_gu, ffn1_dn)
        q, k, vt, gc, sg = _mix_proj(h.reshape(BATCH, SEQ, D_MODEL), layer, rope, row(mix_norm),
                                     w_in_r, conv_w, w_co, row(q_norm), row(kv_norm), w_uq_t, w_k, w_vt)
        o = _attention(q, k, vt)
        h = _mix_out(h, o.reshape(TOKENS, D_MODEL), gc.reshape(TOKENS, D_MODEL),
                     sg.reshape(TOKENS, D_MODEL), layer, w_mo, w_ob)
        h = _ffn(h, layer, row(ffn2_norm), ffn2_gu, ffn2_dn,
                 ple=(p2, row(ple_norm), w_pg, w_pp),
                 final_norm=final_norm.reshape(1, D_MODEL) if layer == DEPTH - 1 else None)
    return h.reshape(BATCH, SEQ, D_MODEL)
```

```python
import functools

import jax
import jax.numpy as jnp
from jax import lax
from jax.experimental import pallas as pl
from jax.experimental.pallas import tpu as pltpu

D_MODEL = 1024
BATCH = 8
SEQ = 4096
DEPTH = 4
CHUNK = 64
PLE_DIM = 256
D_FF = 2816
CONV_DIM = 512
CONV_K = 3
N_HEADS = 8
NOPE_DIM = 128
ROPE_DIM = 64
V_DIM = 128
Q_LORA = 384
KV_LORA = 256
ROPE_THETA = 10000.0
EPS = 1e-6
QK_DIM = NOPE_DIM + ROPE_DIM
ATTN_SCALE = QK_DIM ** -0.5
TOKENS = BATCH * SEQ

LANES = 128
QK_PAD = 2 * LANES
OFF_B, OFF_C, OFF_V = 0, CONV_DIM, 2 * CONV_DIM
OFF_Q = 3 * CONV_DIM
OFF_KV = OFF_Q + Q_LORA
OFF_KR = OFF_KV + KV_LORA
OFF_GC = OFF_KR + LANES
OFF_GM = OFF_GC + D_MODEL
IN_COLS_PAD = OFF_GM + D_MODEL

TM_FFN = 512
FFN_GROUP = 256
FF_CHUNKS = ((0, 1536), (1536, D_FF))
TM_MIX = 512
TQ = 512
TK = 512
HEAD_GROUP = 4
VT_BLK = 256
VT_PER_TK = TK // VT_BLK
V_ROWS = V_DIM + 16
LOG2E = 1.4426950408889634
CONV_HALO = 8
VMEM_LIMIT = 56 * 1024 * 1024

F32 = jnp.float32
BF16 = jnp.bfloat16


def _rmsnorm(x, g):
    return x * lax.rsqrt(jnp.mean(x * x, axis=-1, keepdims=True) + EPS) * g


def _dot(a, b):
    return jnp.dot(a, b, preferred_element_type=F32)


def _dot_nt(a, b):
    return lax.dot_general(a, b, (((1,), (1,)), ((), ())), preferred_element_type=F32)


def _const_spec(shape, layer):
    zeros = (0,) * len(shape)
    return pl.BlockSpec((None,) + tuple(shape), lambda *_: (layer,) + zeros,
                        pipeline_mode=pl.Buffered(1))


def _ffn_kernel(*refs, with_mix, with_ple, with_final):
    refs = list(refs)
    o_ref = refs.pop()
    x_ref = refs.pop(0)
    groups = [slice(r * FFN_GROUP, (r + 1) * FFN_GROUP) for r in range(TM_FFN // FFN_GROUP)]
    xs = [x_ref[g, :] for g in groups]
    if with_mix:
        attn_ref, gc_ref, sg_ref, wmo_ref, wo_ref = refs[:5]
        del refs[:5]
        y_mlas = [_dot(attn_ref[g, :], wmo_ref[...]) for g in groups]
        merged = [(gc_ref[g, :] + sg_ref[g, :] * y).astype(BF16) for g, y in zip(groups, y_mlas)]
        xs = [x + _dot(m, wo_ref[...]) for x, m in zip(xs, merged)]
    g_ref, wgu_ref, wd_ref = refs[:3]
    del refs[:3]
    xns = [_rmsnorm(x, g_ref[...]).astype(BF16) for x in xs]
    ys = None
    for c0, c1 in (FF_CHUNKS if with_mix else ((0, D_FF),)):
        gs = [_dot(xn, wgu_ref[:, c0:c1]) for xn in xns]
        us = [_dot(xn, wgu_ref[:, D_FF + c0:D_FF + c1]) for xn in xns]
        acts = [(g * jax.nn.sigmoid(g) * u).astype(BF16) for g, u in zip(gs, us)]
        parts = [_dot(a, wd_ref[c0:c1, :]) for a in acts]
        ys = parts if ys is None else [y + part for y, part in zip(ys, parts)]
    hs = [x + 0.5 * y for x, y in zip(xs, ys)]
    if with_ple:
        p_ref, pg_ref, wpg_ref, wpp_ref = refs[:4]
        del refs[:4]
        hns = [_rmsnorm(h, pg_ref[...]).astype(BF16) for h in hs]
        gates = [jax.nn.sigmoid(_dot(hn, wpg_ref[...])) for hn in hns]
        hs = [h + gate * _dot(p_ref[g, :].astype(BF16), wpp_ref[...])
              for h, gate, g in zip(hs, gates, groups)]
    if with_final:
        hs = [_rmsnorm(h, refs[0][...]) for h in hs]
    for g, h in zip(groups, hs):
        o_ref[g, :] = h


def _ffn(h, layer, norm, w_gu, w_down, mix=None, ple=None, final_norm=None):
    tile = pl.BlockSpec((TM_FFN, D_MODEL), lambda i: (i, 0))
    in_specs, args = [tile], [h]
    if mix is not None:
        attn, gc, sg, w_mla_out, w_o = mix
        in_specs += [tile, tile, tile,
                     _const_spec((D_MODEL, D_MODEL), layer), _const_spec((D_MODEL, D_MODEL), layer)]
        args += [attn, gc, sg, w_mla_out, w_o]
    in_specs += [_const_spec((1, D_MODEL), layer),
                 _const_spec((D_MODEL, 2 * D_FF), layer), _const_spec((D_FF, D_MODEL), layer)]
    args += [norm, w_gu, w_down]
    if ple is not None:
        p, ple_norm, w_pg, w_pp = ple
        in_specs += [pl.BlockSpec((None, TM_FFN, PLE_DIM), lambda i: (layer, i, 0)),
                     _const_spec((1, D_MODEL), layer),
                     _const_spec((D_MODEL, D_MODEL), layer), _const_spec((PLE_DIM, D_MODEL), layer)]
        args += [p, ple_norm, w_pg, w_pp]
    if final_norm is not None:
        in_specs.append(pl.BlockSpec((1, D_MODEL), lambda i: (0, 0)))
        args.append(final_norm)
    return pl.pallas_call(
        functools.partial(_ffn_kernel, with_mix=mix is not None, with_ple=ple is not None,
                          with_final=final_norm is not None),
        grid=(TOKENS // TM_FFN,),
        in_specs=in_specs,
        out_specs=tile,
        out_shape=jax.ShapeDtypeStruct((TOKENS, D_MODEL), F32),
        compiler_params=pltpu.CompilerParams(dimension_semantics=("arbitrary",),
                                             vmem_limit_bytes=VMEM_LIMIT),
        name="mix_out_ffn_ple" if mix is not None else "ffn",
    )(*args)


def _rope(x, cos_t, sin_t):
    width = x.shape[-1]
    first_half = (lax.broadcasted_iota(jnp.int32, x.shape, 1) % ROPE_DIM) < ROPE_DIM // 2
    swapped = jnp.where(first_half, pltpu.roll(x, width - ROPE_DIM // 2, 1),
                        pltpu.roll(x, ROPE_DIM // 2, 1))
    return x * cos_t + swapped * sin_t


def _mix_proj_kernel(h_ref, cos_ref, sin_ref, cost_ref, sint_ref, g_ref, win_ref, cw_ref, wco_ref,
                     qg_ref, kvg_ref, wuqt_ref, wk_ref, wvt_ref,
                     q_ref, k_ref, vt_ref, gc_ref, sg_ref, z_ref):
    groups = [slice(r * VT_BLK, (r + 1) * VT_BLK) for r in range(TM_MIX // VT_BLK)]

    @pl.when(pl.program_id(1) == 0)
    def _():
        z_ref[0:CONV_HALO, :] = jnp.zeros((CONV_HALO, CONV_DIM), F32)

    us = [_rmsnorm(h_ref[g, :], g_ref[...]).astype(BF16) for g in groups]
    projs = [_dot(u, win_ref[...]) for u in us]
    cw = cw_ref[...]
    rope0, half_rope = N_HEADS * NOPE_DIM, ROPE_DIM // 2
    ones_rows = (lax.broadcasted_iota(jnp.int32, (V_ROWS - V_DIM, VT_BLK), 0) == 0).astype(BF16)
    for r, (g, proj) in enumerate(zip(groups, projs)):
        z0 = CONV_HALO + r * VT_BLK
        z = proj[:, OFF_C:OFF_C + CONV_DIM] * proj[:, OFF_V:OFF_V + CONV_DIM]
        z_ref[z0:z0 + VT_BLK, :] = z
        y = (cw[0:1] * z_ref[z0 - 2:z0 - 2 + VT_BLK, :] + cw[1:2] * z_ref[z0 - 1:z0 - 1 + VT_BLK, :]
             + cw[2:3] * z)
        y_conv = _dot((proj[:, OFF_B:OFF_B + CONV_DIM] * y).astype(BF16), wco_ref[...])
        gc_ref[g, :] = (jax.nn.sigmoid(proj[:, OFF_GC:OFF_GC + D_MODEL]) * y_conv).astype(BF16)
        sg_ref[g, :] = jax.nn.sigmoid(proj[:, OFF_GM:OFF_GM + D_MODEL]).astype(BF16)

        qn = _rmsnorm(proj[:, OFF_Q:OFF_Q + Q_LORA], qg_ref[...]).astype(BF16)
        q_t = _dot_nt(wuqt_ref[...], qn) * (ATTN_SCALE * LOG2E)
        kvn = _rmsnorm(proj[:, OFF_KV:OFF_KV + KV_LORA], kvg_ref[...]).astype(BF16)
        k_nope = _dot(kvn, wk_ref[...])
        v_t = _dot_nt(wvt_ref[...], kvn)
        k_rope = _rope(proj[:, OFF_KR:OFF_KR + LANES], cos_ref[g, :], sin_ref[g, :])
        low_half = lax.broadcasted_iota(jnp.int32, k_rope.shape, 1) < ROPE_DIM
        k_rope = jnp.where(low_half, k_rope, 0.0).astype(BF16)
        cos_c, sin_c = cost_ref[:, g], sint_ref[:, g]
        for hd in range(N_HEADS):
            q_ref[hd, 0:NOPE_DIM, g] = q_t[hd * NOPE_DIM:(hd + 1) * NOPE_DIM, :].astype(BF16)
            x1 = q_t[rope0 + hd * ROPE_DIM:rope0 + hd * ROPE_DIM + half_rope, :]
            x2 = q_t[rope0 + hd * ROPE_DIM + half_rope:rope0 + (hd + 1) * ROPE_DIM, :]
            q_ref[hd, NOPE_DIM:NOPE_DIM + half_rope, g] = (x1 * cos_c - x2 * sin_c).astype(BF16)
            q_ref[hd, NOPE_DIM + half_rope:QK_DIM, g] = (x2 * cos_c + x1 * sin_c).astype(BF16)
            q_ref[hd, QK_DIM:QK_PAD, g] = jnp.zeros((QK_PAD - QK_DIM, VT_BLK), BF16)
            k_ref[hd, g, 0:LANES] = k_nope[:, hd * NOPE_DIM:(hd + 1) * NOPE_DIM].astype(BF16)
            k_ref[hd, g, LANES:QK_PAD] = k_rope
            vt_ref[hd, r, 0:V_DIM, :] = v_t[hd * V_DIM:(hd + 1) * V_DIM, :].astype(BF16)
            vt_ref[hd, r, V_DIM:V_ROWS, :] = ones_rows
    z_ref[0:CONV_HALO, :] = z_ref[TM_MIX:TM_MIX + CONV_HALO, :]


def _mix_proj(h, layer, rope, norm, w_in, conv_w, w_conv_out, q_norm, kv_norm, w_uqt, w_k, w_vt):
    n_t = SEQ // TM_MIX
    tok = lambda w: pl.BlockSpec((None, TM_MIX, w), lambda b, i: (b, i, 0))
    tok_t = lambda w: pl.BlockSpec((None, w, TM_MIX), lambda b, i: (b, 0, i))
    return pl.pallas_call(
        _mix_proj_kernel,
        grid=(BATCH, n_t),
        in_specs=[tok(D_MODEL), tok(LANES), tok(LANES), tok_t(ROPE_DIM // 2), tok_t(ROPE_DIM // 2),
                  _const_spec((1, D_MODEL), layer), _const_spec((D_MODEL, IN_COLS_PAD), layer),
                  _const_spec((CONV_K, CONV_DIM), layer), _const_spec((CONV_DIM, D_MODEL), layer),
                  _const_spec((1, Q_LORA), layer), _const_spec((1, KV_LORA), layer),
                  _const_spec((N_HEADS * QK_DIM, Q_LORA), layer),
                  _const_spec((KV_LORA, N_HEADS * NOPE_DIM), layer),
                  _const_spec((N_HEADS * V_DIM, KV_LORA), layer)],
        out_specs=[pl.BlockSpec((None, N_HEADS, QK_PAD, TM_MIX), lambda b, i: (b, 0, 0, i)),
                   pl.BlockSpec((None, N_HEADS, TM_MIX, QK_PAD), lambda b, i: (b, 0, i, 0)),
                   pl.BlockSpec((None, N_HEADS, TM_MIX // VT_BLK, V_ROWS, VT_BLK),
                                lambda b, i: (b, 0, i, 0, 0)),
                   tok(D_MODEL), tok(D_MODEL)],
        out_shape=[jax.ShapeDtypeStruct((BATCH, N_HEADS, QK_PAD, SEQ), BF16),
                   jax.ShapeDtypeStruct((BATCH, N_HEADS, SEQ, QK_PAD), BF16),
                   jax.ShapeDtypeStruct((BATCH, N_HEADS, SEQ // VT_BLK, V_ROWS, VT_BLK), BF16),
                   jax.ShapeDtypeStruct((BATCH, SEQ, D_MODEL), BF16),
                   jax.ShapeDtypeStruct((BATCH, SEQ, D_MODEL), BF16)],
        scratch_shapes=[pltpu.VMEM((TM_MIX + CONV_HALO, CONV_DIM), F32)],
        compiler_params=pltpu.CompilerParams(dimension_semantics=("arbitrary", "arbitrary"),
                                             vmem_limit_bytes=VMEM_LIMIT),
        name="mix_proj",
    )(h, *rope, norm, w_in, conv_w, w_conv_out, q_norm, kv_norm, w_uqt, w_k, w_vt)


def _attn_kernel(q_ref, qnext_ref, k_ref, vt_ref, o_ref, m_ref, acc_ref, s_ref, tmax_ref):
    n_full = pl.program_id(2)
    half = TQ // 2
    blk_mask = (lax.broadcasted_iota(jnp.int32, (half, half), 0) // CHUNK
                <= lax.broadcasted_iota(jnp.int32, (half, half), 1) // CHUNK)

    heads = range(HEAD_GROUP)
    m_ref[...] = jnp.full(m_ref.shape, -1e30, F32)
    acc_ref[...] = jnp.zeros(acc_ref.shape, F32)

    def scores(hd, j, queries=q_ref):
        k0 = pl.multiple_of(j * TK, TK)
        s = _dot(k_ref[hd, pl.ds(k0, TK), :], queries[hd])
        s_ref[hd] = s
        tmax_ref[hd] = jnp.max(s, axis=0, keepdims=True)

    def softmax(hd, s, tmax, cols=slice(None)):
        m = m_ref[hd, :, cols]
        m_new = jnp.maximum(m, tmax)
        alpha = jnp.exp2(m - m_new)
        m_ref[hd, :, cols] = m_new
        return jnp.exp2((s - m_new).astype(BF16)), alpha

    def weighted_values(hd, j, p, alpha, cols=slice(None)):
        pv = _dot(vt_ref[hd, j * VT_PER_TK], p[0:VT_BLK])
        for c in range(1, p.shape[0] // VT_BLK):
            pv += _dot(vt_ref[hd, j * VT_PER_TK + c], p[c * VT_BLK:(c + 1) * VT_BLK])
        acc_ref[hd, :, cols] = alpha * acc_ref[hd, :, cols] + pv

    @pl.when(n_full == 0)
    def _():
        for hd in heads:
            scores(hd, 0)

    @pl.loop(0, n_full)
    def _(j):
        for hd in heads:
            s, tmax = s_ref[hd], tmax_ref[hd]
            scores(hd, j + 1)
            p, alpha = softmax(hd, s, tmax)
            weighted_values(hd, j, p, alpha)

    lo, hi = slice(0, half), slice(half, TQ)

    def diagonal_tile(prefetch_next):
        for hd in heads:
            s_lo = jnp.where(blk_mask, s_ref[hd, lo, lo], -1e30)
            s_hi = jnp.concatenate([s_ref[hd, lo, hi], jnp.where(blk_mask, s_ref[hd, hi, hi], -1e30)],
                                   axis=0)
            if prefetch_next:
                scores(hd, 0, qnext_ref)
            p, alpha = softmax(hd, s_lo, jnp.max(s_lo, axis=0, keepdims=True), lo)
            weighted_values(hd, n_full, p, alpha, lo)
            p, alpha = softmax(hd, s_hi, jnp.max(s_hi, axis=0, keepdims=True), hi)
            weighted_values(hd, n_full, p, alpha, hi)

    last = pl.num_programs(2) - 1
    pl.when(n_full < last)(lambda: diagonal_tile(True))
    pl.when(n_full == last)(lambda: diagonal_tile(False))
    for hd in heads:
        out = acc_ref[hd, 0:V_DIM, :] / acc_ref[hd, V_DIM:V_DIM + 1, :]
        o_ref[:, hd * V_DIM:(hd + 1) * V_DIM] = out.T.astype(BF16)


def _attention(q, k, vt):
    assert TQ == TK and (TQ // 2) % VT_BLK == 0 and (TQ // 2) % CHUNK == 0
    return pl.pallas_call(
        _attn_kernel,
        grid=(BATCH, N_HEADS // HEAD_GROUP, SEQ // TQ),
        in_specs=[pl.BlockSpec((None, HEAD_GROUP, QK_PAD, TQ), lambda b, g, i: (b, g, 0, i)),
                  pl.BlockSpec((None, HEAD_GROUP, QK_PAD, TQ),
                               lambda b, g, i: (b, g, 0, jnp.minimum(i + 1, SEQ // TQ - 1))),
                  pl.BlockSpec((None, HEAD_GROUP, SEQ, QK_PAD), lambda b, g, i: (b, g, 0, 0)),
                  pl.BlockSpec((None, HEAD_GROUP, SEQ // VT_BLK, V_ROWS, VT_BLK),
                               lambda b, g, i: (b, g, 0, 0, 0))],
        out_specs=pl.BlockSpec((None, TQ, HEAD_GROUP * V_DIM), lambda b, g, i: (b, i, g)),
        out_shape=jax.ShapeDtypeStruct((BATCH, SEQ, N_HEADS * V_DIM), BF16),
        scratch_shapes=[pltpu.VMEM((HEAD_GROUP, 1, TQ), F32),
                        pltpu.VMEM((HEAD_GROUP, V_ROWS, TQ), F32),
                        pltpu.VMEM((HEAD_GROUP, TK, TQ), F32), pltpu.VMEM((HEAD_GROUP, 1, TQ), F32)],
        compiler_params=pltpu.CompilerParams(dimension_semantics=("arbitrary",) * 3,
                                             vmem_limit_bytes=VMEM_LIMIT),
        name="attention",
    )(q, q, k, vt)


def _rope_tables(positions):
    inv_freq = ROPE_THETA ** (-jnp.arange(0, ROPE_DIM, 2, dtype=F32) / ROPE_DIM)
    ang = positions.astype(F32)[..., None] * inv_freq
    cos, sin = lax.optimization_barrier((jnp.cos(ang), jnp.sin(ang)))
    return (jnp.concatenate([cos, cos, cos, cos], axis=-1),
            jnp.concatenate([-sin, sin, -sin, sin], axis=-1),
            jnp.swapaxes(cos, 1, 2), jnp.swapaxes(sin, 1, 2))


def kernel(x, p, positions, ffn1_norm, ffn1_w_gu, ffn1_w_down, mix_norm, w_in, conv_w, w_conv_out,
           q_norm, kv_norm, w_uq, w_ukv, w_mla_out, w_o, ffn2_norm, ffn2_w_gu, ffn2_w_down,
           ple_norm, w_ple_gate, w_ple_proj, final_norm):
    assert x.shape == (BATCH, SEQ, D_MODEL) and p.shape == (DEPTH, BATCH, SEQ, PLE_DIM)
    bf = lambda w: w.astype(BF16)
    row = lambda g: g.reshape(DEPTH, 1, -1)

    k_rot = w_in[:, :, OFF_KR:OFF_KR + ROPE_DIM]
    w_in_r = bf(jnp.concatenate([w_in[:, :, :OFF_KR], k_rot, k_rot, w_in[:, :, OFF_KR + ROPE_DIM:]], axis=-1))
    w_uq_h = w_uq.reshape(DEPTH, Q_LORA, N_HEADS, QK_DIM)
    w_uq_t = bf(jnp.swapaxes(jnp.concatenate([w_uq_h[..., :NOPE_DIM].reshape(DEPTH, Q_LORA, -1),
                                              w_uq_h[..., NOPE_DIM:].reshape(DEPTH, Q_LORA, -1)],
                                             axis=-1), 1, 2))
    w_ukv_h = w_ukv.reshape(DEPTH, KV_LORA, N_HEADS, NOPE_DIM + V_DIM)
    w_k = bf(w_ukv_h[..., :NOPE_DIM].reshape(DEPTH, KV_LORA, -1))
    w_vt = bf(jnp.swapaxes(w_ukv_h[..., NOPE_DIM:].reshape(DEPTH, KV_LORA, -1), 1, 2))
    ffn1_gu, ffn1_dn, ffn2_gu, ffn2_dn = bf(ffn1_w_gu), bf(ffn1_w_down), bf(ffn2_w_gu), bf(ffn2_w_down)
    w_co, w_mo, w_ob, w_pg, w_pp = bf(w_conv_out), bf(w_mla_out), bf(w_o), bf(w_ple_gate), bf(w_ple_proj)
    rope = _rope_tables(positions)
    p2 = p.reshape(DEPTH, TOKENS, PLE_DIM)

    h = x.reshape(TOKENS, D_MODEL)
    for layer in range(DEPTH):
        h = _ffn(h, layer, row(ffn1_norm), ffn1_gu, ffn1_dn)
        q, k, vt, gc, sg = _mix_proj(h.reshape(BATCH, SEQ, D_MODEL), layer, rope, row(mix_norm),
                                     w_in_r, conv_w, w_co, row(q_norm), row(kv_norm), w_uq_t, w_k, w_vt)
        o = _attention(q, k, vt)
        h = _ffn(h, layer, row(ffn2_norm), ffn2_gu, ffn2_dn,
                 mix=(o.reshape(TOKENS, D_MODEL), gc.reshape(TOKENS, D_MODEL),
                      sg.reshape(TOKENS, D_MODEL), w_mo, w_ob),
                 ple=(p2, row(ple_norm), w_pg, w_pp),
                 final_norm=final_norm.reshape(1, D_MODEL) if layer == DEPTH - 1 else None)
    return h.reshape(BATCH, SEQ, D_MODEL)
```

```python
import functools

import jax
import jax.numpy as jnp
from jax import lax
from jax.experimental import pallas as pl
from jax.experimental.pallas import tpu as pltpu

D_MODEL = 1024
BATCH = 8
SEQ = 4096
DEPTH = 4
CHUNK = 64
PLE_DIM = 256
D_FF = 2816
CONV_DIM = 512
CONV_K = 3
N_HEADS = 8
NOPE_DIM = 128
ROPE_DIM = 64
V_DIM = 128
Q_LORA = 384
KV_LORA = 256
ROPE_THETA = 10000.0
EPS = 1e-6
QK_DIM = NOPE_DIM + ROPE_DIM
ATTN_SCALE = QK_DIM ** -0.5
TOKENS = BATCH * SEQ

LANES = 128
QK_PAD = 2 * LANES
OFF_B, OFF_C, OFF_V = 0, CONV_DIM, 2 * CONV_DIM
OFF_Q = 3 * CONV_DIM
OFF_KV = OFF_Q + Q_LORA
OFF_KR = OFF_KV + KV_LORA
OFF_GC = OFF_KR + LANES
OFF_GM = OFF_GC + D_MODEL
IN_COLS_PAD = OFF_GM + D_MODEL

TM_FFN = 512
FFN_GROUP = 256
FF_CHUNKS = ((0, 1536), (1536, D_FF))
TM_MIX = 512
TQ = 512
TK = 512
HEAD_GROUP = 4
VT_BLK = 256
VT_PER_TK = TK // VT_BLK
V_ROWS = V_DIM + 16
LOG2E = 1.4426950408889634
CONV_HALO = 8
VMEM_LIMIT = 56 * 1024 * 1024

F32 = jnp.float32
BF16 = jnp.bfloat16


def _rmsnorm(x, g):
    return x * lax.rsqrt(jnp.mean(x * x, axis=-1, keepdims=True) + EPS) * g


def _dot(a, b):
    return jnp.dot(a, b, preferred_element_type=F32)


def _dot_nt(a, b):
    return lax.dot_general(a, b, (((1,), (1,)), ((), ())), preferred_element_type=F32)


def _const_spec(shape, layer):
    zeros = (0,) * len(shape)
    return pl.BlockSpec((None,) + tuple(shape), lambda *_: (layer,) + zeros,
                        pipeline_mode=pl.Buffered(1))


def _ffn_kernel(*refs, with_mix, with_ple, with_final):
    refs = list(refs)
    o_ref = refs.pop()
    x_ref = refs.pop(0)
    groups = [slice(r * FFN_GROUP, (r + 1) * FFN_GROUP) for r in range(TM_FFN // FFN_GROUP)]
    xs = [x_ref[g, :] for g in groups]
    if with_mix:
        attn_ref, gc_ref, sg_ref, wmo_ref, wo_ref = refs[:5]
        del refs[:5]
        y_mlas = [_dot(attn_ref[g, :], wmo_ref[...]) for g in groups]
        merged = [(gc_ref[g, :] + sg_ref[g, :] * y).astype(BF16) for g, y in zip(groups, y_mlas)]
        xs = [x + _dot(m, wo_ref[...]) for x, m in zip(xs, merged)]
    g_ref, wgu_ref, wd_ref = refs[:3]
    del refs[:3]
    xns = [_rmsnorm(x, g_ref[...]).astype(BF16) for x in xs]
    ys = None
    for c0, c1 in (FF_CHUNKS if with_mix else ((0, D_FF),)):
        gs = [_dot(xn, wgu_ref[:, c0:c1]) for xn in xns]
        us = [_dot(xn, wgu_ref[:, D_FF + c0:D_FF + c1]) for xn in xns]
        acts = [(g * jax.nn.sigmoid(g) * u).astype(BF16) for g, u in zip(gs, us)]
        parts = [_dot(a, wd_ref[c0:c1, :]) for a in acts]
        ys = parts if ys is None else [y + part for y, part in zip(ys, parts)]
    hs = [x + 0.5 * y for x, y in zip(xs, ys)]
    if with_ple:
        p_ref, pg_ref, wpg_ref, wpp_ref = refs[:4]
        del refs[:4]
        hns = [_rmsnorm(h, pg_ref[...]).astype(BF16) for h in hs]
        gates = [jax.nn.sigmoid(_dot(hn, wpg_ref[...])) for hn in hns]
        hs = [h + gate * _dot(p_ref[g, :].astype(BF16), wpp_ref[...])
              for h, gate, g in zip(hs, gates, groups)]
    if with_final:
        hs = [_rmsnorm(h, refs[0][...]) for h in hs]
    for g, h in zip(groups, hs):
        o_ref[g, :] = h


def _ffn(h, layer, norm, w_gu, w_down, mix=None, ple=None, final_norm=None):
    tile = pl.BlockSpec((TM_FFN, D_MODEL), lambda i: (i, 0))
    in_specs, args = [tile], [h]
    if mix is not None:
        attn, gc, sg, w_mla_out, w_o = mix
        in_specs += [tile, tile, tile,
                     _const_spec((D_MODEL, D_MODEL), layer), _const_spec((D_MODEL, D_MODEL), layer)]
        args += [attn, gc, sg, w_mla_out, w_o]
    in_specs += [_const_spec((1, D_MODEL), layer),
                 _const_spec((D_MODEL, 2 * D_FF), layer), _const_spec((D_FF, D_MODEL), layer)]
    args += [norm, w_gu, w_down]
    if ple is not None:
        p, ple_norm, w_pg, w_pp = ple
        in_specs += [pl.BlockSpec((None, TM_FFN, PLE_DIM), lambda i: (layer, i, 0)),
                     _const_spec((1, D_MODEL), layer),
                     _const_spec((D_MODEL, D_MODEL), layer), _const_spec((PLE_DIM, D_MODEL), layer)]
        args += [p, ple_norm, w_pg, w_pp]
    if final_norm is not None:
        in_specs.append(pl.BlockSpec((1, D_MODEL), lambda i: (0, 0)))
        args.append(final_norm)
    return pl.pallas_call(
        functools.partial(_ffn_kernel, with_mix=mix is not None, with_ple=ple is not None,
                          with_final=final_norm is not None),
        grid=(TOKENS // TM_FFN,),
        in_specs=in_specs,
        out_specs=tile,
        out_shape=jax.ShapeDtypeStruct((TOKENS, D_MODEL), F32),
        compiler_params=pltpu.CompilerParams(dimension_semantics=("arbitrary",),
                                             vmem_limit_bytes=VMEM_LIMIT),
        name="mix_out_ffn_ple" if mix is not None else "ffn",
    )(*args)


def _rope_t(x1, x2, cos_c, sin_c):
    return x1 * cos_c - x2 * sin_c, x2 * cos_c + x1 * sin_c


def _mix_proj_kernel(h_ref, cost_ref, sint_ref, g_ref, win_ref, cw_ref, wco_ref,
                     qg_ref, kvg_ref, wuqt_ref, wk_ref, wvt_ref,
                     q_ref, k_ref, vt_ref, gc_ref, sg_ref, z_ref):
    groups = [slice(r * VT_BLK, (r + 1) * VT_BLK) for r in range(TM_MIX // VT_BLK)]

    @pl.when(pl.program_id(1) == 0)
    def _():
        z_ref[0:CONV_HALO, :] = jnp.zeros((CONV_HALO, CONV_DIM), F32)

    us = [_rmsnorm(h_ref[g, :], g_ref[...]).astype(BF16) for g in groups]
    projs = [_dot(u, win_ref[...]) for u in us]
    cw = cw_ref[...]
    rope0, half_rope = N_HEADS * NOPE_DIM, ROPE_DIM // 2
    ones_rows = (lax.broadcasted_iota(jnp.int32, (V_ROWS - V_DIM, VT_BLK), 0) == 0).astype(BF16)
    for r, (g, proj) in enumerate(zip(groups, projs)):
        z0 = CONV_HALO + r * VT_BLK
        z = proj[:, OFF_C:OFF_C + CONV_DIM] * proj[:, OFF_V:OFF_V + CONV_DIM]
        z_ref[z0:z0 + VT_BLK, :] = z
        y = (cw[0:1] * z_ref[z0 - 2:z0 - 2 + VT_BLK, :] + cw[1:2] * z_ref[z0 - 1:z0 - 1 + VT_BLK, :]
             + cw[2:3] * z)
        y_conv = _dot((proj[:, OFF_B:OFF_B + CONV_DIM] * y).astype(BF16), wco_ref[...])
        gc_ref[g, :] = (jax.nn.sigmoid(proj[:, OFF_GC:OFF_GC + D_MODEL]) * y_conv).astype(BF16)
        sg_ref[g, :] = jax.nn.sigmoid(proj[:, OFF_GM:OFF_GM + D_MODEL]).astype(BF16)

        qn = _rmsnorm(proj[:, OFF_Q:OFF_Q + Q_LORA], qg_ref[...]).astype(BF16)
        q_t = _dot_nt(wuqt_ref[...], qn) * (ATTN_SCALE * LOG2E)
        kvn = _rmsnorm(proj[:, OFF_KV:OFF_KV + KV_LORA], kvg_ref[...]).astype(BF16)
        k_nope = _dot(kvn, wk_ref[...])
        v_t = _dot_nt(wvt_ref[...], kvn)
        cos_c, sin_c = cost_ref[:, g], sint_ref[:, g]
        kr_t = proj[:, OFF_KR:OFF_KR + LANES].T
        r1, r2 = _rope_t(kr_t[0:half_rope], kr_t[half_rope:ROPE_DIM], cos_c, sin_c)
        pad = jnp.zeros((LANES - ROPE_DIM, VT_BLK), F32)
        k_rope = jnp.concatenate([r1, r2, pad], axis=0).T.astype(BF16)
        for hd in range(N_HEADS):
            q_ref[hd, 0:NOPE_DIM, g] = q_t[hd * NOPE_DIM:(hd + 1) * NOPE_DIM, :].astype(BF16)
            r1, r2 = _rope_t(q_t[rope0 + hd * ROPE_DIM:rope0 + hd * ROPE_DIM + half_rope, :],
                             q_t[rope0 + hd * ROPE_DIM + half_rope:rope0 + (hd + 1) * ROPE_DIM, :],
                             cos_c, sin_c)
            q_ref[hd, NOPE_DIM:NOPE_DIM + half_rope, g] = r1.astype(BF16)
            q_ref[hd, NOPE_DIM + half_rope:QK_DIM, g] = r2.astype(BF16)
            q_ref[hd, QK_DIM:QK_PAD, g] = jnp.zeros((QK_PAD - QK_DIM, VT_BLK), BF16)
            k_ref[hd, g, 0:LANES] = k_nope[:, hd * NOPE_DIM:(hd + 1) * NOPE_DIM].astype(BF16)
            k_ref[hd, g, LANES:QK_PAD] = k_rope
            vt_ref[hd, r, 0:V_DIM, :] = v_t[hd * V_DIM:(hd + 1) * V_DIM, :].astype(BF16)
            vt_ref[hd, r, V_DIM:V_ROWS, :] = ones_rows
    z_ref[0:CONV_HALO, :] = z_ref[TM_MIX:TM_MIX + CONV_HALO, :]


def _mix_proj(h, layer, rope, norm, w_in, conv_w, w_conv_out, q_norm, kv_norm, w_uqt, w_k, w_vt):
    n_t = SEQ // TM_MIX
    tok = lambda w: pl.BlockSpec((None, TM_MIX, w), lambda b, i: (b, i, 0))
    tok_t = lambda w: pl.BlockSpec((None, w, TM_MIX), lambda b, i: (b, 0, i))
    return pl.pallas_call(
        _mix_proj_kernel,
        grid=(BATCH, n_t),
        in_specs=[tok(D_MODEL), tok_t(ROPE_DIM // 2), tok_t(ROPE_DIM // 2),
                  _const_spec((1, D_MODEL), layer), _const_spec((D_MODEL, IN_COLS_PAD), layer),
                  _const_spec((CONV_K, CONV_DIM), layer), _const_spec((CONV_DIM, D_MODEL), layer),
                  _const_spec((1, Q_LORA), layer), _const_spec((1, KV_LORA), layer),
                  _const_spec((N_HEADS * QK_DIM, Q_LORA), layer),
                  _const_spec((KV_LORA, N_HEADS * NOPE_DIM), layer),
                  _const_spec((N_HEADS * V_DIM, KV_LORA), layer)],
        out_specs=[pl.BlockSpec((None, N_HEADS, QK_PAD, TM_MIX), lambda b, i: (b, 0, 0, i)),
                   pl.BlockSpec((None, N_HEADS, TM_MIX, QK_PAD), lambda b, i: (b, 0, i, 0)),
                   pl.BlockSpec((None, N_HEADS, TM_MIX // VT_BLK, V_ROWS, VT_BLK),
                                lambda b, i: (b, 0, i, 0, 0)),
                   tok(D_MODEL), tok(D_MODEL)],
        out_shape=[jax.ShapeDtypeStruct((BATCH, N_HEADS, QK_PAD, SEQ), BF16),
                   jax.ShapeDtypeStruct((BATCH, N_HEADS, SEQ, QK_PAD), BF16),
                   jax.ShapeDtypeStruct((BATCH, N_HEADS, SEQ // VT_BLK, V_ROWS, VT_BLK), BF16),
                   jax.ShapeDtypeStruct((BATCH, SEQ, D_MODEL), BF16),
                   jax.ShapeDtypeStruct((BATCH, SEQ, D_MODEL), BF16)],
        scratch_shapes=[pltpu.VMEM((TM_MIX + CONV_HALO, CONV_DIM), F32)],
        compiler_params=pltpu.CompilerParams(dimension_semantics=("arbitrary", "arbitrary"),
                                             vmem_limit_bytes=VMEM_LIMIT),
        name="mix_proj",
    )(h, *rope, norm, w_in, conv_w, w_conv_out, q_norm, kv_norm, w_uqt, w_k, w_vt)


def _attn_kernel(q_ref, qnext_ref, k_ref, vt_ref, o_ref, m_ref, acc_ref, s_ref, tmax_ref):
    n_full = pl.program_id(2)
    half = TQ // 2
    blk_mask = (lax.broadcasted_iota(jnp.int32, (half, half), 0) // CHUNK
                <= lax.broadcasted_iota(jnp.int32, (half, half), 1) // CHUNK)

    heads = range(HEAD_GROUP)
    m_ref[...] = jnp.full(m_ref.shape, -1e30, F32)
    acc_ref[...] = jnp.zeros(acc_ref.shape, F32)

    def scores(hd, j, queries=q_ref):
        k0 = pl.multiple_of(j * TK, TK)
        s = _dot(k_ref[hd, pl.ds(k0, TK), :], queries[hd])
        s_ref[hd] = s
        tmax_ref[hd] = jnp.max(s, axis=0, keepdims=True)

    def softmax(hd, s, tmax, cols=slice(None)):
        m = m_ref[hd, :, cols]
        m_new = jnp.maximum(m, tmax)
        alpha = jnp.exp2(m - m_new)
        m_ref[hd, :, cols] = m_new
        return jnp.exp2((s - m_new).astype(BF16)), alpha

    def weighted_values(hd, j, p, alpha, cols=slice(None)):
        pv = _dot(vt_ref[hd, j * VT_PER_TK], p[0:VT_BLK])
        for c in range(1, p.shape[0] // VT_BLK):
            pv += _dot(vt_ref[hd, j * VT_PER_TK + c], p[c * VT_BLK:(c + 1) * VT_BLK])
        acc_ref[hd, :, cols] = alpha * acc_ref[hd, :, cols] + pv

    @pl.when(n_full == 0)
    def _():
        for hd in heads:
            scores(hd, 0)

    @pl.loop(0, n_full)
    def _(j):
        for hd in heads:
            s, tmax = s_ref[hd], tmax_ref[hd]
            scores(hd, j + 1)
            p, alpha = softmax(hd, s, tmax)
            weighted_values(hd, j, p, alpha)

    lo, hi = slice(0, half), slice(half, TQ)

    def diagonal_tile(prefetch_next):
        for hd in heads:
            s_lo = jnp.where(blk_mask, s_ref[hd, lo, lo], -1e30)
            s_hi = jnp.concatenate([s_ref[hd, lo, hi], jnp.where(blk_mask, s_ref[hd, hi, hi], -1e30)],
                                   axis=0)
            if prefetch_next:
                scores(hd, 0, qnext_ref)
            p, alpha = softmax(hd, s_lo, jnp.max(s_lo, axis=0, keepdims=True), lo)
            weighted_values(hd, n_full, p, alpha, lo)
            p, alpha = softmax(hd, s_hi, jnp.max(s_hi, axis=0, keepdims=True), hi)
            weighted_values(hd, n_full, p, alpha, hi)

    last = pl.num_programs(2) - 1
    pl.when(n_full < last)(lambda: diagonal_tile(True))
    pl.when(n_full == last)(lambda: diagonal_tile(False))
    for hd in heads:
        out = acc_ref[hd, 0:V_DIM, :] / acc_ref[hd, V_DIM:V_DIM + 1, :]
        o_ref[:, hd * V_DIM:(hd + 1) * V_DIM] = out.T.astype(BF16)


def _attention(q, k, vt):
    assert TQ == TK and (TQ // 2) % VT_BLK == 0 and (TQ // 2) % CHUNK == 0
    return pl.pallas_call(
        _attn_kernel,
        grid=(BATCH, N_HEADS // HEAD_GROUP, SEQ // TQ),
        in_specs=[pl.BlockSpec((None, HEAD_GROUP, QK_PAD, TQ), lambda b, g, i: (b, g, 0, i)),
                  pl.BlockSpec((None, HEAD_GROUP, QK_PAD, TQ),
                               lambda b, g, i: (b, g, 0, jnp.minimum(i + 1, SEQ // TQ - 1))),
                  pl.BlockSpec((None, HEAD_GROUP, SEQ, QK_PAD), lambda b, g, i: (b, g, 0, 0)),
                  pl.BlockSpec((None, HEAD_GROUP, SEQ // VT_BLK, V_ROWS, VT_BLK),
                               lambda b, g, i: (b, g, 0, 0, 0))],
        out_specs=pl.BlockSpec((None, TQ, HEAD_GROUP * V_DIM), lambda b, g, i: (b, i, g)),
        out_shape=jax.ShapeDtypeStruct((BATCH, SEQ, N_HEADS * V_DIM), BF16),
        scratch_shapes=[pltpu.VMEM((HEAD_GROUP, 1, TQ), F32),
                        pltpu.VMEM((HEAD_GROUP, V_ROWS, TQ), F32),
                        pltpu.VMEM((HEAD_GROUP, TK, TQ), F32), pltpu.VMEM((HEAD_GROUP, 1, TQ), F32)],
        compiler_params=pltpu.CompilerParams(dimension_semantics=("arbitrary",) * 3,
                                             vmem_limit_bytes=VMEM_LIMIT),
        name="attention",
    )(q, q, k, vt)


def _rope_tables(positions):
    inv_freq = ROPE_THETA ** (-jnp.arange(0, ROPE_DIM, 2, dtype=F32) / ROPE_DIM)
    ang = positions.astype(F32)[:, None, :] * inv_freq[None, :, None]
    return jnp.cos(ang), jnp.sin(ang)


def kernel(x, p, positions, ffn1_norm, ffn1_w_gu, ffn1_w_down, mix_norm, w_in, conv_w, w_conv_out,
           q_norm, kv_norm, w_uq, w_ukv, w_mla_out, w_o, ffn2_norm, ffn2_w_gu, ffn2_w_down,
           ple_norm, w_ple_gate, w_ple_proj, final_norm):
    assert x.shape == (BATCH, SEQ, D_MODEL) and p.shape == (DEPTH, BATCH, SEQ, PLE_DIM)
    bf = lambda w: w.astype(BF16)
    row = lambda g: g.reshape(DEPTH, 1, -1)

    k_rot = w_in[:, :, OFF_KR:OFF_KR + ROPE_DIM]
    w_in_r = bf(jnp.concatenate([w_in[:, :, :OFF_KR], k_rot, k_rot, w_in[:, :, OFF_KR + ROPE_DIM:]], axis=-1))
    w_uq_h = w_uq.reshape(DEPTH, Q_LORA, N_HEADS, QK_DIM)
    w_uq_t = bf(jnp.swapaxes(jnp.concatenate([w_uq_h[..., :NOPE_DIM].reshape(DEPTH, Q_LORA, -1),
                                              w_uq_h[..., NOPE_DIM:].reshape(DEPTH, Q_LORA, -1)],
                                             axis=-1), 1, 2))
    w_ukv_h = w_ukv.reshape(DEPTH, KV_LORA, N_HEADS, NOPE_DIM + V_DIM)
    w_k = bf(w_ukv_h[..., :NOPE_DIM].reshape(DEPTH, KV_LORA, -1))
    w_vt = bf(jnp.swapaxes(w_ukv_h[..., NOPE_DIM:].reshape(DEPTH, KV_LORA, -1), 1, 2))
    ffn1_gu, ffn1_dn, ffn2_gu, ffn2_dn = bf(ffn1_w_gu), bf(ffn1_w_down), bf(ffn2_w_gu), bf(ffn2_w_down)
    w_co, w_mo, w_ob, w_pg, w_pp = bf(w_conv_out), bf(w_mla_out), bf(w_o), bf(w_ple_gate), bf(w_ple_proj)
    rope = _rope_tables(positions)
    p2 = p.reshape(DEPTH, TOKENS, PLE_DIM)

    h = x.reshape(TOKENS, D_MODEL)
    for layer in range(DEPTH):
        h = _ffn(h, layer, row(ffn1_norm), ffn1_gu, ffn1_dn)
        q, k, vt, gc, sg = _mix_proj(h.reshape(BATCH, SEQ, D_MODEL), layer, rope, row(mix_norm),
                                     w_in_r, conv_w, w_co, row(q_norm), row(kv_norm), w_uq_t, w_k, w_vt)
        o = _attention(q, k, vt)
        h = _ffn(h, layer, row(ffn2_norm), ffn2_gu, ffn2_dn,
                 mix=(o.reshape(TOKENS, D_MODEL), gc.reshape(TOKENS, D_MODEL),
                      sg.reshape(TOKENS, D_MODEL), w_mo, w_ob),
                 ple=(p2, row(ple_norm), w_pg, w_pp),
                 final_norm=final_norm.reshape(1, D_MODEL) if layer == DEPTH - 1 else None)
    return h.reshape(BATCH, SEQ, D_MODEL)
```

```python
import functools

import jax
import jax.numpy as jnp
from jax import lax
from jax.experimental import pallas as pl
from jax.experimental.pallas import tpu as pltpu

D_MODEL = 1024
BATCH = 8
SEQ = 4096
DEPTH = 4
CHUNK = 64
PLE_DIM = 256
D_FF = 2816
CONV_DIM = 512
CONV_K = 3
N_HEADS = 8
NOPE_DIM = 128
ROPE_DIM = 64
V_DIM = 128
Q_LORA = 384
KV_LORA = 256
ROPE_THETA = 10000.0
EPS = 1e-6
QK_DIM = NOPE_DIM + ROPE_DIM
ATTN_SCALE = QK_DIM ** -0.5
TOKENS = BATCH * SEQ

LANES = 128
QK_PAD = 2 * LANES
OFF_B, OFF_C, OFF_V = 0, CONV_DIM, 2 * CONV_DIM
OFF_Q = 3 * CONV_DIM
OFF_KV = OFF_Q + Q_LORA
OFF_KR = OFF_KV + KV_LORA
OFF_GC = OFF_KR + LANES
OFF_GM = OFF_GC + D_MODEL
IN_COLS_PAD = OFF_GM + D_MODEL

TM_FFN = 512
FFN_GROUP = 256
FF_CHUNKS = ((0, 1536), (1536, D_FF))
TM_MIX = 512
TQ = 512
TK = 512
HEAD_GROUP = 4
VT_BLK = 256
VT_PER_TK = TK // VT_BLK
V_ROWS = V_DIM + 16
LOG2E = 1.4426950408889634
CONV_HALO = 8
VMEM_LIMIT = 56 * 1024 * 1024

F32 = jnp.float32
BF16 = jnp.bfloat16


def _rmsnorm(x, g):
    return x * lax.rsqrt(jnp.mean(x * x, axis=-1, keepdims=True) + EPS) * g


def _dot(a, b):
    return jnp.dot(a, b, preferred_element_type=F32)


def _dot_nt(a, b):
    return lax.dot_general(a, b, (((1,), (1,)), ((), ())), preferred_element_type=F32)


def _const_spec(shape, layer):
    zeros = (0,) * len(shape)
    return pl.BlockSpec((None,) + tuple(shape), lambda *_: (layer,) + zeros,
                        pipeline_mode=pl.Buffered(1))


def _ffn_kernel(*refs, with_mix, with_ple, with_final):
    refs = list(refs)
    o_ref = refs.pop()
    x_ref = refs.pop(0)
    groups = [slice(r * FFN_GROUP, (r + 1) * FFN_GROUP) for r in range(TM_FFN // FFN_GROUP)]
    xs = [x_ref[g, :] for g in groups]
    if with_mix:
        attn_ref, gc_ref, sg_ref, wmo_ref, wo_ref = refs[:5]
        del refs[:5]
        y_mlas = [_dot(attn_ref[g, :], wmo_ref[...]) for g in groups]
        merged = [(gc_ref[g, :] + sg_ref[g, :] * y).astype(BF16) for g, y in zip(groups, y_mlas)]
        xs = [x + _dot(m, wo_ref[...]) for x, m in zip(xs, merged)]
    g_ref, wgu_ref, wd_ref = refs[:3]
    del refs[:3]
    xns = [_rmsnorm(x, g_ref[...]).astype(BF16) for x in xs]
    ys = None
    for c0, c1 in (FF_CHUNKS if with_mix else ((0, D_FF),)):
        gs = [_dot(xn, wgu_ref[:, c0:c1]) for xn in xns]
        us = [_dot(xn, wgu_ref[:, D_FF + c0:D_FF + c1]) for xn in xns]
        acts = [(g * jax.nn.sigmoid(g) * u).astype(BF16) for g, u in zip(gs, us)]
        parts = [_dot(a, wd_ref[c0:c1, :]) for a in acts]
        ys = parts if ys is None else [y + part for y, part in zip(ys, parts)]
    hs = [x + 0.5 * y for x, y in zip(xs, ys)]
    if with_ple:
        p_ref, pg_ref, wpg_ref, wpp_ref = refs[:4]
        del refs[:4]
        hns = [_rmsnorm(h, pg_ref[...]).astype(BF16) for h in hs]
        gates = [jax.nn.sigmoid(_dot(hn, wpg_ref[...])) for hn in hns]
        hs = [h + gate * _dot(p_ref[g, :].astype(BF16), wpp_ref[...])
              for h, gate, g in zip(hs, gates, groups)]
    if with_final:
        hs = [_rmsnorm(h, refs[0][...]) for h in hs]
    for g, h in zip(groups, hs):
        o_ref[g, :] = h


def _ffn(h, layer, norm, w_gu, w_down, mix=None, ple=None, final_norm=None):
    tile = pl.BlockSpec((TM_FFN, D_MODEL), lambda i: (i, 0))
    in_specs, args = [tile], [h]
    if mix is not None:
        attn, gc, sg, w_mla_out, w_o = mix
        in_specs += [tile, tile, tile,
                     _const_spec((D_MODEL, D_MODEL), layer), _const_spec((D_MODEL, D_MODEL), layer)]
        args += [attn, gc, sg, w_mla_out, w_o]
    in_specs += [_const_spec((1, D_MODEL), layer),
                 _const_spec((D_MODEL, 2 * D_FF), layer), _const_spec((D_FF, D_MODEL), layer)]
    args += [norm, w_gu, w_down]
    if ple is not None:
        p, ple_norm, w_pg, w_pp = ple
        in_specs += [pl.BlockSpec((None, TM_FFN, PLE_DIM), lambda i: (layer, i, 0)),
                     _const_spec((1, D_MODEL), layer),
                     _const_spec((D_MODEL, D_MODEL), layer), _const_spec((PLE_DIM, D_MODEL), layer)]
        args += [p, ple_norm, w_pg, w_pp]
    if final_norm is not None:
        in_specs.append(pl.BlockSpec((1, D_MODEL), lambda i: (0, 0)))
        args.append(final_norm)
    return pl.pallas_call(
        functools.partial(_ffn_kernel, with_mix=mix is not None, with_ple=ple is not None,
                          with_final=final_norm is not None),
        grid=(TOKENS // TM_FFN,),
        in_specs=in_specs,
        out_specs=tile,
        out_shape=jax.ShapeDtypeStruct((TOKENS, D_MODEL), F32),
        compiler_params=pltpu.CompilerParams(dimension_semantics=("arbitrary",),
                                             vmem_limit_bytes=VMEM_LIMIT),
        name="mix_out_ffn_ple" if mix is not None else "ffn",
    )(*args)


def _rope_t(x1, x2, cos_c, sin_c):
    return x1 * cos_c - x2 * sin_c, x2 * cos_c + x1 * sin_c


def _mix_proj_kernel(h_ref, cost_ref, sint_ref, g_ref, win_ref, cw_ref, wco_ref,
                     qg_ref, kvg_ref, wuqt_ref, wukv_ref,
                     q_ref, k_ref, vt_ref, gc_ref, sg_ref, z_ref):
    groups = [slice(r * VT_BLK, (r + 1) * VT_BLK) for r in range(TM_MIX // VT_BLK)]

    @pl.when(pl.program_id(1) == 0)
    def _():
        z_ref[0:CONV_HALO, :] = jnp.zeros((CONV_HALO, CONV_DIM), F32)

    us = [_rmsnorm(h_ref[g, :], g_ref[...]).astype(BF16) for g in groups]
    projs = [_dot(u, win_ref[...]) for u in us]
    cw = cw_ref[...]
    rope0, half_rope = N_HEADS * NOPE_DIM, ROPE_DIM // 2
    ones_rows = (lax.broadcasted_iota(jnp.int32, (V_ROWS - V_DIM, VT_BLK), 0) == 0).astype(BF16)
    for r, (g, proj) in enumerate(zip(groups, projs)):
        z0 = CONV_HALO + r * VT_BLK
        z = proj[:, OFF_C:OFF_C + CONV_DIM] * proj[:, OFF_V:OFF_V + CONV_DIM]
        z_ref[z0:z0 + VT_BLK, :] = z
        y = (cw[0:1] * z_ref[z0 - 2:z0 - 2 + VT_BLK, :] + cw[1:2] * z_ref[z0 - 1:z0 - 1 + VT_BLK, :]
             + cw[2:3] * z)
        y_conv = _dot((proj[:, OFF_B:OFF_B + CONV_DIM] * y).astype(BF16), wco_ref[...])
        gc_ref[g, :] = (jax.nn.sigmoid(proj[:, OFF_GC:OFF_GC + D_MODEL]) * y_conv).astype(BF16)
        sg_ref[g, :] = jax.nn.sigmoid(proj[:, OFF_GM:OFF_GM + D_MODEL]).astype(BF16)

        qn = _rmsnorm(proj[:, OFF_Q:OFF_Q + Q_LORA], qg_ref[...]).astype(BF16)
        q_t = _dot_nt(wuqt_ref[...], qn) * (ATTN_SCALE * LOG2E)
        kvn = _rmsnorm(proj[:, OFF_KV:OFF_KV + KV_LORA], kvg_ref[...]).astype(BF16)
        kv = _dot(kvn, wukv_ref[...])
        cos_c, sin_c = cost_ref[:, g], sint_ref[:, g]
        kr_t = proj[:, OFF_KR:OFF_KR + LANES].T
        r1, r2 = _rope_t(kr_t[0:half_rope], kr_t[half_rope:ROPE_DIM], cos_c, sin_c)
        pad = jnp.zeros((LANES - ROPE_DIM, VT_BLK), F32)
        k_rope = jnp.concatenate([r1, r2, pad], axis=0).T.astype(BF16)
        for hd in range(N_HEADS):
            q_ref[hd, 0:NOPE_DIM, g] = q_t[hd * NOPE_DIM:(hd + 1) * NOPE_DIM, :].astype(BF16)
            r1, r2 = _rope_t(q_t[rope0 + hd * ROPE_DIM:rope0 + hd * ROPE_DIM + half_rope, :],
                             q_t[rope0 + hd * ROPE_DIM + half_rope:rope0 + (hd + 1) * ROPE_DIM, :],
                             cos_c, sin_c)
            q_ref[hd, NOPE_DIM:NOPE_DIM + half_rope, g] = r1.astype(BF16)
            q_ref[hd, NOPE_DIM + half_rope:QK_DIM, g] = r2.astype(BF16)
            q_ref[hd, QK_DIM:QK_PAD, g] = jnp.zeros((QK_PAD - QK_DIM, VT_BLK), BF16)
            kv0 = hd * (NOPE_DIM + V_DIM)
            k_ref[hd, g, 0:LANES] = kv[:, kv0:kv0 + NOPE_DIM].astype(BF16)
            k_ref[hd, g, LANES:QK_PAD] = k_rope
            vt_ref[hd, r, 0:V_DIM, :] = kv[:, kv0 + NOPE_DIM:kv0 + NOPE_DIM + V_DIM].T.astype(BF16)
            vt_ref[hd, r, V_DIM:V_ROWS, :] = ones_rows
    z_ref[0:CONV_HALO, :] = z_ref[TM_MIX:TM_MIX + CONV_HALO, :]


def _mix_proj(h, layer, rope, norm, w_in, conv_w, w_conv_out, q_norm, kv_norm, w_uqt, w_ukv):
    n_t = SEQ // TM_MIX
    tok = lambda w: pl.BlockSpec((None, TM_MIX, w), lambda b, i: (b, i, 0))
    tok_t = lambda w: pl.BlockSpec((None, w, TM_MIX), lambda b, i: (b, 0, i))
    return pl.pallas_call(
        _mix_proj_kernel,
        grid=(BATCH, n_t),
        in_specs=[tok(D_MODEL), tok_t(ROPE_DIM // 2), tok_t(ROPE_DIM // 2),
                  _const_spec((1, D_MODEL), layer), _const_spec((D_MODEL, IN_COLS_PAD), layer),
                  _const_spec((CONV_K, CONV_DIM), layer), _const_spec((CONV_DIM, D_MODEL), layer),
                  _const_spec((1, Q_LORA), layer), _const_spec((1, KV_LORA), layer),
                  _const_spec((N_HEADS * QK_DIM, Q_LORA), layer),
                  _const_spec((KV_LORA, N_HEADS * (NOPE_DIM + V_DIM)), layer)],
        out_specs=[pl.BlockSpec((None, N_HEADS, QK_PAD, TM_MIX), lambda b, i: (b, 0, 0, i)),
                   pl.BlockSpec((None, N_HEADS, TM_MIX, QK_PAD), lambda b, i: (b, 0, i, 0)),
                   pl.BlockSpec((None, N_HEADS, TM_MIX // VT_BLK, V_ROWS, VT_BLK),
                                lambda b, i: (b, 0, i, 0, 0)),
                   tok(D_MODEL), tok(D_MODEL)],
        out_shape=[jax.ShapeDtypeStruct((BATCH, N_HEADS, QK_PAD, SEQ), BF16),
                   jax.ShapeDtypeStruct((BATCH, N_HEADS, SEQ, QK_PAD), BF16),
                   jax.ShapeDtypeStruct((BATCH, N_HEADS, SEQ // VT_BLK, V_ROWS, VT_BLK), BF16),
                   jax.ShapeDtypeStruct((BATCH, SEQ, D_MODEL), BF16),
                   jax.ShapeDtypeStruct((BATCH, SEQ, D_MODEL), BF16)],
        scratch_shapes=[pltpu.VMEM((TM_MIX + CONV_HALO, CONV_DIM), F32)],
        compiler_params=pltpu.CompilerParams(dimension_semantics=("arbitrary", "arbitrary"),
                                             vmem_limit_bytes=VMEM_LIMIT),
        name="mix_proj",
    )(h, *rope, norm, w_in, conv_w, w_conv_out, q_norm, kv_norm, w_uqt, w_ukv)


def _attn_kernel(q_ref, qnext_ref, k_ref, vt_ref, o_ref, m_ref, acc_ref, s_ref, tmax_ref):
    n_full = pl.program_id(2)
    half = TQ // 2
    blk_mask = (lax.broadcasted_iota(jnp.int32, (half, half), 0) // CHUNK
                <= lax.broadcasted_iota(jnp.int32, (half, half), 1) // CHUNK)

    heads = range(HEAD_GROUP)
    m_ref[...] = jnp.full(m_ref.shape, -1e30, F32)
    acc_ref[...] = jnp.zeros(acc_ref.shape, F32)

    def scores(hd, j, queries=q_ref):
        k0 = pl.multiple_of(j * TK, TK)
        s = _dot(k_ref[hd, pl.ds(k0, TK), :], queries[hd])
        s_ref[hd] = s
        tmax_ref[hd] = jnp.max(s, axis=0, keepdims=True)

    def softmax(hd, s, tmax, cols=slice(None)):
        m = m_ref[hd, :, cols]
        m_new = jnp.maximum(m, tmax)
        alpha = jnp.exp2(m - m_new)
        m_ref[hd, :, cols] = m_new
        return jnp.exp2((s - m_new).astype(BF16)), alpha

    def weighted_values(hd, j, p, alpha, cols=slice(None)):
        pv = _dot(vt_ref[hd, j * VT_PER_TK], p[0:VT_BLK])
        for c in range(1, p.shape[0] // VT_BLK):
            pv += _dot(vt_ref[hd, j * VT_PER_TK + c], p[c * VT_BLK:(c + 1) * VT_BLK])
        acc_ref[hd, :, cols] = alpha * acc_ref[hd, :, cols] + pv

    @pl.when(n_full == 0)
    def _():
        for hd in heads:
            scores(hd, 0)

    @pl.loop(0, n_full)
    def _(j):
        for hd in heads:
            s, tmax = s_ref[hd], tmax_ref[hd]
            scores(hd, j + 1)
            p, alpha = softmax(hd, s, tmax)
            weighted_values(hd, j, p, alpha)

    lo, hi = slice(0, half), slice(half, TQ)

    def diagonal_tile(prefetch_next):
        for hd in heads:
            s_lo = jnp.where(blk_mask, s_ref[hd, lo, lo], -1e30)
            s_hi = jnp.concatenate([s_ref[hd, lo, hi], jnp.where(blk_mask, s_ref[hd, hi, hi], -1e30)],
                                   axis=0)
            if prefetch_next:
                scores(hd, 0, qnext_ref)
            p, alpha = softmax(hd, s_lo, jnp.max(s_lo, axis=0, keepdims=True), lo)
            weighted_values(hd, n_full, p, alpha, lo)
            p, alpha = softmax(hd, s_hi, jnp.max(s_hi, axis=0, keepdims=True), hi)
            weighted_values(hd, n_full, p, alpha, hi)

    last = pl.num_programs(2) - 1
    pl.when(n_full < last)(lambda: diagonal_tile(True))
    pl.when(n_full == last)(lambda: diagonal_tile(False))
    for hd in heads:
        out = acc_ref[hd, 0:V_DIM, :] / acc_ref[hd, V_DIM:V_DIM + 1, :]
        o_ref[:, hd * V_DIM:(hd + 1) * V_DIM] = out.T.astype(BF16)


def _attention(q, k, vt):
    assert TQ == TK and (TQ // 2) % VT_BLK == 0 and (TQ // 2) % CHUNK == 0
    return pl.pallas_call(
        _attn_kernel,
        grid=(BATCH, N_HEADS // HEAD_GROUP, SEQ // TQ),
        in_specs=[pl.BlockSpec((None, HEAD_GROUP, QK_PAD, TQ), lambda b, g, i: (b, g, 0, i)),
                  pl.BlockSpec((None, HEAD_GROUP, QK_PAD, TQ),
                               lambda b, g, i: (b, g, 0, jnp.minimum(i + 1, SEQ // TQ - 1))),
                  pl.BlockSpec((None, HEAD_GROUP, SEQ, QK_PAD), lambda b, g, i: (b, g, 0, 0)),
                  pl.BlockSpec((None, HEAD_GROUP, SEQ // VT_BLK, V_ROWS, VT_BLK),
                               lambda b, g, i: (b, g, 0, 0, 0))],
        out_specs=pl.BlockSpec((None, TQ, HEAD_GROUP * V_DIM), lambda b, g, i: (b, i, g)),
        out_shape=jax.ShapeDtypeStruct((BATCH, SEQ, N_HEADS * V_DIM), BF16),
        scratch_shapes=[pltpu.VMEM((HEAD_GROUP, 1, TQ), F32),
                        pltpu.VMEM((HEAD_GROUP, V_ROWS, TQ), F32),
                        pltpu.VMEM((HEAD_GROUP, TK, TQ), F32), pltpu.VMEM((HEAD_GROUP, 1, TQ), F32)],
        compiler_params=pltpu.CompilerParams(dimension_semantics=("arbitrary",) * 3,
                                             vmem_limit_bytes=VMEM_LIMIT),
        name="attention",
    )(q, q, k, vt)


def _rope_tables(positions):
    inv_freq = ROPE_THETA ** (-jnp.arange(0, ROPE_DIM, 2, dtype=F32) / ROPE_DIM)
    ang = positions.astype(F32)[:, None, :] * inv_freq[None, :, None]
    return jnp.cos(ang), jnp.sin(ang)


def kernel(x, p, positions, ffn1_norm, ffn1_w_gu, ffn1_w_down, mix_norm, w_in, conv_w, w_conv_out,
           q_norm, kv_norm, w_uq, w_ukv, w_mla_out, w_o, ffn2_norm, ffn2_w_gu, ffn2_w_down,
           ple_norm, w_ple_gate, w_ple_proj, final_norm):
    assert x.shape == (BATCH, SEQ, D_MODEL) and p.shape == (DEPTH, BATCH, SEQ, PLE_DIM)
    bf = lambda w: w.astype(BF16)
    row = lambda g: g.reshape(DEPTH, 1, -1)

    k_rot = w_in[:, :, OFF_KR:OFF_KR + ROPE_DIM]
    w_in_r = bf(jnp.concatenate([w_in[:, :, :OFF_KR], k_rot, k_rot, w_in[:, :, OFF_KR + ROPE_DIM:]], axis=-1))
    w_uq_h = w_uq.reshape(DEPTH, Q_LORA, N_HEADS, QK_DIM)
    w_uq_t = bf(jnp.swapaxes(jnp.concatenate([w_uq_h[..., :NOPE_DIM].reshape(DEPTH, Q_LORA, -1),
                                              w_uq_h[..., NOPE_DIM:].reshape(DEPTH, Q_LORA, -1)],
                                             axis=-1), 1, 2))
    w_ukv_b = bf(w_ukv)
    ffn1_gu, ffn1_dn, ffn2_gu, ffn2_dn = bf(ffn1_w_gu), bf(ffn1_w_down), bf(ffn2_w_gu), bf(ffn2_w_down)
    w_co, w_mo, w_ob, w_pg, w_pp = bf(w_conv_out), bf(w_mla_out), bf(w_o), bf(w_ple_gate), bf(w_ple_proj)
    rope = _rope_tables(positions)
    p2 = p.reshape(DEPTH, TOKENS, PLE_DIM)

    h = x.reshape(TOKENS, D_MODEL)
    for layer in range(DEPTH):
        h = _ffn(h, layer, row(ffn1_norm), ffn1_gu, ffn1_dn)
        q, k, vt, gc, sg = _mix_proj(h.reshape(BATCH, SEQ, D_MODEL), layer, rope, row(mix_norm),
                                     w_in_r, conv_w, w_co, row(q_norm), row(kv_norm), w_uq_t, w_ukv_b)
        o = _attention(q, k, vt)
        h = _ffn(h, layer, row(ffn2_norm), ffn2_gu, ffn2_dn,
                 mix=(o.reshape(TOKENS, D_MODEL), gc.reshape(TOKENS, D_MODEL),
                      sg.reshape(TOKENS, D_MODEL), w_mo, w_ob),
                 ple=(p2, row(ple_norm), w_pg, w_pp),
                 final_norm=final_norm.reshape(1, D_MODEL) if layer == DEPTH - 1 else None)
    return h.reshape(BATCH, SEQ, D_MODEL)
```

```python
import functools

import jax
import jax.numpy as jnp
from jax import lax
from jax.experimental import pallas as pl
from jax.experimental.pallas import tpu as pltpu

D_MODEL = 1024
BATCH = 8
SEQ = 4096
DEPTH = 4
CHUNK = 64
PLE_DIM = 256
D_FF = 2816
CONV_DIM = 512
CONV_K = 3
N_HEADS = 8
NOPE_DIM = 128
ROPE_DIM = 64
V_DIM = 128
Q_LORA = 384
KV_LORA = 256
ROPE_THETA = 10000.0
EPS = 1e-6
QK_DIM = NOPE_DIM + ROPE_DIM
ATTN_SCALE = QK_DIM ** -0.5
TOKENS = BATCH * SEQ

LANES = 128
QK_PAD = 2 * LANES
OFF_B, OFF_C, OFF_V = 0, CONV_DIM, 2 * CONV_DIM
OFF_Q = 3 * CONV_DIM
OFF_KV = OFF_Q + Q_LORA
OFF_KR = OFF_KV + KV_LORA
OFF_GC = OFF_KR + LANES
OFF_GM = OFF_GC + D_MODEL
IN_COLS_PAD = OFF_GM + D_MODEL

TM_FFN = 512
FFN_GROUP = 256
FF_CHUNKS = ((0, 1536), (1536, D_FF))
TM_MIX = 512
TQ = 512
TK = 512
HEAD_GROUP = 4
VT_BLK = 256
VT_PER_TK = TK // VT_BLK
V_ROWS = V_DIM + 16
LOG2E = 1.4426950408889634
CONV_HALO = 8
VMEM_LIMIT = 56 * 1024 * 1024

F32 = jnp.float32
BF16 = jnp.bfloat16


def _rmsnorm(x, g):
    return x * lax.rsqrt(jnp.mean(x * x, axis=-1, keepdims=True) + EPS) * g


def _dot(a, b):
    return jnp.dot(a, b, preferred_element_type=F32)


def _dot_nt(a, b):
    return lax.dot_general(a, b, (((1,), (1,)), ((), ())), preferred_element_type=F32)


def _const_spec(shape, layer):
    zeros = (0,) * len(shape)
    return pl.BlockSpec((None,) + tuple(shape), lambda *_: (layer,) + zeros,
                        pipeline_mode=pl.Buffered(1))


def _ffn_kernel(*refs, with_mix, with_ple, with_final):
    refs = list(refs)
    o_ref = refs.pop()
    x_ref = refs.pop(0)
    groups = [slice(r * FFN_GROUP, (r + 1) * FFN_GROUP) for r in range(TM_FFN // FFN_GROUP)]
    xs = [x_ref[g, :] for g in groups]
    if with_mix:
        attn_ref, gc_ref, sg_ref, wmo_ref, wo_ref = refs[:5]
        del refs[:5]
        y_mlas = [_dot(attn_ref[g, :], wmo_ref[...]) for g in groups]
        merged = [(gc_ref[g, :] + sg_ref[g, :] * y).astype(BF16) for g, y in zip(groups, y_mlas)]
        xs = [x + _dot(m, wo_ref[...]) for x, m in zip(xs, merged)]
    g_ref, wgu_ref, wd_ref = refs[:3]
    del refs[:3]
    xns = [_rmsnorm(x, g_ref[...]).astype(BF16) for x in xs]
    ys = None
    for c0, c1 in (FF_CHUNKS if with_mix else ((0, D_FF),)):
        gs = [_dot(xn, wgu_ref[:, c0:c1]) for xn in xns]
        us = [_dot(xn, wgu_ref[:, D_FF + c0:D_FF + c1]) for xn in xns]
        acts = [(g * jax.nn.sigmoid(g) * u).astype(BF16) for g, u in zip(gs, us)]
        parts = [_dot(a, wd_ref[c0:c1, :]) for a in acts]
        ys = parts if ys is None else [y + part for y, part in zip(ys, parts)]
    hs = [x + 0.5 * y for x, y in zip(xs, ys)]
    if with_ple:
        p_ref, pg_ref, wpg_ref, wpp_ref = refs[:4]
        del refs[:4]
        hns = [_rmsnorm(h, pg_ref[...]).astype(BF16) for h in hs]
        gates = [jax.nn.sigmoid(_dot(hn, wpg_ref[...])) for hn in hns]
        hs = [h + gate * _dot(p_ref[g, :].astype(BF16), wpp_ref[...])
              for h, gate, g in zip(hs, gates, groups)]
    if with_final:
        hs = [_rmsnorm(h, refs[0][...]) for h in hs]
    for g, h in zip(groups, hs):
        o_ref[g, :] = h


def _ffn(h, layer, norm, w_gu, w_down, mix=None, ple=None, final_norm=None):
    tile = pl.BlockSpec((TM_FFN, D_MODEL), lambda i: (i, 0))
    in_specs, args = [tile], [h]
    if mix is not None:
        attn, gc, sg, w_mla_out, w_o = mix
        in_specs += [tile, tile, tile,
                     _const_spec((D_MODEL, D_MODEL), layer), _const_spec((D_MODEL, D_MODEL), layer)]
        args += [attn, gc, sg, w_mla_out, w_o]
    in_specs += [_const_spec((1, D_MODEL), layer),
                 _const_spec((D_MODEL, 2 * D_FF), layer), _const_spec((D_FF, D_MODEL), layer)]
    args += [norm, w_gu, w_down]
    if ple is not None:
        p, ple_norm, w_pg, w_pp = ple
        in_specs += [pl.BlockSpec((None, TM_FFN, PLE_DIM), lambda i: (layer, i, 0)),
                     _const_spec((1, D_MODEL), layer),
                     _const_spec((D_MODEL, D_MODEL), layer), _const_spec((PLE_DIM, D_MODEL), layer)]
        args += [p, ple_norm, w_pg, w_pp]
    if final_norm is not None:
        in_specs.append(pl.BlockSpec((1, D_MODEL), lambda i: (0, 0)))
        args.append(final_norm)
    return pl.pallas_call(
        functools.partial(_ffn_kernel, with_mix=mix is not None, with_ple=ple is not None,
                          with_final=final_norm is not None),
        grid=(TOKENS // TM_FFN,),
        in_specs=in_specs,
        out_specs=tile,
        out_shape=jax.ShapeDtypeStruct((TOKENS, D_MODEL), F32),
        compiler_params=pltpu.CompilerParams(dimension_semantics=("arbitrary",),
                                             vmem_limit_bytes=VMEM_LIMIT),
        name="mix_out_ffn_ple" if mix is not None else "ffn",
    )(*args)


def _rope_t(x1, x2, cos_c, sin_c):
    return x1 * cos_c - x2 * sin_c, x2 * cos_c + x1 * sin_c


def _mix_proj_kernel(h_ref, cost_ref, sint_ref, g_ref, win_ref, cw_ref, wco_ref,
                     qg_ref, kvg_ref, wuqt_ref, wukv_ref,
                     q_ref, k_ref, vt_ref, gc_ref, sg_ref, z_ref):
    groups = [slice(r * VT_BLK, (r + 1) * VT_BLK) for r in range(TM_MIX // VT_BLK)]

    @pl.when(pl.program_id(1) == 0)
    def _():
        z_ref[0:CONV_HALO, :] = jnp.zeros((CONV_HALO, CONV_DIM), F32)

    us = [_rmsnorm(h_ref[g, :], g_ref[...]).astype(BF16) for g in groups]
    projs = [_dot(u, win_ref[...]) for u in us]
    cw = cw_ref[...]
    rope0, half_rope = N_HEADS * NOPE_DIM, ROPE_DIM // 2
    ones_rows = (lax.broadcasted_iota(jnp.int32, (V_ROWS - V_DIM, VT_BLK), 0) == 0).astype(BF16)
    for r, (g, proj) in enumerate(zip(groups, projs)):
        z0 = CONV_HALO + r * VT_BLK
        z = proj[:, OFF_C:OFF_C + CONV_DIM] * proj[:, OFF_V:OFF_V + CONV_DIM]
        z_ref[z0:z0 + VT_BLK, :] = z
        y = (cw[0:1] * z_ref[z0 - 2:z0 - 2 + VT_BLK, :] + cw[1:2] * z_ref[z0 - 1:z0 - 1 + VT_BLK, :]
             + cw[2:3] * z)
        y_conv = _dot((proj[:, OFF_B:OFF_B + CONV_DIM] * y).astype(BF16), wco_ref[...])
        gc_ref[g, :] = (jax.nn.sigmoid(proj[:, OFF_GC:OFF_GC + D_MODEL]) * y_conv).astype(BF16)
        sg_ref[g, :] = jax.nn.sigmoid(proj[:, OFF_GM:OFF_GM + D_MODEL]).astype(BF16)

        qn = _rmsnorm(proj[:, OFF_Q:OFF_Q + Q_LORA], qg_ref[...]).astype(BF16)
        q_t = _dot_nt(wuqt_ref[...], qn) * (ATTN_SCALE * LOG2E)
        kvn = _rmsnorm(proj[:, OFF_KV:OFF_KV + KV_LORA], kvg_ref[...]).astype(BF16)
        kv = _dot(kvn, wukv_ref[...])
        cos_c, sin_c = cost_ref[:, g], sint_ref[:, g]
        kr_t = proj[:, OFF_KR:OFF_KR + LANES].T
        r1, r2 = _rope_t(kr_t[0:half_rope], kr_t[half_rope:ROPE_DIM], cos_c, sin_c)
        pad = jnp.zeros((LANES - ROPE_DIM, VT_BLK), F32)
        k_rope = jnp.concatenate([r1, r2, pad], axis=0).T.astype(BF16)
        for hd in range(N_HEADS):
            q_ref[hd, 0:NOPE_DIM, g] = q_t[hd * NOPE_DIM:(hd + 1) * NOPE_DIM, :].astype(BF16)
            r1, r2 = _rope_t(q_t[rope0 + hd * ROPE_DIM:rope0 + hd * ROPE_DIM + half_rope, :],
                             q_t[rope0 + hd * ROPE_DIM + half_rope:rope0 + (hd + 1) * ROPE_DIM, :],
                             cos_c, sin_c)
            q_ref[hd, NOPE_DIM:NOPE_DIM + half_rope, g] = r1.astype(BF16)
            q_ref[hd, NOPE_DIM + half_rope:QK_DIM, g] = r2.astype(BF16)
            q_ref[hd, QK_DIM:QK_PAD, g] = jnp.zeros((QK_PAD - QK_DIM, VT_BLK), BF16)
            kv0 = hd * (NOPE_DIM + V_DIM)
            k_ref[hd, g, 0:LANES] = kv[:, kv0:kv0 + NOPE_DIM].astype(BF16)
            k_ref[hd, g, LANES:QK_PAD] = k_rope
            vt_ref[hd, r, 0:V_DIM, :] = kv[:, kv0 + NOPE_DIM:kv0 + NOPE_DIM + V_DIM].T.astype(BF16)
            vt_ref[hd, r, V_DIM:V_ROWS, :] = ones_rows
    z_ref[0:CONV_HALO, :] = z_ref[TM_MIX:TM_MIX + CONV_HALO, :]


def _mix_proj(h, layer, rope, norm, w_in, conv_w, w_conv_out, q_norm, kv_norm, w_uqt, w_ukv):
    n_t = SEQ // TM_MIX
    tok = lambda w: pl.BlockSpec((None, TM_MIX, w), lambda b, i: (b, i, 0))
    tok_t = lambda w: pl.BlockSpec((None, w, TM_MIX), lambda b, i: (b, 0, i))
    return pl.pallas_call(
        _mix_proj_kernel,
        grid=(BATCH, n_t),
        in_specs=[tok(D_MODEL), tok_t(ROPE_DIM // 2), tok_t(ROPE_DIM // 2),
                  _const_spec((1, D_MODEL), layer), _const_spec((D_MODEL, IN_COLS_PAD), layer),
                  _const_spec((CONV_K, CONV_DIM), layer), _const_spec((CONV_DIM, D_MODEL), layer),
                  _const_spec((1, Q_LORA), layer), _const_spec((1, KV_LORA), layer),
                  _const_spec((N_HEADS * QK_DIM, Q_LORA), layer),
                  _const_spec((KV_LORA, N_HEADS * (NOPE_DIM + V_DIM)), layer)],
        out_specs=[pl.BlockSpec((None, N_HEADS, QK_PAD, TM_MIX), lambda b, i: (b, 0, 0, i)),
                   pl.BlockSpec((None, N_HEADS, TM_MIX, QK_PAD), lambda b, i: (b, 0, i, 0)),
                   pl.BlockSpec((None, N_HEADS, TM_MIX // VT_BLK, V_ROWS, VT_BLK),
                                lambda b, i: (b, 0, i, 0, 0)),
                   tok(D_MODEL), tok(D_MODEL)],
        out_shape=[jax.ShapeDtypeStruct((BATCH, N_HEADS, QK_PAD, SEQ), BF16),
                   jax.ShapeDtypeStruct((BATCH, N_HEADS, SEQ, QK_PAD), BF16),
                   jax.ShapeDtypeStruct((BATCH, N_HEADS, SEQ // VT_BLK, V_ROWS, VT_BLK), BF16),
                   jax.ShapeDtypeStruct((BATCH, SEQ, D_MODEL), BF16),
                   jax.ShapeDtypeStruct((BATCH, SEQ, D_MODEL), BF16)],
        scratch_shapes=[pltpu.VMEM((TM_MIX + CONV_HALO, CONV_DIM), F32)],
        compiler_params=pltpu.CompilerParams(dimension_semantics=("arbitrary", "arbitrary"),
                                             vmem_limit_bytes=VMEM_LIMIT),
        name="mix_proj",
    )(h, *rope, norm, w_in, conv_w, w_conv_out, q_norm, kv_norm, w_uqt, w_ukv)


def _attn_kernel(q_ref, qnext_ref, k_ref, vt_ref, o_ref, m_ref, acc_ref, s_ref, tmax_ref):
    n_full = pl.program_id(2)
    half = TQ // 2
    blk_mask = (lax.broadcasted_iota(jnp.int32, (half, half), 0) // CHUNK
                <= lax.broadcasted_iota(jnp.int32, (half, half), 1) // CHUNK)

    heads = range(HEAD_GROUP)
    m_ref[...] = jnp.full(m_ref.shape, -1e30, F32)
    acc_ref[...] = jnp.zeros(acc_ref.shape, F32)

    def scores(hd, j, queries=q_ref):
        k0 = pl.multiple_of(j * TK, TK)
        s = _dot(k_ref[hd, pl.ds(k0, TK), :], queries[hd])
        s_ref[hd] = s
        tmax_ref[hd] = jnp.max(s, axis=0, keepdims=True)

    def softmax(hd, s, tmax, cols=slice(None)):
        m = m_ref[hd, :, cols]
        m_new = jnp.maximum(m, tmax)
        alpha = jnp.exp2(m - m_new)
        m_ref[hd, :, cols] = m_new
        return jnp.exp2((s - m_new).astype(BF16)), alpha

    def weighted_values(hd, j, p, alpha, cols=slice(None)):
        pv = _dot(vt_ref[hd, j * VT_PER_TK], p[0:VT_BLK])
        for c in range(1, p.shape[0] // VT_BLK):
            pv += _dot(vt_ref[hd, j * VT_PER_TK + c], p[c * VT_BLK:(c + 1) * VT_BLK])
        acc_ref[hd, :, cols] = alpha * acc_ref[hd, :, cols] + pv

    @pl.when(n_full == 0)
    def _():
        for hd in heads:
            scores(hd, 0)

    def pipelined_step(j):
        for hd in heads:
            s, tmax = s_ref[hd], tmax_ref[hd]
            scores(hd, j + 1)
            p, alpha = softmax(hd, s, tmax)
            weighted_values(hd, j, p, alpha)

    @pl.loop(0, n_full // 2)
    def _(t):
        pipelined_step(2 * t)
        pipelined_step(2 * t + 1)

    @pl.when(n_full % 2 == 1)
    def _():
        pipelined_step(n_full - 1)

    lo, hi = slice(0, half), slice(half, TQ)

    def diagonal_tile(prefetch_next):
        for hd in heads:
            s_lo = jnp.where(blk_mask, s_ref[hd, lo, lo], -1e30)
            s_hi = jnp.concatenate([s_ref[hd, lo, hi], jnp.where(blk_mask, s_ref[hd, hi, hi], -1e30)],
                                   axis=0)
            if prefetch_next:
                scores(hd, 0, qnext_ref)
            p, alpha = softmax(hd, s_lo, jnp.max(s_lo, axis=0, keepdims=True), lo)
            weighted_values(hd, n_full, p, alpha, lo)
            p, alpha = softmax(hd, s_hi, jnp.max(s_hi, axis=0, keepdims=True), hi)
            weighted_values(hd, n_full, p, alpha, hi)

    last = pl.num_programs(2) - 1
    pl.when(n_full < last)(lambda: diagonal_tile(True))
    pl.when(n_full == last)(lambda: diagonal_tile(False))
    for hd in heads:
        out = acc_ref[hd, 0:V_DIM, :] / acc_ref[hd, V_DIM:V_DIM + 1, :]
        o_ref[:, hd * V_DIM:(hd + 1) * V_DIM] = out.T.astype(BF16)


def _attention(q, k, vt):
    assert TQ == TK and (TQ // 2) % VT_BLK == 0 and (TQ // 2) % CHUNK == 0
    return pl.pallas_call(
        _attn_kernel,
        grid=(BATCH, N_HEADS // HEAD_GROUP, SEQ // TQ),
        in_specs=[pl.BlockSpec((None, HEAD_GROUP, QK_PAD, TQ), lambda b, g, i: (b, g, 0, i)),
                  pl.BlockSpec((None, HEAD_GROUP, QK_PAD, TQ),
                               lambda b, g, i: (b, g, 0, jnp.minimum(i + 1, SEQ // TQ - 1))),
                  pl.BlockSpec((None, HEAD_GROUP, SEQ, QK_PAD), lambda b, g, i: (b, g, 0, 0)),
                  pl.BlockSpec((None, HEAD_GROUP, SEQ // VT_BLK, V_ROWS, VT_BLK),
                               lambda b, g, i: (b, g, 0, 0, 0))],
        out_specs=pl.BlockSpec((None, TQ, HEAD_GROUP * V_DIM), lambda b, g, i: (b, i, g)),
        out_shape=jax.ShapeDtypeStruct((BATCH, SEQ, N_HEADS * V_DIM), BF16),
        scratch_shapes=[pltpu.VMEM((HEAD_GROUP, 1, TQ), F32),
                        pltpu.VMEM((HEAD_GROUP, V_ROWS, TQ), F32),
                        pltpu.VMEM((HEAD_GROUP, TK, TQ), F32), pltpu.VMEM((HEAD_GROUP, 1, TQ), F32)],
        compiler_params=pltpu.CompilerParams(dimension_semantics=("arbitrary",) * 3,
                                             vmem_limit_bytes=VMEM_LIMIT),
        name="attention",
    )(q, q, k, vt)


def _rope_tables(positions):
    inv_freq = ROPE_THETA ** (-jnp.arange(0, ROPE_DIM, 2, dtype=F32) / ROPE_DIM)
    ang = positions.astype(F32)[:, None, :] * inv_freq[None, :, None]
    return jnp.cos(ang), jnp.sin(ang)


def kernel(x, p, positions, ffn1_norm, ffn1_w_gu, ffn1_w_down, mix_norm, w_in, conv_w, w_conv_out,
           q_norm, kv_norm, w_uq, w_ukv, w_mla_out, w_o, ffn2_norm, ffn2_w_gu, ffn2_w_down,
           ple_norm, w_ple_gate, w_ple_proj, final_norm):
    assert x.shape == (BATCH, SEQ, D_MODEL) and p.shape == (DEPTH, BATCH, SEQ, PLE_DIM)
    bf = lambda w: w.astype(BF16)
    row = lambda g: g.reshape(DEPTH, 1, -1)

    w_in_b = bf(w_in)
    k_rot = w_in_b[:, :, OFF_KR:OFF_KR + ROPE_DIM]
    w_in_r = jnp.concatenate([w_in_b[:, :, :OFF_KR], k_rot, k_rot, w_in_b[:, :, OFF_KR + ROPE_DIM:]], axis=-1)
    w_uq_h = w_uq.reshape(DEPTH, Q_LORA, N_HEADS, QK_DIM)
    w_uq_t = bf(jnp.swapaxes(jnp.concatenate([w_uq_h[..., :NOPE_DIM].reshape(DEPTH, Q_LORA, -1),
                                              w_uq_h[..., NOPE_DIM:].reshape(DEPTH, Q_LORA, -1)],
                                             axis=-1), 1, 2))
    w_ukv_b = bf(w_ukv)
    ffn1_gu, ffn1_dn, ffn2_gu, ffn2_dn = bf(ffn1_w_gu), bf(ffn1_w_down), bf(ffn2_w_gu), bf(ffn2_w_down)
    w_co, w_mo, w_ob, w_pg, w_pp = bf(w_conv_out), bf(w_mla_out), bf(w_o), bf(w_ple_gate), bf(w_ple_proj)
    rope = _rope_tables(positions)
    p2 = p.reshape(DEPTH, TOKENS, PLE_DIM)

    h = x.reshape(TOKENS, D_MODEL)
    for layer in range(DEPTH):
        h = _ffn(h, layer, row(ffn1_norm), ffn1_gu, ffn1_dn)
        q, k, vt, gc, sg = _mix_proj(h.reshape(BATCH, SEQ, D_MODEL), layer, rope, row(mix_norm),
                                     w_in_r, conv_w, w_co, row(q_norm), row(kv_norm), w_uq_t, w_ukv_b)
        o = _attention(q, k, vt)
        h = _ffn(h, layer, row(ffn2_norm), ffn2_gu, ffn2_dn,
                 mix=(o.reshape(TOKENS, D_MODEL), gc.reshape(TOKENS, D_MODEL),
                      sg.reshape(TOKENS, D_MODEL), w_mo, w_ob),
                 ple=(p2, row(ple_norm), w_pg, w_pp),
                 final_norm=final_norm.reshape(1, D_MODEL) if layer == DEPTH - 1 else None)
    return h.reshape(BATCH, SEQ, D_MODEL)
```

```python
import functools

import jax
import jax.numpy as jnp
from jax import lax
from jax.experimental import pallas as pl
from jax.experimental.pallas import tpu as pltpu

D_MODEL = 1024
BATCH = 8
SEQ = 4096
DEPTH = 4
CHUNK = 64
PLE_DIM = 256
D_FF = 2816
CONV_DIM = 512
CONV_K = 3
N_HEADS = 8
NOPE_DIM = 128
ROPE_DIM = 64
V_DIM = 128
Q_LORA = 384
KV_LORA = 256
ROPE_THETA = 10000.0
EPS = 1e-6
QK_DIM = NOPE_DIM + ROPE_DIM
ATTN_SCALE = QK_DIM ** -0.5
TOKENS = BATCH * SEQ

LANES = 128
F32_SUBLANES = 8
BF16_SUBLANES = 16
MXU_DIM = 256
QK_PAD = MXU_DIM
OFF_B, OFF_C, OFF_V = 0, CONV_DIM, 2 * CONV_DIM
OFF_Q = 3 * CONV_DIM
OFF_KV = OFF_Q + Q_LORA
OFF_KR = OFF_KV + KV_LORA
OFF_GC = OFF_KR + LANES
OFF_GM = OFF_GC + D_MODEL
IN_COLS_PAD = OFF_GM + D_MODEL

TM_FFN = 512
FFN_GROUP = MXU_DIM
FF_SPLIT = 6 * MXU_DIM
FF_CHUNKS = ((0, FF_SPLIT), (FF_SPLIT, D_FF))
TM_MIX = 512
TQ = 512
TK = 512
HEAD_GROUP = 4
VT_BLK = MXU_DIM
VT_PER_TK = TK // VT_BLK
V_ROWS = V_DIM + BF16_SUBLANES
LOG2E = 1.4426950408889634
CONV_HALO = F32_SUBLANES
VMEM_LIMIT = 56 * 1024 * 1024

F32 = jnp.float32
BF16 = jnp.bfloat16


def _rmsnorm(x, g):
    return x * lax.rsqrt(jnp.mean(x * x, axis=-1, keepdims=True) + EPS) * g


def _dot(a, b):
    return jnp.dot(a, b, preferred_element_type=F32)


def _dot_nt(a, b):
    return lax.dot_general(a, b, (((1,), (1,)), ((), ())), preferred_element_type=F32)


def _const_spec(shape, layer):
    zeros = (0,) * len(shape)
    return pl.BlockSpec((None,) + tuple(shape), lambda *_: (layer,) + zeros,
                        pipeline_mode=pl.Buffered(1))


def _ffn_kernel(*refs, with_mix, with_ple, with_final):
    refs = list(refs)
    o_ref = refs.pop()
    x_ref = refs.pop(0)
    groups = [slice(r * FFN_GROUP, (r + 1) * FFN_GROUP) for r in range(TM_FFN // FFN_GROUP)]
    xs = [x_ref[g, :] for g in groups]
    if with_mix:
        attn_ref, gc_ref, sg_ref, wmo_ref, wo_ref = refs[:5]
        del refs[:5]
        y_mlas = [_dot(attn_ref[g, :], wmo_ref[...]) for g in groups]
        merged = [(gc_ref[g, :] + sg_ref[g, :] * y).astype(BF16) for g, y in zip(groups, y_mlas)]
        xs = [x + _dot(m, wo_ref[...]) for x, m in zip(xs, merged)]
    g_ref, wgu_ref, wd_ref = refs[:3]
    del refs[:3]
    xns = [_rmsnorm(x, g_ref[...]).astype(BF16) for x in xs]
    ys = None
    for c0, c1 in (FF_CHUNKS if with_mix else ((0, D_FF),)):
        gs = [_dot(xn, wgu_ref[:, c0:c1]) for xn in xns]
        us = [_dot(xn, wgu_ref[:, D_FF + c0:D_FF + c1]) for xn in xns]
        acts = [(g * jax.nn.sigmoid(g) * u).astype(BF16) for g, u in zip(gs, us)]
        parts = [_dot(a, wd_ref[c0:c1, :]) for a in acts]
        ys = parts if ys is None else [y + part for y, part in zip(ys, parts)]
    hs = [x + 0.5 * y for x, y in zip(xs, ys)]
    if with_ple:
        p_ref, pg_ref, wpg_ref, wpp_ref = refs[:4]
        del refs[:4]
        hns = [_rmsnorm(h, pg_ref[...]).astype(BF16) for h in hs]
        gates = [jax.nn.sigmoid(_dot(hn, wpg_ref[...])) for hn in hns]
        hs = [h + gate * _dot(p_ref[g, :].astype(BF16), wpp_ref[...])
              for h, gate, g in zip(hs, gates, groups)]
    if with_final:
        hs = [_rmsnorm(h, refs[0][...]) for h in hs]
    for g, h in zip(groups, hs):
        o_ref[g, :] = h


def _ffn(h, layer, norm, w_gu, w_down, mix=None, ple=None, final_norm=None):
    tile = pl.BlockSpec((TM_FFN, D_MODEL), lambda i: (i, 0))
    in_specs, args = [tile], [h]
    if mix is not None:
        attn, gc, sg, w_mla_out, w_o = mix
        in_specs += [tile, tile, tile,
                     _const_spec((D_MODEL, D_MODEL), layer), _const_spec((D_MODEL, D_MODEL), layer)]
        args += [attn, gc, sg, w_mla_out, w_o]
    in_specs += [_const_spec((1, D_MODEL), layer),
                 _const_spec((D_MODEL, 2 * D_FF), layer), _const_spec((D_FF, D_MODEL), layer)]
    args += [norm, w_gu, w_down]
    if ple is not None:
        p, ple_norm, w_pg, w_pp = ple
        in_specs += [pl.BlockSpec((None, TM_FFN, PLE_DIM), lambda i: (layer, i, 0)),
                     _const_spec((1, D_MODEL), layer),
                     _const_spec((D_MODEL, D_MODEL), layer), _const_spec((PLE_DIM, D_MODEL), layer)]
        args += [p, ple_norm, w_pg, w_pp]
    if final_norm is not None:
        in_specs.append(pl.BlockSpec((1, D_MODEL), lambda i: (0, 0)))
        args.append(final_norm)
    return pl.pallas_call(
        functools.partial(_ffn_kernel, with_mix=mix is not None, with_ple=ple is not None,
                          with_final=final_norm is not None),
        grid=(TOKENS // TM_FFN,),
        in_specs=in_specs,
        out_specs=tile,
        out_shape=jax.ShapeDtypeStruct((TOKENS, D_MODEL), F32),
        compiler_params=pltpu.CompilerParams(dimension_semantics=("arbitrary",),
                                             vmem_limit_bytes=VMEM_LIMIT),
        name="mix_out_ffn_ple" if mix is not None else "ffn",
    )(*args)


def _rope_t(x1, x2, cos_c, sin_c):
    return x1 * cos_c - x2 * sin_c, x2 * cos_c + x1 * sin_c


def _mix_proj_kernel(h_ref, cost_ref, sint_ref, g_ref, win_ref, cw_ref, wco_ref,
                     qg_ref, kvg_ref, wuqt_ref, wukv_ref,
                     q_ref, k_ref, vt_ref, gc_ref, sg_ref, z_ref):
    groups = [slice(r * VT_BLK, (r + 1) * VT_BLK) for r in range(TM_MIX // VT_BLK)]

    @pl.when(pl.program_id(1) == 0)
    def _():
        z_ref[0:CONV_HALO, :] = jnp.zeros((CONV_HALO, CONV_DIM), F32)

    us = [_rmsnorm(h_ref[g, :], g_ref[...]).astype(BF16) for g in groups]
    projs = [_dot(u, win_ref[...]) for u in us]
    cw = cw_ref[...]
    rope0, half_rope = N_HEADS * NOPE_DIM, ROPE_DIM // 2
    ones_rows = (lax.broadcasted_iota(jnp.int32, (V_ROWS - V_DIM, VT_BLK), 0) == 0).astype(BF16)
    for r, (g, proj) in enumerate(zip(groups, projs)):
        z0 = CONV_HALO + r * VT_BLK
        z = proj[:, OFF_C:OFF_C + CONV_DIM] * proj[:, OFF_V:OFF_V + CONV_DIM]
        z_ref[z0:z0 + VT_BLK, :] = z
        y = (cw[0:1] * z_ref[z0 - 2:z0 - 2 + VT_BLK, :] + cw[1:2] * z_ref[z0 - 1:z0 - 1 + VT_BLK, :]
             + cw[2:3] * z)
        y_conv = _dot((proj[:, OFF_B:OFF_B + CONV_DIM] * y).astype(BF16), wco_ref[...])
        gc_ref[g, :] = (jax.nn.sigmoid(proj[:, OFF_GC:OFF_GC + D_MODEL]) * y_conv).astype(BF16)
        sg_ref[g, :] = jax.nn.sigmoid(proj[:, OFF_GM:OFF_GM + D_MODEL]).astype(BF16)

        qn = _rmsnorm(proj[:, OFF_Q:OFF_Q + Q_LORA], qg_ref[...]).astype(BF16)
        q_t = _dot_nt(wuqt_ref[...], qn) * (ATTN_SCALE * LOG2E)
        kvn = _rmsnorm(proj[:, OFF_KV:OFF_KV + KV_LORA], kvg_ref[...]).astype(BF16)
        kv = _dot(kvn, wukv_ref[...])
        cos_c, sin_c = cost_ref[:, g], sint_ref[:, g]
        kr_t = proj[:, OFF_KR:OFF_KR + LANES].T
        r1, r2 = _rope_t(kr_t[0:half_rope], kr_t[half_rope:ROPE_DIM], cos_c, sin_c)
        pad = jnp.zeros((LANES - ROPE_DIM, VT_BLK), F32)
        k_rope = jnp.concatenate([r1, r2, pad], axis=0).T.astype(BF16)
        for hd in range(N_HEADS):
            q_ref[hd, 0:NOPE_DIM, g] = q_t[hd * NOPE_DIM:(hd + 1) * NOPE_DIM, :].astype(BF16)
            r1, r2 = _rope_t(q_t[rope0 + hd * ROPE_DIM:rope0 + hd * ROPE_DIM + half_rope, :],
                             q_t[rope0 + hd * ROPE_DIM + half_rope:rope0 + (hd + 1) * ROPE_DIM, :],
                             cos_c, sin_c)
            q_ref[hd, NOPE_DIM:NOPE_DIM + half_rope, g] = r1.astype(BF16)
            q_ref[hd, NOPE_DIM + half_rope:QK_DIM, g] = r2.astype(BF16)
            q_ref[hd, QK_DIM:QK_PAD, g] = jnp.zeros((QK_PAD - QK_DIM, VT_BLK), BF16)
            kv0 = hd * (NOPE_DIM + V_DIM)
            k_ref[hd, g, 0:LANES] = kv[:, kv0:kv0 + NOPE_DIM].astype(BF16)
            k_ref[hd, g, LANES:QK_PAD] = k_rope
            vt_ref[hd, r, 0:V_DIM, :] = kv[:, kv0 + NOPE_DIM:kv0 + NOPE_DIM + V_DIM].T.astype(BF16)
            vt_ref[hd, r, V_DIM:V_ROWS, :] = ones_rows
    z_ref[0:CONV_HALO, :] = z_ref[TM_MIX:TM_MIX + CONV_HALO, :]


def _mix_proj(h, layer, rope, norm, w_in, conv_w, w_conv_out, q_norm, kv_norm, w_uqt, w_ukv):
    n_t = SEQ // TM_MIX
    tok = lambda w: pl.BlockSpec((None, TM_MIX, w), lambda b, i: (b, i, 0))
    tok_t = lambda w: pl.BlockSpec((None, w, TM_MIX), lambda b, i: (b, 0, i))
    return pl.pallas_call(
        _mix_proj_kernel,
        grid=(BATCH, n_t),
        in_specs=[tok(D_MODEL), tok_t(ROPE_DIM // 2), tok_t(ROPE_DIM // 2),
                  _const_spec((1, D_MODEL), layer), _const_spec((D_MODEL, IN_COLS_PAD), layer),
                  _const_spec((CONV_K, CONV_DIM), layer), _const_spec((CONV_DIM, D_MODEL), layer),
                  _const_spec((1, Q_LORA), layer), _const_spec((1, KV_LORA), layer),
                  _const_spec((N_HEADS * QK_DIM, Q_LORA), layer),
                  _const_spec((KV_LORA, N_HEADS * (NOPE_DIM + V_DIM)), layer)],
        out_specs=[pl.BlockSpec((None, N_HEADS, QK_PAD, TM_MIX), lambda b, i: (b, 0, 0, i)),
                   pl.BlockSpec((None, N_HEADS, TM_MIX, QK_PAD), lambda b, i: (b, 0, i, 0)),
                   pl.BlockSpec((None, N_HEADS, TM_MIX // VT_BLK, V_ROWS, VT_BLK),
                                lambda b, i: (b, 0, i, 0, 0)),
                   tok(D_MODEL), tok(D_MODEL)],
        out_shape=[jax.ShapeDtypeStruct((BATCH, N_HEADS, QK_PAD, SEQ), BF16),
                   jax.ShapeDtypeStruct((BATCH, N_HEADS, SEQ, QK_PAD), BF16),
                   jax.ShapeDtypeStruct((BATCH, N_HEADS, SEQ // VT_BLK, V_ROWS, VT_BLK), BF16),
                   jax.ShapeDtypeStruct((BATCH, SEQ, D_MODEL), BF16),
                   jax.ShapeDtypeStruct((BATCH, SEQ, D_MODEL), BF16)],
        scratch_shapes=[pltpu.VMEM((TM_MIX + CONV_HALO, CONV_DIM), F32)],
        compiler_params=pltpu.CompilerParams(dimension_semantics=("arbitrary", "arbitrary"),
                                             vmem_limit_bytes=VMEM_LIMIT),
        name="mix_proj",
    )(h, *rope, norm, w_in, conv_w, w_conv_out, q_norm, kv_norm, w_uqt, w_ukv)


def _attn_kernel(q_ref, qnext_ref, k_ref, vt_ref, o_ref, m_ref, acc_ref, s_ref, tmax_ref):
    n_full = pl.program_id(2)
    half = TQ // 2
    blk_mask = (lax.broadcasted_iota(jnp.int32, (half, half), 0) // CHUNK
                <= lax.broadcasted_iota(jnp.int32, (half, half), 1) // CHUNK)

    heads = range(HEAD_GROUP)
    m_ref[...] = jnp.full(m_ref.shape, -1e30, F32)
    acc_ref[...] = jnp.zeros(acc_ref.shape, F32)

    def scores(hd, j, queries=q_ref):
        k0 = pl.multiple_of(j * TK, TK)
        s = _dot(k_ref[hd, pl.ds(k0, TK), :], queries[hd])
        s_ref[hd] = s
        tmax_ref[hd] = jnp.max(s, axis=0, keepdims=True)

    def softmax(hd, s, tmax, cols=slice(None)):
        m = m_ref[hd, :, cols]
        m_new = jnp.maximum(m, tmax)
        alpha = jnp.exp2(m - m_new)
        m_ref[hd, :, cols] = m_new
        return jnp.exp2((s - m_new).astype(BF16)), alpha

    def weighted_values(hd, j, p, alpha, cols=slice(None)):
        pv = _dot(vt_ref[hd, j * VT_PER_TK], p[0:VT_BLK])
        for c in range(1, p.shape[0] // VT_BLK):
            pv += _dot(vt_ref[hd, j * VT_PER_TK + c], p[c * VT_BLK:(c + 1) * VT_BLK])
        acc_ref[hd, :, cols] = alpha * acc_ref[hd, :, cols] + pv

    @pl.when(n_full == 0)
    def _():
        for hd in heads:
            scores(hd, 0)

    def pipelined_step(j):
        for hd in heads:
            s, tmax = s_ref[hd], tmax_ref[hd]
            scores(hd, j + 1)
            p, alpha = softmax(hd, s, tmax)
            weighted_values(hd, j, p, alpha)

    @pl.loop(0, n_full // 2)
    def _(t):
        pipelined_step(2 * t)
        pipelined_step(2 * t + 1)

    @pl.when(n_full % 2 == 1)
    def _():
        pipelined_step(n_full - 1)

    lo, hi = slice(0, half), slice(half, TQ)

    def diagonal_tile(prefetch_next):
        for hd in heads:
            s_lo = jnp.where(blk_mask, s_ref[hd, lo, lo], -1e30)
            s_hi = jnp.concatenate([s_ref[hd, lo, hi], jnp.where(blk_mask, s_ref[hd, hi, hi], -1e30)],
                                   axis=0)
            if prefetch_next:
                scores(hd, 0, qnext_ref)
            p, alpha = softmax(hd, s_lo, jnp.max(s_lo, axis=0, keepdims=True), lo)
            weighted_values(hd, n_full, p, alpha, lo)
            p, alpha = softmax(hd, s_hi, jnp.max(s_hi, axis=0, keepdims=True), hi)
            weighted_values(hd, n_full, p, alpha, hi)

    last = pl.num_programs(2) - 1
    pl.when(n_full < last)(lambda: diagonal_tile(True))
    pl.when(n_full == last)(lambda: diagonal_tile(False))
    for hd in heads:
        out = acc_ref[hd, 0:V_DIM, :] / acc_ref[hd, V_DIM:V_DIM + 1, :]
        o_ref[:, hd * V_DIM:(hd + 1) * V_DIM] = out.T.astype(BF16)


def _attention(q, k, vt):
    assert TQ == TK and (TQ // 2) % VT_BLK == 0 and (TQ // 2) % CHUNK == 0
    return pl.pallas_call(
        _attn_kernel,
        grid=(BATCH, N_HEADS // HEAD_GROUP, SEQ // TQ),
        in_specs=[pl.BlockSpec((None, HEAD_GROUP, QK_PAD, TQ), lambda b, g, i: (b, g, 0, i)),
                  pl.BlockSpec((None, HEAD_GROUP, QK_PAD, TQ),
                               lambda b, g, i: (b, g, 0, jnp.minimum(i + 1, SEQ // TQ - 1))),
                  pl.BlockSpec((None, HEAD_GROUP, SEQ, QK_PAD), lambda b, g, i: (b, g, 0, 0)),
                  pl.BlockSpec((None, HEAD_GROUP, SEQ // VT_BLK, V_ROWS, VT_BLK),
                               lambda b, g, i: (b, g, 0, 0, 0))],
        out_specs=pl.BlockSpec((None, TQ, HEAD_GROUP * V_DIM), lambda b, g, i: (b, i, g)),
        out_shape=jax.ShapeDtypeStruct((BATCH, SEQ, N_HEADS * V_DIM), BF16),
        scratch_shapes=[pltpu.VMEM((HEAD_GROUP, 1, TQ), F32),
                        pltpu.VMEM((HEAD_GROUP, V_ROWS, TQ), F32),
                        pltpu.VMEM((HEAD_GROUP, TK, TQ), F32), pltpu.VMEM((HEAD_GROUP, 1, TQ), F32)],
        compiler_params=pltpu.CompilerParams(dimension_semantics=("arbitrary",) * 3,
                                             vmem_limit_bytes=VMEM_LIMIT),
        name="attention",
    )(q, q, k, vt)


def _rope_tables(positions):
    inv_freq = ROPE_THETA ** (-jnp.arange(0, ROPE_DIM, 2, dtype=F32) / ROPE_DIM)
    ang = positions.astype(F32)[:, None, :] * inv_freq[None, :, None]
    return jnp.cos(ang), jnp.sin(ang)


def kernel(x, p, positions, ffn1_norm, ffn1_w_gu, ffn1_w_down, mix_norm, w_in, conv_w, w_conv_out,
           q_norm, kv_norm, w_uq, w_ukv, w_mla_out, w_o, ffn2_norm, ffn2_w_gu, ffn2_w_down,
           ple_norm, w_ple_gate, w_ple_proj, final_norm):
    assert x.shape == (BATCH, SEQ, D_MODEL) and p.shape == (DEPTH, BATCH, SEQ, PLE_DIM)
    bf = lambda w: w.astype(BF16)
    row = lambda g: g.reshape(DEPTH, 1, -1)

    w_in_b = bf(w_in)
    k_rot = w_in_b[:, :, OFF_KR:OFF_KR + ROPE_DIM]
    w_in_r = jnp.concatenate([w_in_b[:, :, :OFF_KR], k_rot, k_rot, w_in_b[:, :, OFF_KR + ROPE_DIM:]], axis=-1)
    w_uq_h = w_uq.reshape(DEPTH, Q_LORA, N_HEADS, QK_DIM)
    w_uq_t = bf(jnp.swapaxes(jnp.concatenate([w_uq_h[..., :NOPE_DIM].reshape(DEPTH, Q_LORA, -1),
                                              w_uq_h[..., NOPE_DIM:].reshape(DEPTH, Q_LORA, -1)],
                                             axis=-1), 1, 2))
    w_ukv_b = bf(w_ukv)
    ffn1_gu, ffn1_dn, ffn2_gu, ffn2_dn = bf(ffn1_w_gu), bf(ffn1_w_down), bf(ffn2_w_gu), bf(ffn2_w_down)
    w_co, w_mo, w_ob, w_pg, w_pp = bf(w_conv_out), bf(w_mla_out), bf(w_o), bf(w_ple_gate), bf(w_ple_proj)
    rope = _rope_tables(positions)
    p2 = p.reshape(DEPTH, TOKENS, PLE_DIM)

    h = x.reshape(TOKENS, D_MODEL)
    for layer in range(DEPTH):
        h = _ffn(h, layer, row(ffn1_norm), ffn1_gu, ffn1_dn)
        q, k, vt, gc, sg = _mix_proj(h.reshape(BATCH, SEQ, D_MODEL), layer, rope, row(mix_norm),
                                     w_in_r, conv_w, w_co, row(q_norm), row(kv_norm), w_uq_t, w_ukv_b)
        o = _attention(q, k, vt)
        h = _ffn(h, layer, row(ffn2_norm), ffn2_gu, ffn2_dn,
                 mix=(o.reshape(TOKENS, D_MODEL), gc.reshape(TOKENS, D_MODEL),
                      sg.reshape(TOKENS, D_MODEL), w_mo, w_ob),
                 ple=(p2, row(ple_norm), w_pg, w_pp),
                 final_norm=final_norm.reshape(1, D_MODEL) if layer == DEPTH - 1 else None)
    return h.reshape(BATCH, SEQ, D_MODEL)
```

```python
import functools

import jax
import jax.numpy as jnp
from jax import lax
from jax.experimental import pallas as pl
from jax.experimental.pallas import tpu as pltpu

D_MODEL = 1024
BATCH = 8
SEQ = 4096
DEPTH = 4
CHUNK = 64
PLE_DIM = 256
D_FF = 2816
CONV_DIM = 512
CONV_K = 3
N_HEADS = 8
NOPE_DIM = 128
ROPE_DIM = 64
V_DIM = 128
Q_LORA = 384
KV_LORA = 256
ROPE_THETA = 10000.0
EPS = 1e-6
QK_DIM = NOPE_DIM + ROPE_DIM
ATTN_SCALE = QK_DIM ** -0.5
TOKENS = BATCH * SEQ

LANES = 128
F32_SUBLANES = 8
BF16_SUBLANES = 16
MXU_DIM = 256
QK_PAD = MXU_DIM
OFF_B, OFF_C, OFF_V = 0, CONV_DIM, 2 * CONV_DIM
OFF_Q = 3 * CONV_DIM
OFF_KV = OFF_Q + Q_LORA
OFF_KR = OFF_KV + KV_LORA
OFF_GC = OFF_KR + LANES
OFF_GM = OFF_GC + D_MODEL
IN_COLS_PAD = OFF_GM + D_MODEL

TM_FFN = 512
FFN_GROUP = MXU_DIM
FF_SPLIT = 6 * MXU_DIM
FF_CHUNKS = ((0, FF_SPLIT), (FF_SPLIT, D_FF))
TM_MIX = 512
TQ = 512
TK = 512
HEAD_GROUP = 4
VT_BLK = MXU_DIM
VT_PER_TK = TK // VT_BLK
V_ROWS = V_DIM + BF16_SUBLANES
LOG2E = 1.4426950408889634
CONV_HALO = F32_SUBLANES
VMEM_LIMIT = 56 * 1024 * 1024

F32 = jnp.float32
BF16 = jnp.bfloat16


def _rmsnorm(x, g):
    return x * lax.rsqrt(jnp.mean(x * x, axis=-1, keepdims=True) + EPS) * g


def _dot(a, b):
    return jnp.dot(a, b, preferred_element_type=F32)


def _dot_nt(a, b):
    return lax.dot_general(a, b, (((1,), (1,)), ((), ())), preferred_element_type=F32)


def _const_spec(shape, layer):
    zeros = (0,) * len(shape)
    return pl.BlockSpec((None,) + tuple(shape), lambda *_: (layer,) + zeros,
                        pipeline_mode=pl.Buffered(1))


def _ffn_kernel(*refs, with_mix, with_ple, with_final):
    refs = list(refs)
    o_ref = refs.pop()
    x_ref = refs.pop(0)
    groups = [slice(r * FFN_GROUP, (r + 1) * FFN_GROUP) for r in range(TM_FFN // FFN_GROUP)]
    xs = [x_ref[g, :] for g in groups]
    if with_mix:
        attn_ref, gc_ref, sg_ref, wmo_ref, wo_ref = refs[:5]
        del refs[:5]
        y_mlas = [_dot(attn_ref[g, :], wmo_ref[...]) for g in groups]
        merged = [(gc_ref[g, :] + sg_ref[g, :] * y).astype(BF16) for g, y in zip(groups, y_mlas)]
        xs = [x + _dot(m, wo_ref[...]) for x, m in zip(xs, merged)]
    g_ref, wgu_ref, wd_ref = refs[:3]
    del refs[:3]
    xns = [_rmsnorm(x, g_ref[...]).astype(BF16) for x in xs]
    ys = None
    for c0, c1 in (FF_CHUNKS if with_mix else ((0, D_FF),)):
        gs = [_dot(xn, wgu_ref[:, c0:c1]) for xn in xns]
        us = [_dot(xn, wgu_ref[:, D_FF + c0:D_FF + c1]) for xn in xns]
        acts = [(g * jax.nn.sigmoid(g) * u).astype(BF16) for g, u in zip(gs, us)]
        parts = [_dot(a, wd_ref[c0:c1, :]) for a in acts]
        ys = parts if ys is None else [y + part for y, part in zip(ys, parts)]
    hs = [x + 0.5 * y for x, y in zip(xs, ys)]
    if with_ple:
        p_ref, pg_ref, wpg_ref, wpp_ref = refs[:4]
        del refs[:4]
        hns = [_rmsnorm(h, pg_ref[...]).astype(BF16) for h in hs]
        gates = [jax.nn.sigmoid(_dot(hn, wpg_ref[...])) for hn in hns]
        hs = [h + gate * _dot(p_ref[g, :].astype(BF16), wpp_ref[...])
              for h, gate, g in zip(hs, gates, groups)]
    if with_final:
        hs = [_rmsnorm(h, refs[0][...]) for h in hs]
    for g, h in zip(groups, hs):
        o_ref[g, :] = h


def _ffn(h, layer, norm, w_gu, w_down, mix=None, ple=None, final_norm=None):
    tile = pl.BlockSpec((TM_FFN, D_MODEL), lambda i: (i, 0))
    in_specs, args = [tile], [h]
    if mix is not None:
        attn, gc, sg, w_mla_out, w_o = mix
        in_specs += [tile, tile, tile,
                     _const_spec((D_MODEL, D_MODEL), layer), _const_spec((D_MODEL, D_MODEL), layer)]
        args += [attn, gc, sg, w_mla_out, w_o]
    in_specs += [_const_spec((1, D_MODEL), layer),
                 _const_spec((D_MODEL, 2 * D_FF), layer), _const_spec((D_FF, D_MODEL), layer)]
    args += [norm, w_gu, w_down]
    if ple is not None:
        p, ple_norm, w_pg, w_pp = ple
        in_specs += [pl.BlockSpec((None, TM_FFN, PLE_DIM), lambda i: (layer, i, 0)),
                     _const_spec((1, D_MODEL), layer),
                     _const_spec((D_MODEL, D_MODEL), layer), _const_spec((PLE_DIM, D_MODEL), layer)]
        args += [p, ple_norm, w_pg, w_pp]
    if final_norm is not None:
        in_specs.append(pl.BlockSpec((1, D_MODEL), lambda i: (0, 0)))
        args.append(final_norm)
    return pl.pallas_call(
        functools.partial(_ffn_kernel, with_mix=mix is not None, with_ple=ple is not None,
                          with_final=final_norm is not None),
        grid=(TOKENS // TM_FFN,),
        in_specs=in_specs,
        out_specs=tile,
        out_shape=jax.ShapeDtypeStruct((TOKENS, D_MODEL), F32),
        compiler_params=pltpu.CompilerParams(dimension_semantics=("arbitrary",),
                                             vmem_limit_bytes=VMEM_LIMIT),
        name="mix_out_ffn_ple" if mix is not None else "ffn",
    )(*args)


def _rope_t(x1, x2, cos_c, sin_c):
    return x1 * cos_c - x2 * sin_c, x2 * cos_c + x1 * sin_c


def _mix_proj_kernel(h_ref, cost_ref, sint_ref, g_ref, win_ref, cw_ref, wco_ref,
                     qg_ref, kvg_ref, wuqt_ref, wukv_ref,
                     q_ref, k_ref, vt_ref, gc_ref, sg_ref, z_ref):
    groups = [slice(r * VT_BLK, (r + 1) * VT_BLK) for r in range(TM_MIX // VT_BLK)]

    @pl.when(pl.program_id(1) == 0)
    def _():
        z_ref[0:CONV_HALO, :] = jnp.zeros((CONV_HALO, CONV_DIM), F32)

    us = [_rmsnorm(h_ref[g, :], g_ref[...]).astype(BF16) for g in groups]
    projs = [_dot(u, win_ref[...]) for u in us]
    cw = cw_ref[...]
    rope0, half_rope = N_HEADS * NOPE_DIM, ROPE_DIM // 2
    ones_rows = (lax.broadcasted_iota(jnp.int32, (V_ROWS - V_DIM, VT_BLK), 0) == 0).astype(BF16)
    for r, (g, proj) in enumerate(zip(groups, projs)):
        z0 = CONV_HALO + r * VT_BLK
        z = proj[:, OFF_C:OFF_C + CONV_DIM] * proj[:, OFF_V:OFF_V + CONV_DIM]
        z_ref[z0:z0 + VT_BLK, :] = z
        y = (cw[0:1] * z_ref[z0 - 2:z0 - 2 + VT_BLK, :] + cw[1:2] * z_ref[z0 - 1:z0 - 1 + VT_BLK, :]
             + cw[2:3] * z)
        y_conv = _dot((proj[:, OFF_B:OFF_B + CONV_DIM] * y).astype(BF16), wco_ref[...])
        gc_ref[g, :] = (jax.nn.sigmoid(proj[:, OFF_GC:OFF_GC + D_MODEL]) * y_conv).astype(BF16)
        sg_ref[g, :] = jax.nn.sigmoid(proj[:, OFF_GM:OFF_GM + D_MODEL]).astype(BF16)

        qn = _rmsnorm(proj[:, OFF_Q:OFF_Q + Q_LORA], qg_ref[...]).astype(BF16)
        q_t = _dot_nt(wuqt_ref[...], qn) * (ATTN_SCALE * LOG2E)
        kvn = _rmsnorm(proj[:, OFF_KV:OFF_KV + KV_LORA], kvg_ref[...]).astype(BF16)
        kv = _dot(kvn, wukv_ref[...])
        cos_c, sin_c = cost_ref[:, g], sint_ref[:, g]
        kr_t = proj[:, OFF_KR:OFF_KR + LANES].T
        r1, r2 = _rope_t(kr_t[0:half_rope], kr_t[half_rope:ROPE_DIM], cos_c, sin_c)
        pad = jnp.zeros((LANES - ROPE_DIM, VT_BLK), F32)
        k_rope = jnp.concatenate([r1, r2, pad], axis=0).T.astype(BF16)
        for hd in range(N_HEADS):
            q_ref[hd, 0:NOPE_DIM, g] = q_t[hd * NOPE_DIM:(hd + 1) * NOPE_DIM, :].astype(BF16)
            r1, r2 = _rope_t(q_t[rope0 + hd * ROPE_DIM:rope0 + hd * ROPE_DIM + half_rope, :],
                             q_t[rope0 + hd * ROPE_DIM + half_rope:rope0 + (hd + 1) * ROPE_DIM, :],
                             cos_c, sin_c)
            q_ref[hd, NOPE_DIM:NOPE_DIM + half_rope, g] = r1.astype(BF16)
            q_ref[hd, NOPE_DIM + half_rope:QK_DIM, g] = r2.astype(BF16)
            q_ref[hd, QK_DIM:QK_PAD, g] = jnp.zeros((QK_PAD - QK_DIM, VT_BLK), BF16)
            kv0 = hd * (NOPE_DIM + V_DIM)
            k_ref[hd, g, 0:LANES] = kv[:, kv0:kv0 + NOPE_DIM].astype(BF16)
            k_ref[hd, g, LANES:QK_PAD] = k_rope
            vt_ref[hd, r, 0:V_DIM, :] = kv[:, kv0 + NOPE_DIM:kv0 + NOPE_DIM + V_DIM].T.astype(BF16)
            vt_ref[hd, r, V_DIM:V_ROWS, :] = ones_rows
    z_ref[0:CONV_HALO, :] = z_ref[TM_MIX:TM_MIX + CONV_HALO, :]


def _mix_proj(h, layer, rope, norm, w_in, conv_w, w_conv_out, q_norm, kv_norm, w_uqt, w_ukv):
    n_t = SEQ // TM_MIX
    tok = lambda w: pl.BlockSpec((None, TM_MIX, w), lambda b, i: (b, i, 0))
    tok_t = lambda w: pl.BlockSpec((None, w, TM_MIX), lambda b, i: (b, 0, i))
    return pl.pallas_call(
        _mix_proj_kernel,
        grid=(BATCH, n_t),
        in_specs=[tok(D_MODEL), tok_t(ROPE_DIM // 2), tok_t(ROPE_DIM // 2),
                  _const_spec((1, D_MODEL), layer), _const_spec((D_MODEL, IN_COLS_PAD), layer),
                  _const_spec((CONV_K, CONV_DIM), layer), _const_spec((CONV_DIM, D_MODEL), layer),
                  _const_spec((1, Q_LORA), layer), _const_spec((1, KV_LORA), layer),
                  _const_spec((N_HEADS * QK_DIM, Q_LORA), layer),
                  _const_spec((KV_LORA, N_HEADS * (NOPE_DIM + V_DIM)), layer)],
        out_specs=[pl.BlockSpec((None, N_HEADS, QK_PAD, TM_MIX), lambda b, i: (b, 0, 0, i)),
                   pl.BlockSpec((None, N_HEADS, TM_MIX, QK_PAD), lambda b, i: (b, 0, i, 0)),
                   pl.BlockSpec((None, N_HEADS, TM_MIX // VT_BLK, V_ROWS, VT_BLK),
                                lambda b, i: (b, 0, i, 0, 0)),
                   tok(D_MODEL), tok(D_MODEL)],
        out_shape=[jax.ShapeDtypeStruct((BATCH, N_HEADS, QK_PAD, SEQ), BF16),
                   jax.ShapeDtypeStruct((BATCH, N_HEADS, SEQ, QK_PAD), BF16),
                   jax.ShapeDtypeStruct((BATCH, N_HEADS, SEQ // VT_BLK, V_ROWS, VT_BLK), BF16),
                   jax.ShapeDtypeStruct((BATCH, SEQ, D_MODEL), BF16),
                   jax.ShapeDtypeStruct((BATCH, SEQ, D_MODEL), BF16)],
        scratch_shapes=[pltpu.VMEM((TM_MIX + CONV_HALO, CONV_DIM), F32)],
        compiler_params=pltpu.CompilerParams(dimension_semantics=("arbitrary", "arbitrary"),
                                             vmem_limit_bytes=VMEM_LIMIT),
        name="mix_proj",
    )(h, *rope, norm, w_in, conv_w, w_conv_out, q_norm, kv_norm, w_uqt, w_ukv)


def _attn_kernel(q_ref, qnext_ref, k_ref, vt_ref, o_ref, m_ref, acc_ref, s_ref, tmax_ref):
    n_full = pl.program_id(2)
    half = TQ // 2
    blk_mask = (lax.broadcasted_iota(jnp.int32, (half, half), 0) // CHUNK
                <= lax.broadcasted_iota(jnp.int32, (half, half), 1) // CHUNK)

    heads = range(HEAD_GROUP)
    m_ref[...] = jnp.full(m_ref.shape, -1e30, F32)
    acc_ref[...] = jnp.zeros(acc_ref.shape, F32)

    def scores(hd, j, queries=q_ref):
        k0 = pl.multiple_of(j * TK, TK)
        s = _dot(k_ref[hd, pl.ds(k0, TK), :], queries[hd])
        s_ref[hd] = s
        tmax_ref[hd] = jnp.max(s, axis=0, keepdims=True)

    def softmax(hd, s, tmax, cols=slice(None)):
        m = m_ref[hd, :, cols]
        m_new = jnp.maximum(m, tmax)
        alpha = jnp.exp2(m - m_new)
        m_ref[hd, :, cols] = m_new
        return jnp.exp2((s - m_new).astype(BF16)), alpha

    def weighted_values(hd, j, p, alpha, cols=slice(None)):
        pv = _dot(vt_ref[hd, j * VT_PER_TK], p[0:VT_BLK])
        for c in range(1, p.shape[0] // VT_BLK):
            pv += _dot(vt_ref[hd, j * VT_PER_TK + c], p[c * VT_BLK:(c + 1) * VT_BLK])
        acc_ref[hd, :, cols] = alpha * acc_ref[hd, :, cols] + pv

    @pl.when(n_full == 0)
    def _():
        for hd in heads:
            scores(hd, 0)

    def pipelined_step(j):
        for hd in heads:
            s, tmax = s_ref[hd], tmax_ref[hd]
            scores(hd, j + 1)
            p, alpha = softmax(hd, s, tmax)
            weighted_values(hd, j, p, alpha)

    @pl.loop(0, n_full // 2)
    def _(t):
        pipelined_step(2 * t)
        pipelined_step(2 * t + 1)

    @pl.when(n_full % 2 == 1)
    def _():
        pipelined_step(n_full - 1)

    lo, hi = slice(0, half), slice(half, TQ)

    def diagonal_tile(prefetch_next):
        for hd in heads:
            s_lo = jnp.where(blk_mask, s_ref[hd, lo, lo], -1e30)
            s_hi = jnp.concatenate([s_ref[hd, lo, hi], jnp.where(blk_mask, s_ref[hd, hi, hi], -1e30)],
                                   axis=0)
            if prefetch_next:
                scores(hd, 0, qnext_ref)
            p, alpha = softmax(hd, s_lo, jnp.max(s_lo, axis=0, keepdims=True), lo)
            weighted_values(hd, n_full, p, alpha, lo)
            p, alpha = softmax(hd, s_hi, jnp.max(s_hi, axis=0, keepdims=True), hi)
            weighted_values(hd, n_full, p, alpha, hi)

    last = pl.num_programs(2) - 1
    pl.when(n_full < last)(lambda: diagonal_tile(True))
    pl.when(n_full == last)(lambda: diagonal_tile(False))
    for hd in heads:
        out = acc_ref[hd, 0:V_DIM, :] / acc_ref[hd, V_DIM:V_DIM + 1, :]
        o_ref[:, hd * V_DIM:(hd + 1) * V_DIM] = out.T.astype(BF16)


def _attention(q, k, vt):
    assert TQ == TK and (TQ // 2) % VT_BLK == 0 and (TQ // 2) % CHUNK == 0
    return pl.pallas_call(
        _attn_kernel,
        grid=(BATCH, N_HEADS // HEAD_GROUP, SEQ // TQ),
        in_specs=[pl.BlockSpec((None, HEAD_GROUP, QK_PAD, TQ), lambda b, g, i: (b, g, 0, i)),
                  pl.BlockSpec((None, HEAD_GROUP, QK_PAD, TQ),
                               lambda b, g, i: (b, g, 0, jnp.minimum(i + 1, SEQ // TQ - 1))),
                  pl.BlockSpec((None, HEAD_GROUP, SEQ, QK_PAD), lambda b, g, i: (b, g, 0, 0)),
                  pl.BlockSpec((None, HEAD_GROUP, SEQ // VT_BLK, V_ROWS, VT_BLK),
                               lambda b, g, i: (b, g, 0, 0, 0))],
        out_specs=pl.BlockSpec((None, TQ, HEAD_GROUP * V_DIM), lambda b, g, i: (b, i, g)),
        out_shape=jax.ShapeDtypeStruct((BATCH, SEQ, N_HEADS * V_DIM), BF16),
        scratch_shapes=[pltpu.VMEM((HEAD_GROUP, 1, TQ), F32),
                        pltpu.VMEM((HEAD_GROUP, V_ROWS, TQ), F32),
                        pltpu.VMEM((HEAD_GROUP, TK, TQ), F32), pltpu.VMEM((HEAD_GROUP, 1, TQ), F32)],
        compiler_params=pltpu.CompilerParams(dimension_semantics=("arbitrary",) * 3,
                                             vmem_limit_bytes=VMEM_LIMIT),
        name="attention",
    )(q, q, k, vt)


def _relayout_w_in_kernel(w_ref, o_ref):
    w = w_ref[...]
    k_rot = w[:, OFF_KR:OFF_KR + ROPE_DIM]
    o_ref[:, 0:OFF_KR] = w[:, 0:OFF_KR].astype(BF16)
    o_ref[:, OFF_KR:OFF_GC] = jnp.concatenate([k_rot, k_rot], axis=1).astype(BF16)
    o_ref[:, OFF_GC:IN_COLS_PAD] = w[:, OFF_KR + ROPE_DIM:].astype(BF16)


def _relayout_w_in(w_in):
    rows = MXU_DIM
    return pl.pallas_call(
        _relayout_w_in_kernel,
        grid=(DEPTH, D_MODEL // rows),
        in_specs=[pl.BlockSpec((None, rows, w_in.shape[-1]), lambda l, i: (l, i, 0))],
        out_specs=pl.BlockSpec((None, rows, IN_COLS_PAD), lambda l, i: (l, i, 0)),
        out_shape=jax.ShapeDtypeStruct((DEPTH, D_MODEL, IN_COLS_PAD), BF16),
        compiler_params=pltpu.CompilerParams(dimension_semantics=("arbitrary", "arbitrary")),
        name="relayout_w_in",
    )(w_in)


def _rope_tables(positions):
    inv_freq = ROPE_THETA ** (-jnp.arange(0, ROPE_DIM, 2, dtype=F32) / ROPE_DIM)
    ang = positions.astype(F32)[:, None, :] * inv_freq[None, :, None]
    return jnp.cos(ang), jnp.sin(ang)


def kernel(x, p, positions, ffn1_norm, ffn1_w_gu, ffn1_w_down, mix_norm, w_in, conv_w, w_conv_out,
           q_norm, kv_norm, w_uq, w_ukv, w_mla_out, w_o, ffn2_norm, ffn2_w_gu, ffn2_w_down,
           ple_norm, w_ple_gate, w_ple_proj, final_norm):
    assert x.shape == (BATCH, SEQ, D_MODEL) and p.shape == (DEPTH, BATCH, SEQ, PLE_DIM)
    bf = lambda w: w.astype(BF16)
    row = lambda g: g.reshape(DEPTH, 1, -1)

    w_in_r = _relayout_w_in(w_in)
    w_uq_h = w_uq.reshape(DEPTH, Q_LORA, N_HEADS, QK_DIM)
    w_uq_t = bf(jnp.swapaxes(jnp.concatenate([w_uq_h[..., :NOPE_DIM].reshape(DEPTH, Q_LORA, -1),
                                              w_uq_h[..., NOPE_DIM:].reshape(DEPTH, Q_LORA, -1)],
                                             axis=-1), 1, 2))
    w_ukv_b = bf(w_ukv)
    ffn1_gu, ffn1_dn, ffn2_gu, ffn2_dn = bf(ffn1_w_gu), bf(ffn1_w_down), bf(ffn2_w_gu), bf(ffn2_w_down)
    w_co, w_mo, w_ob, w_pg, w_pp = bf(w_conv_out), bf(w_mla_out), bf(w_o), bf(w_ple_gate), bf(w_ple_proj)
    rope = _rope_tables(positions)
    p2 = p.reshape(DEPTH, TOKENS, PLE_DIM)

    h = x.reshape(TOKENS, D_MODEL)
    for layer in range(DEPTH):
        h = _ffn(h, layer, row(ffn1_norm), ffn1_gu, ffn1_dn)
        q, k, vt, gc, sg = _mix_proj(h.reshape(BATCH, SEQ, D_MODEL), layer, rope, row(mix_norm),
                                     w_in_r, conv_w, w_co, row(q_norm), row(kv_norm), w_uq_t, w_ukv_b)
        o = _attention(q, k, vt)
        h = _ffn(h, layer, row(ffn2_norm), ffn2_gu, ffn2_dn,
                 mix=(o.reshape(TOKENS, D_MODEL), gc.reshape(TOKENS, D_MODEL),
                      sg.reshape(TOKENS, D_MODEL), w_mo, w_ob),
                 ple=(p2, row(ple_norm), w_pg, w_pp),
                 final_norm=final_norm.reshape(1, D_MODEL) if layer == DEPTH - 1 else None)
    return h.reshape(BATCH, SEQ, D_MODEL)
```

```python
import functools

import jax
import jax.numpy as jnp
from jax import lax
from jax.experimental import pallas as pl
from jax.experimental.pallas import tpu as pltpu

D_MODEL = 1024
BATCH = 8
SEQ = 4096
DEPTH = 4
CHUNK = 64
PLE_DIM = 256
D_FF = 2816
CONV_DIM = 512
CONV_K = 3
N_HEADS = 8
NOPE_DIM = 128
ROPE_DIM = 64
V_DIM = 128
Q_LORA = 384
KV_LORA = 256
ROPE_THETA = 10000.0
EPS = 1e-6
QK_DIM = NOPE_DIM + ROPE_DIM
ATTN_SCALE = QK_DIM ** -0.5
TOKENS = BATCH * SEQ

LANES = 128
F32_SUBLANES = 8
BF16_SUBLANES = 16
MXU_DIM = 256
QK_PAD = MXU_DIM
OFF_B, OFF_C, OFF_V = 0, CONV_DIM, 2 * CONV_DIM
OFF_Q = 3 * CONV_DIM
OFF_KV = OFF_Q + Q_LORA
OFF_KR = OFF_KV + KV_LORA
OFF_GC = OFF_KR + LANES
OFF_GM = OFF_GC + D_MODEL
IN_COLS_PAD = OFF_GM + D_MODEL

TM_FFN = 512
FFN_GROUP = MXU_DIM
FF_SPLIT = 6 * MXU_DIM
FF_CHUNKS = ((0, FF_SPLIT), (FF_SPLIT, D_FF))
TM_MIX = 512
TQ = 512
TK = 512
HEAD_GROUP = 4
Q_PER_STEP = 2
VT_BLK = MXU_DIM
VT_PER_TK = TK // VT_BLK
V_ROWS = V_DIM + BF16_SUBLANES
LOG2E = 1.4426950408889634
CONV_HALO = F32_SUBLANES
VMEM_LIMIT = 56 * 1024 * 1024

F32 = jnp.float32
BF16 = jnp.bfloat16


def _rmsnorm(x, g):
    return x * lax.rsqrt(jnp.mean(x * x, axis=-1, keepdims=True) + EPS) * g


def _dot(a, b):
    return jnp.dot(a, b, preferred_element_type=F32)


def _dot_nt(a, b):
    return lax.dot_general(a, b, (((1,), (1,)), ((), ())), preferred_element_type=F32)


def _const_spec(shape, layer):
    zeros = (0,) * len(shape)
    return pl.BlockSpec((None,) + tuple(shape), lambda *_: (layer,) + zeros,
                        pipeline_mode=pl.Buffered(1))


def _ffn_kernel(*refs, with_mix, with_ple, with_final):
    refs = list(refs)
    o_ref = refs.pop()
    x_ref = refs.pop(0)
    groups = [slice(r * FFN_GROUP, (r + 1) * FFN_GROUP) for r in range(TM_FFN // FFN_GROUP)]
    xs = [x_ref[g, :] for g in groups]
    if with_mix:
        attn_ref, gc_ref, sg_ref, wmo_ref, wo_ref = refs[:5]
        del refs[:5]
        y_mlas = [_dot(attn_ref[g, :], wmo_ref[...]) for g in groups]
        merged = [(gc_ref[g, :] + sg_ref[g, :] * y).astype(BF16) for g, y in zip(groups, y_mlas)]
        xs = [x + _dot(m, wo_ref[...]) for x, m in zip(xs, merged)]
    g_ref, wgu_ref, wd_ref = refs[:3]
    del refs[:3]
    xns = [_rmsnorm(x, g_ref[...]).astype(BF16) for x in xs]
    ys = None
    for c0, c1 in (FF_CHUNKS if with_mix else ((0, D_FF),)):
        gs = [_dot(xn, wgu_ref[:, c0:c1]) for xn in xns]
        us = [_dot(xn, wgu_ref[:, D_FF + c0:D_FF + c1]) for xn in xns]
        acts = [(g * jax.nn.sigmoid(g) * u).astype(BF16) for g, u in zip(gs, us)]
        parts = [_dot(a, wd_ref[c0:c1, :]) for a in acts]
        ys = parts if ys is None else [y + part for y, part in zip(ys, parts)]
    hs = [x + 0.5 * y for x, y in zip(xs, ys)]
    if with_ple:
        p_ref, pg_ref, wpg_ref, wpp_ref = refs[:4]
        del refs[:4]
        hns = [_rmsnorm(h, pg_ref[...]).astype(BF16) for h in hs]
        gates = [jax.nn.sigmoid(_dot(hn, wpg_ref[...])) for hn in hns]
        hs = [h + gate * _dot(p_ref[g, :].astype(BF16), wpp_ref[...])
              for h, gate, g in zip(hs, gates, groups)]
    if with_final:
        hs = [_rmsnorm(h, refs[0][...]) for h in hs]
    for g, h in zip(groups, hs):
        o_ref[g, :] = h


def _ffn(h, layer, norm, w_gu, w_down, mix=None, ple=None, final_norm=None):
    tile = pl.BlockSpec((TM_FFN, D_MODEL), lambda i: (i, 0))
    in_specs, args = [tile], [h]
    if mix is not None:
        attn, gc, sg, w_mla_out, w_o = mix
        in_specs += [tile, tile, tile,
                     _const_spec((D_MODEL, D_MODEL), layer), _const_spec((D_MODEL, D_MODEL), layer)]
        args += [attn, gc, sg, w_mla_out, w_o]
    in_specs += [_const_spec((1, D_MODEL), layer),
                 _const_spec((D_MODEL, 2 * D_FF), layer), _const_spec((D_FF, D_MODEL), layer)]
    args += [norm, w_gu, w_down]
    if ple is not None:
        p, ple_norm, w_pg, w_pp = ple
        in_specs += [pl.BlockSpec((None, TM_FFN, PLE_DIM), lambda i: (layer, i, 0)),
                     _const_spec((1, D_MODEL), layer),
                     _const_spec((D_MODEL, D_MODEL), layer), _const_spec((PLE_DIM, D_MODEL), layer)]
        args += [p, ple_norm, w_pg, w_pp]
    if final_norm is not None:
        in_specs.append(pl.BlockSpec((1, D_MODEL), lambda i: (0, 0)))
        args.append(final_norm)
    return pl.pallas_call(
        functools.partial(_ffn_kernel, with_mix=mix is not None, with_ple=ple is not None,
                          with_final=final_norm is not None),
        grid=(TOKENS // TM_FFN,),
        in_specs=in_specs,
        out_specs=tile,
        out_shape=jax.ShapeDtypeStruct((TOKENS, D_MODEL), F32),
        compiler_params=pltpu.CompilerParams(dimension_semantics=("arbitrary",),
                                             vmem_limit_bytes=VMEM_LIMIT),
        name="mix_out_ffn_ple" if mix is not None else "ffn",
    )(*args)


def _rope_t(x1, x2, cos_c, sin_c):
    return x1 * cos_c - x2 * sin_c, x2 * cos_c + x1 * sin_c


def _mix_proj_kernel(h_ref, cost_ref, sint_ref, g_ref, win_ref, cw_ref, wco_ref,
                     qg_ref, kvg_ref, wuqt_ref, wukv_ref,
                     q_ref, k_ref, vt_ref, gc_ref, sg_ref, z_ref):
    groups = [slice(r * VT_BLK, (r + 1) * VT_BLK) for r in range(TM_MIX // VT_BLK)]

    @pl.when(pl.program_id(1) == 0)
    def _():
        z_ref[0:CONV_HALO, :] = jnp.zeros((CONV_HALO, CONV_DIM), F32)

    us = [_rmsnorm(h_ref[g, :], g_ref[...]).astype(BF16) for g in groups]
    projs = [_dot(u, win_ref[...]) for u in us]
    cw = cw_ref[...]
    rope0, half_rope = N_HEADS * NOPE_DIM, ROPE_DIM // 2
    ones_rows = (lax.broadcasted_iota(jnp.int32, (V_ROWS - V_DIM, VT_BLK), 0) == 0).astype(BF16)
    for r, (g, proj) in enumerate(zip(groups, projs)):
        z0 = CONV_HALO + r * VT_BLK
        z = proj[:, OFF_C:OFF_C + CONV_DIM] * proj[:, OFF_V:OFF_V + CONV_DIM]
        z_ref[z0:z0 + VT_BLK, :] = z
        y = (cw[0:1] * z_ref[z0 - 2:z0 - 2 + VT_BLK, :] + cw[1:2] * z_ref[z0 - 1:z0 - 1 + VT_BLK, :]
             + cw[2:3] * z)
        y_conv = _dot((proj[:, OFF_B:OFF_B + CONV_DIM] * y).astype(BF16), wco_ref[...])
        gc_ref[g, :] = (jax.nn.sigmoid(proj[:, OFF_GC:OFF_GC + D_MODEL]) * y_conv).astype(BF16)
        sg_ref[g, :] = jax.nn.sigmoid(proj[:, OFF_GM:OFF_GM + D_MODEL]).astype(BF16)

        qn = _rmsnorm(proj[:, OFF_Q:OFF_Q + Q_LORA], qg_ref[...]).astype(BF16)
        q_t = _dot_nt(wuqt_ref[...], qn) * (ATTN_SCALE * LOG2E)
        kvn = _rmsnorm(proj[:, OFF_KV:OFF_KV + KV_LORA], kvg_ref[...]).astype(BF16)
        kv = _dot(kvn, wukv_ref[...])
        cos_c, sin_c = cost_ref[:, g], sint_ref[:, g]
        kr_t = proj[:, OFF_KR:OFF_KR + LANES].T
        r1, r2 = _rope_t(kr_t[0:half_rope], kr_t[half_rope:ROPE_DIM], cos_c, sin_c)
        pad = jnp.zeros((LANES - ROPE_DIM, VT_BLK), F32)
        k_rope = jnp.concatenate([r1, r2, pad], axis=0).T.astype(BF16)
        for hd in range(N_HEADS):
            q_ref[hd, 0:NOPE_DIM, g] = q_t[hd * NOPE_DIM:(hd + 1) * NOPE_DIM, :].astype(BF16)
            r1, r2 = _rope_t(q_t[rope0 + hd * ROPE_DIM:rope0 + hd * ROPE_DIM + half_rope, :],
                             q_t[rope0 + hd * ROPE_DIM + half_rope:rope0 + (hd + 1) * ROPE_DIM, :],
                             cos_c, sin_c)
            q_ref[hd, NOPE_DIM:NOPE_DIM + half_rope, g] = r1.astype(BF16)
            q_ref[hd, NOPE_DIM + half_rope:QK_DIM, g] = r2.astype(BF16)
            q_ref[hd, QK_DIM:QK_PAD, g] = jnp.zeros((QK_PAD - QK_DIM, VT_BLK), BF16)
            kv0 = hd * (NOPE_DIM + V_DIM)
            k_ref[hd, g, 0:LANES] = kv[:, kv0:kv0 + NOPE_DIM].astype(BF16)
            k_ref[hd, g, LANES:QK_PAD] = k_rope
            vt_ref[hd, r, 0:V_DIM, :] = kv[:, kv0 + NOPE_DIM:kv0 + NOPE_DIM + V_DIM].T.astype(BF16)
            vt_ref[hd, r, V_DIM:V_ROWS, :] = ones_rows
    z_ref[0:CONV_HALO, :] = z_ref[TM_MIX:TM_MIX + CONV_HALO, :]


def _mix_proj(h, layer, rope, norm, w_in, conv_w, w_conv_out, q_norm, kv_norm, w_uqt, w_ukv):
    n_t = SEQ // TM_MIX
    tok = lambda w: pl.BlockSpec((None, TM_MIX, w), lambda b, i: (b, i, 0))
    tok_t = lambda w: pl.BlockSpec((None, w, TM_MIX), lambda b, i: (b, 0, i))
    return pl.pallas_call(
        _mix_proj_kernel,
        grid=(BATCH, n_t),
        in_specs=[tok(D_MODEL), tok_t(ROPE_DIM // 2), tok_t(ROPE_DIM // 2),
                  _const_spec((1, D_MODEL), layer), _const_spec((D_MODEL, IN_COLS_PAD), layer),
                  _const_spec((CONV_K, CONV_DIM), layer), _const_spec((CONV_DIM, D_MODEL), layer),
                  _const_spec((1, Q_LORA), layer), _const_spec((1, KV_LORA), layer),
                  _const_spec((N_HEADS * QK_DIM, Q_LORA), layer),
                  _const_spec((KV_LORA, N_HEADS * (NOPE_DIM + V_DIM)), layer)],
        out_specs=[pl.BlockSpec((None, N_HEADS, QK_PAD, TM_MIX), lambda b, i: (b, 0, 0, i)),
                   pl.BlockSpec((None, N_HEADS, TM_MIX, QK_PAD), lambda b, i: (b, 0, i, 0)),
                   pl.BlockSpec((None, N_HEADS, TM_MIX // VT_BLK, V_ROWS, VT_BLK),
                                lambda b, i: (b, 0, i, 0, 0)),
                   tok(D_MODEL), tok(D_MODEL)],
        out_shape=[jax.ShapeDtypeStruct((BATCH, N_HEADS, QK_PAD, SEQ), BF16),
                   jax.ShapeDtypeStruct((BATCH, N_HEADS, SEQ, QK_PAD), BF16),
                   jax.ShapeDtypeStruct((BATCH, N_HEADS, SEQ // VT_BLK, V_ROWS, VT_BLK), BF16),
                   jax.ShapeDtypeStruct((BATCH, SEQ, D_MODEL), BF16),
                   jax.ShapeDtypeStruct((BATCH, SEQ, D_MODEL), BF16)],
        scratch_shapes=[pltpu.VMEM((TM_MIX + CONV_HALO, CONV_DIM), F32)],
        compiler_params=pltpu.CompilerParams(dimension_semantics=("arbitrary", "arbitrary"),
                                             vmem_limit_bytes=VMEM_LIMIT),
        name="mix_proj",
    )(h, *rope, norm, w_in, conv_w, w_conv_out, q_norm, kv_norm, w_uqt, w_ukv)


def _attn_kernel(q_ref, qnext_ref, k_ref, vt_ref, o_ref, m_ref, acc_ref, s_ref, tmax_ref):
    step = pl.program_id(2)
    last_step = pl.num_programs(2) - 1
    half = TQ // 2
    lo, hi = slice(0, half), slice(half, TQ)
    blk_mask = (lax.broadcasted_iota(jnp.int32, (half, half), 0) // CHUNK
                <= lax.broadcasted_iota(jnp.int32, (half, half), 1) // CHUNK)
    heads = range(HEAD_GROUP)

    def scores(hd, j, queries):
        q_src, q_cols = queries
        k0 = pl.multiple_of(j * TK, TK)
        s = _dot(k_ref[hd, pl.ds(k0, TK), :], q_src[hd, :, q_cols])
        s_ref[hd] = s
        tmax_ref[hd] = jnp.max(s, axis=0, keepdims=True)

    def softmax(hd, s, tmax, cols=slice(None)):
        m = m_ref[hd, :, cols]
        m_new = jnp.maximum(m, tmax)
        alpha = jnp.exp2(m - m_new)
        m_ref[hd, :, cols] = m_new
        return jnp.exp2((s - m_new).astype(BF16)), alpha

    def weighted_values(hd, j, p, alpha, cols=slice(None)):
        pv = _dot(vt_ref[hd, j * VT_PER_TK], p[0:VT_BLK])
        for c in range(1, p.shape[0] // VT_BLK):
            pv += _dot(vt_ref[hd, j * VT_PER_TK + c], p[c * VT_BLK:(c + 1) * VT_BLK])
        acc_ref[hd, :, cols] = alpha * acc_ref[hd, :, cols] + pv

    def pipelined_step(j, queries):
        for hd in heads:
            s, tmax = s_ref[hd], tmax_ref[hd]
            scores(hd, j + 1, queries)
            p, alpha = softmax(hd, s, tmax)
            weighted_values(hd, j, p, alpha)

    def diagonal_tile(n_full, next_queries):
        for hd in heads:
            s_lo = jnp.where(blk_mask, s_ref[hd, lo, lo], -1e30)
            s_hi = jnp.concatenate([s_ref[hd, lo, hi], jnp.where(blk_mask, s_ref[hd, hi, hi], -1e30)],
                                   axis=0)
            if next_queries is not None:
                scores(hd, 0, next_queries)
            p, alpha = softmax(hd, s_lo, jnp.max(s_lo, axis=0, keepdims=True), lo)
            weighted_values(hd, n_full, p, alpha, lo)
            p, alpha = softmax(hd, s_hi, jnp.max(s_hi, axis=0, keepdims=True), hi)
            weighted_values(hd, n_full, p, alpha, hi)

    tiles = [(q_ref, slice(t * TQ, (t + 1) * TQ)) for t in range(Q_PER_STEP)]

    @pl.when(step == 0)
    def _():
        for hd in heads:
            scores(hd, 0, tiles[0])

    for t, queries in enumerate(tiles):
        n_full = Q_PER_STEP * step + t
        m_ref[...] = jnp.full(m_ref.shape, -1e30, F32)
        acc_ref[...] = jnp.zeros(acc_ref.shape, F32)

        @pl.loop(0, n_full // 2)
        def _(pair):
            pipelined_step(2 * pair, queries)
            pipelined_step(2 * pair + 1, queries)

        if t % 2 == 1:
            pipelined_step(n_full - 1, queries)

        if t + 1 < Q_PER_STEP:
            diagonal_tile(n_full, tiles[t + 1])
        else:
            following = (qnext_ref, slice(0, TQ))
            pl.when(step < last_step)(functools.partial(diagonal_tile, n_full, following))
            pl.when(step == last_step)(functools.partial(diagonal_tile, n_full, None))
        for hd in heads:
            out = acc_ref[hd, 0:V_DIM, :] / acc_ref[hd, V_DIM:V_DIM + 1, :]
            o_ref[queries[1], hd * V_DIM:(hd + 1) * V_DIM] = out.T.astype(BF16)


def _attention(q, k, vt):
    assert TQ == TK and (TQ // 2) % VT_BLK == 0 and (TQ // 2) % CHUNK == 0 and Q_PER_STEP % 2 == 0
    tq_step = Q_PER_STEP * TQ
    n_steps = SEQ // tq_step
    return pl.pallas_call(
        _attn_kernel,
        grid=(BATCH, N_HEADS // HEAD_GROUP, n_steps),
        in_specs=[pl.BlockSpec((None, HEAD_GROUP, QK_PAD, tq_step), lambda b, g, i: (b, g, 0, i)),
                  pl.BlockSpec((None, HEAD_GROUP, QK_PAD, tq_step),
                               lambda b, g, i: (b, g, 0, jnp.minimum(i + 1, n_steps - 1))),
                  pl.BlockSpec((None, HEAD_GROUP, SEQ, QK_PAD), lambda b, g, i: (b, g, 0, 0)),
                  pl.BlockSpec((None, HEAD_GROUP, SEQ // VT_BLK, V_ROWS, VT_BLK),
                               lambda b, g, i: (b, g, 0, 0, 0))],
        out_specs=pl.BlockSpec((None, tq_step, HEAD_GROUP * V_DIM), lambda b, g, i: (b, i, g)),
        out_shape=jax.ShapeDtypeStruct((BATCH, SEQ, N_HEADS * V_DIM), BF16),
        scratch_shapes=[pltpu.VMEM((HEAD_GROUP, 1, TQ), F32),
                        pltpu.VMEM((HEAD_GROUP, V_ROWS, TQ), F32),
                        pltpu.VMEM((HEAD_GROUP, TK, TQ), F32), pltpu.VMEM((HEAD_GROUP, 1, TQ), F32)],
        compiler_params=pltpu.CompilerParams(dimension_semantics=("arbitrary",) * 3,
                                             vmem_limit_bytes=VMEM_LIMIT),
        name="attention",
    )(q, q, k, vt)


def _rope_tables(positions):
    inv_freq = ROPE_THETA ** (-jnp.arange(0, ROPE_DIM, 2, dtype=F32) / ROPE_DIM)
    ang = positions.astype(F32)[:, None, :] * inv_freq[None, :, None]
    return jnp.cos(ang), jnp.sin(ang)


def kernel(x, p, positions, ffn1_norm, ffn1_w_gu, ffn1_w_down, mix_norm, w_in, conv_w, w_conv_out,
           q_norm, kv_norm, w_uq, w_ukv, w_mla_out, w_o, ffn2_norm, ffn2_w_gu, ffn2_w_down,
           ple_norm, w_ple_gate, w_ple_proj, final_norm):
    assert x.shape == (BATCH, SEQ, D_MODEL) and p.shape == (DEPTH, BATCH, SEQ, PLE_DIM)
    bf = lambda w: w.astype(BF16)
    row = lambda g: g.reshape(DEPTH, 1, -1)

    w_in_b = bf(w_in)
    k_rot = w_in_b[:, :, OFF_KR:OFF_KR + ROPE_DIM]
    w_in_r = jnp.concatenate([w_in_b[:, :, :OFF_KR], k_rot, k_rot, w_in_b[:, :, OFF_KR + ROPE_DIM:]], axis=-1)
    w_uq_h = w_uq.reshape(DEPTH, Q_LORA, N_HEADS, QK_DIM)
    w_uq_t = bf(jnp.swapaxes(jnp.concatenate([w_uq_h[..., :NOPE_DIM].reshape(DEPTH, Q_LORA, -1),
                                              w_uq_h[..., NOPE_DIM:].reshape(DEPTH, Q_LORA, -1)],
                                             axis=-1), 1, 2))
    w_ukv_b = bf(w_ukv)
    ffn1_gu, ffn1_dn, ffn2_gu, ffn2_dn = bf(ffn1_w_gu), bf(ffn1_w_down), bf(ffn2_w_gu), bf(ffn2_w_down)
    w_co, w_mo, w_ob, w_pg, w_pp = bf(w_conv_out), bf(w_mla_out), bf(w_o), bf(w_ple_gate), bf(w_ple_proj)
    rope = _rope_tables(positions)
    p2 = p.reshape(DEPTH, TOKENS, PLE_DIM)

    h = x.reshape(TOKENS, D_MODEL)
    for layer in range(DEPTH):
        h = _ffn(h, layer, row(ffn1_norm), ffn1_gu, ffn1_dn)
        q, k, vt, gc, sg = _mix_proj(h.reshape(BATCH, SEQ, D_MODEL), layer, rope, row(mix_norm),
                                     w_in_r, conv_w, w_co, row(q_norm), row(kv_norm), w_uq_t, w_ukv_b)
        o = _attention(q, k, vt)
        h = _ffn(h, layer, row(ffn2_norm), ffn2_gu, ffn2_dn,
                 mix=(o.reshape(TOKENS, D_MODEL), gc.reshape(TOKENS, D_MODEL),
                      sg.reshape(TOKENS, D_MODEL), w_mo, w_ob),
                 ple=(p2, row(ple_norm), w_pg, w_pp),
                 final_norm=final_norm.reshape(1, D_MODEL) if layer == DEPTH - 1 else None)
    return h.reshape(BATCH, SEQ, D_MODEL)
```

```python
import functools

import jax
import jax.numpy as jnp
from jax import lax
from jax.experimental import pallas as pl
from jax.experimental.pallas import tpu as pltpu

D_MODEL = 1024
BATCH = 8
SEQ = 4096
DEPTH = 4
CHUNK = 64
PLE_DIM = 256
D_FF = 2816
CONV_DIM = 512
CONV_K = 3
N_HEADS = 8
NOPE_DIM = 128
ROPE_DIM = 64
V_DIM = 128
Q_LORA = 384
KV_LORA = 256
ROPE_THETA = 10000.0
EPS = 1e-6
QK_DIM = NOPE_DIM + ROPE_DIM
ATTN_SCALE = QK_DIM ** -0.5
TOKENS = BATCH * SEQ

LANES = 128
F32_SUBLANES = 8
BF16_SUBLANES = 16
MXU_DIM = 256
QK_PAD = MXU_DIM
OFF_B, OFF_C, OFF_V = 0, CONV_DIM, 2 * CONV_DIM
OFF_Q = 3 * CONV_DIM
OFF_KV = OFF_Q + Q_LORA
OFF_KR = OFF_KV + KV_LORA
OFF_GC = OFF_KR + LANES
OFF_GM = OFF_GC + D_MODEL
IN_COLS_PAD = OFF_GM + D_MODEL

TM_FFN = 512
FFN_GROUP = MXU_DIM
FF_SPLIT = 6 * MXU_DIM
FF_CHUNKS = ((0, FF_SPLIT), (FF_SPLIT, D_FF))
TM_MIX = 512
TQ = 512
TK = 512
HEAD_GROUP = 4
Q_PER_STEP = 4
VT_BLK = MXU_DIM
VT_PER_TK = TK // VT_BLK
V_ROWS = V_DIM + BF16_SUBLANES
LOG2E = 1.4426950408889634
CONV_HALO = F32_SUBLANES
VMEM_LIMIT = 56 * 1024 * 1024

F32 = jnp.float32
BF16 = jnp.bfloat16


def _rmsnorm(x, g):
    return x * lax.rsqrt(jnp.mean(x * x, axis=-1, keepdims=True) + EPS) * g


def _dot(a, b):
    return jnp.dot(a, b, preferred_element_type=F32)


def _dot_nt(a, b):
    return lax.dot_general(a, b, (((1,), (1,)), ((), ())), preferred_element_type=F32)


def _const_spec(shape, layer):
    zeros = (0,) * len(shape)
    return pl.BlockSpec((None,) + tuple(shape), lambda *_: (layer,) + zeros,
                        pipeline_mode=pl.Buffered(1))


def _ffn_kernel(*refs, with_mix, with_ple, with_final):
    refs = list(refs)
    o_ref = refs.pop()
    x_ref = refs.pop(0)
    groups = [slice(r * FFN_GROUP, (r + 1) * FFN_GROUP) for r in range(TM_FFN // FFN_GROUP)]
    xs = [x_ref[g, :] for g in groups]
    if with_mix:
        attn_ref, gc_ref, sg_ref, wmo_ref, wo_ref = refs[:5]
        del refs[:5]
        y_mlas = [_dot(attn_ref[g, :], wmo_ref[...]) for g in groups]
        merged = [(gc_ref[g, :] + sg_ref[g, :] * y).astype(BF16) for g, y in zip(groups, y_mlas)]
        xs = [x + _dot(m, wo_ref[...]) for x, m in zip(xs, merged)]
    g_ref, wgu_ref, wd_ref = refs[:3]
    del refs[:3]
    xns = [_rmsnorm(x, g_ref[...]).astype(BF16) for x in xs]
    ys = None
    for c0, c1 in (FF_CHUNKS if with_mix else ((0, D_FF),)):
        gs = [_dot(xn, wgu_ref[:, c0:c1]) for xn in xns]
        us = [_dot(xn, wgu_ref[:, D_FF + c0:D_FF + c1]) for xn in xns]
        acts = [(g * jax.nn.sigmoid(g) * u).astype(BF16) for g, u in zip(gs, us)]
        parts = [_dot(a, wd_ref[c0:c1, :]) for a in acts]
        ys = parts if ys is None else [y + part for y, part in zip(ys, parts)]
    hs = [x + 0.5 * y for x, y in zip(xs, ys)]
    if with_ple:
        p_ref, pg_ref, wpg_ref, wpp_ref = refs[:4]
        del refs[:4]
        hns = [_rmsnorm(h, pg_ref[...]).astype(BF16) for h in hs]
        gates = [jax.nn.sigmoid(_dot(hn, wpg_ref[...])) for hn in hns]
        hs = [h + gate * _dot(p_ref[g, :].astype(BF16), wpp_ref[...])
              for h, gate, g in zip(hs, gates, groups)]
    if with_final:
        hs = [_rmsnorm(h, refs[0][...]) for h in hs]
    for g, h in zip(groups, hs):
        o_ref[g, :] = h


def _ffn(h, layer, norm, w_gu, w_down, mix=None, ple=None, final_norm=None):
    tile = pl.BlockSpec((TM_FFN, D_MODEL), lambda i: (i, 0))
    in_specs, args = [tile], [h]
    if mix is not None:
        attn, gc, sg, w_mla_out, w_o = mix
        in_specs += [tile, tile, tile,
                     _const_spec((D_MODEL, D_MODEL), layer), _const_spec((D_MODEL, D_MODEL), layer)]
        args += [attn, gc, sg, w_mla_out, w_o]
    in_specs += [_const_spec((1, D_MODEL), layer),
                 _const_spec((D_MODEL, 2 * D_FF), layer), _const_spec((D_FF, D_MODEL), layer)]
    args += [norm, w_gu, w_down]
    if ple is not None:
        p, ple_norm, w_pg, w_pp = ple
        in_specs += [pl.BlockSpec((None, TM_FFN, PLE_DIM), lambda i: (layer, i, 0)),
                     _const_spec((1, D_MODEL), layer),
                     _const_spec((D_MODEL, D_MODEL), layer), _const_spec((PLE_DIM, D_MODEL), layer)]
        args += [p, ple_norm, w_pg, w_pp]
    if final_norm is not None:
        in_specs.append(pl.BlockSpec((1, D_MODEL), lambda i: (0, 0)))
        args.append(final_norm)
    return pl.pallas_call(
        functools.partial(_ffn_kernel, with_mix=mix is not None, with_ple=ple is not None,
                          with_final=final_norm is not None),
        grid=(TOKENS // TM_FFN,),
        in_specs=in_specs,
        out_specs=tile,
        out_shape=jax.ShapeDtypeStruct((TOKENS, D_MODEL), F32),
        compiler_params=pltpu.CompilerParams(dimension_semantics=("arbitrary",),
                                             vmem_limit_bytes=VMEM_LIMIT),
        name="mix_out_ffn_ple" if mix is not None else "ffn",
    )(*args)


def _rope_t(x1, x2, cos_c, sin_c):
    return x1 * cos_c - x2 * sin_c, x2 * cos_c + x1 * sin_c


def _mix_proj_kernel(h_ref, cost_ref, sint_ref, g_ref, win_ref, cw_ref, wco_ref,
                     qg_ref, kvg_ref, wuqt_ref, wukv_ref,
                     q_ref, k_ref, vt_ref, gc_ref, sg_ref, z_ref):
    groups = [slice(r * VT_BLK, (r + 1) * VT_BLK) for r in range(TM_MIX // VT_BLK)]

    @pl.when(pl.program_id(1) == 0)
    def _():
        z_ref[0:CONV_HALO, :] = jnp.zeros((CONV_HALO, CONV_DIM), F32)

    us = [_rmsnorm(h_ref[g, :], g_ref[...]).astype(BF16) for g in groups]
    projs = [_dot(u, win_ref[...]) for u in us]
    cw = cw_ref[...]
    rope0, half_rope = N_HEADS * NOPE_DIM, ROPE_DIM // 2
    ones_rows = (lax.broadcasted_iota(jnp.int32, (V_ROWS - V_DIM, VT_BLK), 0) == 0).astype(BF16)
    for r, (g, proj) in enumerate(zip(groups, projs)):
        z0 = CONV_HALO + r * VT_BLK
        z = proj[:, OFF_C:OFF_C + CONV_DIM] * proj[:, OFF_V:OFF_V + CONV_DIM]
        z_ref[z0:z0 + VT_BLK, :] = z
        y = (cw[0:1] * z_ref[z0 - 2:z0 - 2 + VT_BLK, :] + cw[1:2] * z_ref[z0 - 1:z0 - 1 + VT_BLK, :]
             + cw[2:3] * z)
        y_conv = _dot((proj[:, OFF_B:OFF_B + CONV_DIM] * y).astype(BF16), wco_ref[...])
        gc_ref[g, :] = (jax.nn.sigmoid(proj[:, OFF_GC:OFF_GC + D_MODEL]) * y_conv).astype(BF16)
        sg_ref[g, :] = jax.nn.sigmoid(proj[:, OFF_GM:OFF_GM + D_MODEL]).astype(BF16)

        qn = _rmsnorm(proj[:, OFF_Q:OFF_Q + Q_LORA], qg_ref[...]).astype(BF16)
        q_t = _dot_nt(wuqt_ref[...], qn) * (ATTN_SCALE * LOG2E)
        kvn = _rmsnorm(proj[:, OFF_KV:OFF_KV + KV_LORA], kvg_ref[...]).astype(BF16)
        kv = _dot(kvn, wukv_ref[...])
        cos_c, sin_c = cost_ref[:, g], sint_ref[:, g]
        kr_t = proj[:, OFF_KR:OFF_KR + LANES].T
        r1, r2 = _rope_t(kr_t[0:half_rope], kr_t[half_rope:ROPE_DIM], cos_c, sin_c)
        pad = jnp.zeros((LANES - ROPE_DIM, VT_BLK), F32)
        k_rope = jnp.concatenate([r1, r2, pad], axis=0).T.astype(BF16)
        for hd in range(N_HEADS):
            q_ref[hd, 0:NOPE_DIM, g] = q_t[hd * NOPE_DIM:(hd + 1) * NOPE_DIM, :].astype(BF16)
            r1, r2 = _rope_t(q_t[rope0 + hd * ROPE_DIM:rope0 + hd * ROPE_DIM + half_rope, :],
                             q_t[rope0 + hd * ROPE_DIM + half_rope:rope0 + (hd + 1) * ROPE_DIM, :],
                             cos_c, sin_c)
            q_ref[hd, NOPE_DIM:NOPE_DIM + half_rope, g] = r1.astype(BF16)
            q_ref[hd, NOPE_DIM + half_rope:QK_DIM, g] = r2.astype(BF16)
            q_ref[hd, QK_DIM:QK_PAD, g] = jnp.zeros((QK_PAD - QK_DIM, VT_BLK), BF16)
            kv0 = hd * (NOPE_DIM + V_DIM)
            k_ref[hd, g, 0:LANES] = kv[:, kv0:kv0 + NOPE_DIM].astype(BF16)
            k_ref[hd, g, LANES:QK_PAD] = k_rope
            vt_ref[hd, r, 0:V_DIM, :] = kv[:, kv0 + NOPE_DIM:kv0 + NOPE_DIM + V_DIM].T.astype(BF16)
            vt_ref[hd, r, V_DIM:V_ROWS, :] = ones_rows
    z_ref[0:CONV_HALO, :] = z_ref[TM_MIX:TM_MIX + CONV_HALO, :]


def _mix_proj(h, layer, rope, norm, w_in, conv_w, w_conv_out, q_norm, kv_norm, w_uqt, w_ukv):
    n_t = SEQ // TM_MIX
    tok = lambda w: pl.BlockSpec((None, TM_MIX, w), lambda b, i: (b, i, 0))
    tok_t = lambda w: pl.BlockSpec((None, w, TM_MIX), lambda b, i: (b, 0, i))
    return pl.pallas_call(
        _mix_proj_kernel,
        grid=(BATCH, n_t),
        in_specs=[tok(D_MODEL), tok_t(ROPE_DIM // 2), tok_t(ROPE_DIM // 2),
                  _const_spec((1, D_MODEL), layer), _const_spec((D_MODEL, IN_COLS_PAD), layer),
                  _const_spec((CONV_K, CONV_DIM), layer), _const_spec((CONV_DIM, D_MODEL), layer),
                  _const_spec((1, Q_LORA), layer), _const_spec((1, KV_LORA), layer),
                  _const_spec((N_HEADS * QK_DIM, Q_LORA), layer),
                  _const_spec((KV_LORA, N_HEADS * (NOPE_DIM + V_DIM)), layer)],
        out_specs=[pl.BlockSpec((None, N_HEADS, QK_PAD, TM_MIX), lambda b, i: (b, 0, 0, i)),
                   pl.BlockSpec((None, N_HEADS, TM_MIX, QK_PAD), lambda b, i: (b, 0, i, 0)),
                   pl.BlockSpec((None, N_HEADS, TM_MIX // VT_BLK, V_ROWS, VT_BLK),
                                lambda b, i: (b, 0, i, 0, 0)),
                   tok(D_MODEL), tok(D_MODEL)],
        out_shape=[jax.ShapeDtypeStruct((BATCH, N_HEADS, QK_PAD, SEQ), BF16),
                   jax.ShapeDtypeStruct((BATCH, N_HEADS, SEQ, QK_PAD), BF16),
                   jax.ShapeDtypeStruct((BATCH, N_HEADS, SEQ // VT_BLK, V_ROWS, VT_BLK), BF16),
                   jax.ShapeDtypeStruct((BATCH, SEQ, D_MODEL), BF16),
                   jax.ShapeDtypeStruct((BATCH, SEQ, D_MODEL), BF16)],
        scratch_shapes=[pltpu.VMEM((TM_MIX + CONV_HALO, CONV_DIM), F32)],
        compiler_params=pltpu.CompilerParams(dimension_semantics=("arbitrary", "arbitrary"),
                                             vmem_limit_bytes=VMEM_LIMIT),
        name="mix_proj",
    )(h, *rope, norm, w_in, conv_w, w_conv_out, q_norm, kv_norm, w_uqt, w_ukv)


def _attn_kernel(q_ref, qnext_ref, k_ref, vt_ref, o_ref, m_ref, acc_ref, s_ref, tmax_ref):
    step = pl.program_id(2)
    last_step = pl.num_programs(2) - 1
    half = TQ // 2
    lo, hi = slice(0, half), slice(half, TQ)
    blk_mask = (lax.broadcasted_iota(jnp.int32, (half, half), 0) // CHUNK
                <= lax.broadcasted_iota(jnp.int32, (half, half), 1) // CHUNK)
    heads = range(HEAD_GROUP)

    def scores(hd, j, queries):
        q_src, q_cols = queries
        k0 = pl.multiple_of(j * TK, TK)
        s = _dot(k_ref[hd, pl.ds(k0, TK), :], q_src[hd, :, q_cols])
        s_ref[hd] = s
        tmax_ref[hd] = jnp.max(s, axis=0, keepdims=True)

    def softmax(hd, s, tmax, cols=slice(None)):
        m = m_ref[hd, :, cols]
        m_new = jnp.maximum(m, tmax)
        alpha = jnp.exp2(m - m_new)
        m_ref[hd, :, cols] = m_new
        return jnp.exp2((s - m_new).astype(BF16)), alpha

    def weighted_values(hd, j, p, alpha, cols=slice(None)):
        pv = _dot(vt_ref[hd, j * VT_PER_TK], p[0:VT_BLK])
        for c in range(1, p.shape[0] // VT_BLK):
            pv += _dot(vt_ref[hd, j * VT_PER_TK + c], p[c * VT_BLK:(c + 1) * VT_BLK])
        acc_ref[hd, :, cols] = alpha * acc_ref[hd, :, cols] + pv

    def pipelined_step(j, queries):
        for hd in heads:
            s, tmax = s_ref[hd], tmax_ref[hd]
            scores(hd, j + 1, queries)
            p, alpha = softmax(hd, s, tmax)
            weighted_values(hd, j, p, alpha)

    def diagonal_tile(n_full, next_queries):
        for hd in heads:
            s_lo = jnp.where(blk_mask, s_ref[hd, lo, lo], -1e30)
            s_hi = jnp.concatenate([s_ref[hd, lo, hi], jnp.where(blk_mask, s_ref[hd, hi, hi], -1e30)],
                                   axis=0)
            if next_queries is not None:
                scores(hd, 0, next_queries)
            p, alpha = softmax(hd, s_lo, jnp.max(s_lo, axis=0, keepdims=True), lo)
            weighted_values(hd, n_full, p, alpha, lo)
            p, alpha = softmax(hd, s_hi, jnp.max(s_hi, axis=0, keepdims=True), hi)
            weighted_values(hd, n_full, p, alpha, hi)

    tiles = [(q_ref, slice(t * TQ, (t + 1) * TQ)) for t in range(Q_PER_STEP)]

    @pl.when(step == 0)
    def _():
        for hd in heads:
            scores(hd, 0, tiles[0])

    for t, queries in enumerate(tiles):
        n_full = Q_PER_STEP * step + t
        m_ref[...] = jnp.full(m_ref.shape, -1e30, F32)
        acc_ref[...] = jnp.zeros(acc_ref.shape, F32)

        @pl.loop(0, n_full // 2)
        def _(pair):
            pipelined_step(2 * pair, queries)
            pipelined_step(2 * pair + 1, queries)

        if t % 2 == 1:
            pipelined_step(n_full - 1, queries)

        if t + 1 < Q_PER_STEP:
            diagonal_tile(n_full, tiles[t + 1])
        else:
            following = (qnext_ref, slice(0, TQ))
            pl.when(step < last_step)(functools.partial(diagonal_tile, n_full, following))
            pl.when(step == last_step)(functools.partial(diagonal_tile, n_full, None))
        for hd in heads:
            out = acc_ref[hd, 0:V_DIM, :] / acc_ref[hd, V_DIM:V_DIM + 1, :]
            o_ref[queries[1], hd * V_DIM:(hd + 1) * V_DIM] = out.T.astype(BF16)


def _attention(q, k, vt):
    assert TQ == TK and (TQ // 2) % VT_BLK == 0 and (TQ // 2) % CHUNK == 0 and Q_PER_STEP % 2 == 0
    tq_step = Q_PER_STEP * TQ
    n_steps = SEQ // tq_step
    return pl.pallas_call(
        _attn_kernel,
        grid=(BATCH, N_HEADS // HEAD_GROUP, n_steps),
        in_specs=[pl.BlockSpec((None, HEAD_GROUP, QK_PAD, tq_step), lambda b, g, i: (b, g, 0, i)),
                  pl.BlockSpec((None, HEAD_GROUP, QK_PAD, tq_step),
                               lambda b, g, i: (b, g, 0, jnp.minimum(i + 1, n_steps - 1))),
                  pl.BlockSpec((None, HEAD_GROUP, SEQ, QK_PAD), lambda b, g, i: (b, g, 0, 0)),
                  pl.BlockSpec((None, HEAD_GROUP, SEQ // VT_BLK, V_ROWS, VT_BLK),
                               lambda b, g, i: (b, g, 0, 0, 0))],
        out_specs=pl.BlockSpec((None, tq_step, HEAD_GROUP * V_DIM), lambda b, g, i: (b, i, g)),
        out_shape=jax.ShapeDtypeStruct((BATCH, SEQ, N_HEADS * V_DIM), BF16),
        scratch_shapes=[pltpu.VMEM((HEAD_GROUP, 1, TQ), F32),
                        pltpu.VMEM((HEAD_GROUP, V_ROWS, TQ), F32),
                        pltpu.VMEM((HEAD_GROUP, TK, TQ), F32), pltpu.VMEM((HEAD_GROUP, 1, TQ), F32)],
        compiler_params=pltpu.CompilerParams(dimension_semantics=("arbitrary",) * 3,
                                             vmem_limit_bytes=VMEM_LIMIT),
        name="attention",
    )(q, q, k, vt)


def _rope_tables(positions):
    inv_freq = ROPE_THETA ** (-jnp.arange(0, ROPE_DIM, 2, dtype=F32) / ROPE_DIM)
    ang = positions.astype(F32)[:, None, :] * inv_freq[None, :, None]
    return jnp.cos(ang), jnp.sin(ang)


def kernel(x, p, positions, ffn1_norm, ffn1_w_gu, ffn1_w_down, mix_norm, w_in, conv_w, w_conv_out,
           q_norm, kv_norm, w_uq, w_ukv, w_mla_out, w_o, ffn2_norm, ffn2_w_gu, ffn2_w_down,
           ple_norm, w_ple_gate, w_ple_proj, final_norm):
    assert x.shape == (BATCH, SEQ, D_MODEL) and p.shape == (DEPTH, BATCH, SEQ, PLE_DIM)
    bf = lambda w: w.astype(BF16)
    row = lambda g: g.reshape(DEPTH, 1, -1)

    w_in_b = bf(w_in)
    k_rot = w_in_b[:, :, OFF_KR:OFF_KR + ROPE_DIM]
    w_in_r = jnp.concatenate([w_in_b[:, :, :OFF_KR], k_rot, k_rot, w_in_b[:, :, OFF_KR + ROPE_DIM:]], axis=-1)
    w_uq_h = w_uq.reshape(DEPTH, Q_LORA, N_HEADS, QK_DIM)
    w_uq_t = bf(jnp.swapaxes(jnp.concatenate([w_uq_h[..., :NOPE_DIM].reshape(DEPTH, Q_LORA, -1),
                                              w_uq_h[..., NOPE_DIM:].reshape(DEPTH, Q_LORA, -1)],
                                             axis=-1), 1, 2))
    w_ukv_b = bf(w_ukv)
    ffn1_gu, ffn1_dn, ffn2_gu, ffn2_dn = bf(ffn1_w_gu), bf(ffn1_w_down), bf(ffn2_w_gu), bf(ffn2_w_down)
    w_co, w_mo, w_ob, w_pg, w_pp = bf(w_conv_out), bf(w_mla_out), bf(w_o), bf(w_ple_gate), bf(w_ple_proj)
    rope = _rope_tables(positions)
    p2 = p.reshape(DEPTH, TOKENS, PLE_DIM)

    h = x.reshape(TOKENS, D_MODEL)
    for layer in range(DEPTH):
        h = _ffn(h, layer, row(ffn1_norm), ffn1_gu, ffn1_dn)
        q, k, vt, gc, sg = _mix_proj(h.reshape(BATCH, SEQ, D_MODEL), layer, rope, row(mix_norm),
                                     w_in_r, conv_w, w_co, row(q_norm), row(kv_norm), w_uq_t, w_ukv_b)
        o = _attention(q, k, vt)
        h = _ffn(h, layer, row(ffn2_norm), ffn2_gu, ffn2_dn,
                 mix=(o.reshape(TOKENS, D_MODEL), gc.reshape(TOKENS, D_MODEL),
                      sg.reshape(TOKENS, D_MODEL), w_mo, w_ob),
                 ple=(p2, row(ple_norm), w_pg, w_pp),
                 final_norm=final_norm.reshape(1, D_MODEL) if layer == DEPTH - 1 else None)
    return h.reshape(BATCH, SEQ, D_MODEL)
```

```python
import functools

import jax
import jax.numpy as jnp
from jax import lax
from jax.experimental import pallas as pl
from jax.experimental.pallas import tpu as pltpu

D_MODEL = 1024
BATCH = 8
SEQ = 4096
DEPTH = 4
CHUNK = 64
PLE_DIM = 256
D_FF = 2816
CONV_DIM = 512
CONV_K = 3
N_HEADS = 8
NOPE_DIM = 128
ROPE_DIM = 64
V_DIM = 128
Q_LORA = 384
KV_LORA = 256
ROPE_THETA = 10000.0
EPS = 1e-6
QK_DIM = NOPE_DIM + ROPE_DIM
ATTN_SCALE = QK_DIM ** -0.5
TOKENS = BATCH * SEQ

LANES = 128
F32_SUBLANES = 8
BF16_SUBLANES = 16
MXU_DIM = 256
QK_PAD = MXU_DIM
OFF_B, OFF_C, OFF_V = 0, CONV_DIM, 2 * CONV_DIM
OFF_Q = 3 * CONV_DIM
OFF_KV = OFF_Q + Q_LORA
OFF_KR = OFF_KV + KV_LORA
OFF_GC = OFF_KR + LANES
OFF_GM = OFF_GC + D_MODEL
IN_COLS_PAD = OFF_GM + D_MODEL

TM_FFN = 512
FFN_GROUP = MXU_DIM
FF_SPLIT = 6 * MXU_DIM
FF_CHUNKS = ((0, FF_SPLIT), (FF_SPLIT, D_FF))
TM_MIX = 512
TQ = 512
TK = 512
HEAD_GROUP = 4
Q_PER_STEP = 4
VT_BLK = MXU_DIM
VT_PER_TK = TK // VT_BLK
V_ROWS = V_DIM + BF16_SUBLANES
LOG2E = 1.4426950408889634
CONV_HALO = F32_SUBLANES
VMEM_LIMIT = 56 * 1024 * 1024

F32 = jnp.float32
BF16 = jnp.bfloat16


def _rmsnorm(x, g):
    return x * lax.rsqrt(jnp.mean(x * x, axis=-1, keepdims=True) + EPS) * g


def _dot(a, b):
    return jnp.dot(a, b, preferred_element_type=F32)


def _dot_nt(a, b):
    return lax.dot_general(a, b, (((1,), (1,)), ((), ())), preferred_element_type=F32)


def _const_spec(shape, layer):
    zeros = (0,) * len(shape)
    return pl.BlockSpec((None,) + tuple(shape), lambda *_: (layer,) + zeros,
                        pipeline_mode=pl.Buffered(1))


def _ffn_kernel(*refs, with_mix, with_ple, with_final):
    refs = list(refs)
    o_ref = refs.pop()
    x_ref = refs.pop(0)
    groups = [slice(r * FFN_GROUP, (r + 1) * FFN_GROUP) for r in range(TM_FFN // FFN_GROUP)]
    xs = [x_ref[g, :] for g in groups]
    if with_mix:
        attn_ref, gc_ref, sg_ref, wmo_ref, wo_ref = refs[:5]
        del refs[:5]
        y_mlas = [lax.dot_general(attn_ref[:, g], wmo_ref[...], (((0,), (0,)), ((), ())),
                                  preferred_element_type=F32) for g in groups]
        merged = [(gc_ref[g, :] + sg_ref[g, :] * y).astype(BF16) for g, y in zip(groups, y_mlas)]
        xs = [x + _dot(m, wo_ref[...]) for x, m in zip(xs, merged)]
    g_ref, wgu_ref, wd_ref = refs[:3]
    del refs[:3]
    xns = [_rmsnorm(x, g_ref[...]).astype(BF16) for x in xs]
    ys = None
    for c0, c1 in (FF_CHUNKS if with_mix else ((0, D_FF),)):
        gs = [_dot(xn, wgu_ref[:, c0:c1]) for xn in xns]
        us = [_dot(xn, wgu_ref[:, D_FF + c0:D_FF + c1]) for xn in xns]
        acts = [(g * jax.nn.sigmoid(g) * u).astype(BF16) for g, u in zip(gs, us)]
        parts = [_dot(a, wd_ref[c0:c1, :]) for a in acts]
        ys = parts if ys is None else [y + part for y, part in zip(ys, parts)]
    hs = [x + 0.5 * y for x, y in zip(xs, ys)]
    if with_ple:
        p_ref, pg_ref, wpg_ref, wpp_ref = refs[:4]
        del refs[:4]
        hns = [_rmsnorm(h, pg_ref[...]).astype(BF16) for h in hs]
        gates = [jax.nn.sigmoid(_dot(hn, wpg_ref[...])) for hn in hns]
        hs = [h + gate * _dot(p_ref[g, :].astype(BF16), wpp_ref[...])
              for h, gate, g in zip(hs, gates, groups)]
    if with_final:
        hs = [_rmsnorm(h, refs[0][...]) for h in hs]
    for g, h in zip(groups, hs):
        o_ref[g, :] = h


def _ffn(h, layer, norm, w_gu, w_down, mix=None, ple=None, final_norm=None):
    tile = pl.BlockSpec((TM_FFN, D_MODEL), lambda i: (i, 0))
    in_specs, args = [tile], [h]
    if mix is not None:
        attn, gc, sg, w_mla_out, w_o = mix
        tiles_per_seq = SEQ // TM_FFN
        attn_tile = pl.BlockSpec((None, D_MODEL, TM_FFN),
                                 lambda i: (i // tiles_per_seq, 0, i % tiles_per_seq))
        in_specs += [attn_tile, tile, tile,
                     _const_spec((D_MODEL, D_MODEL), layer), _const_spec((D_MODEL, D_MODEL), layer)]
        args += [attn, gc, sg, w_mla_out, w_o]
    in_specs += [_const_spec((1, D_MODEL), layer),
                 _const_spec((D_MODEL, 2 * D_FF), layer), _const_spec((D_FF, D_MODEL), layer)]
    args += [norm, w_gu, w_down]
    if ple is not None:
        p, ple_norm, w_pg, w_pp = ple
        in_specs += [pl.BlockSpec((None, TM_FFN, PLE_DIM), lambda i: (layer, i, 0)),
                     _const_spec((1, D_MODEL), layer),
                     _const_spec((D_MODEL, D_MODEL), layer), _const_spec((PLE_DIM, D_MODEL), layer)]
        args += [p, ple_norm, w_pg, w_pp]
    if final_norm is not None:
        in_specs.append(pl.BlockSpec((1, D_MODEL), lambda i: (0, 0)))
        args.append(final_norm)
    return pl.pallas_call(
        functools.partial(_ffn_kernel, with_mix=mix is not None, with_ple=ple is not None,
                          with_final=final_norm is not None),
        grid=(TOKENS // TM_FFN,),
        in_specs=in_specs,
        out_specs=tile,
        out_shape=jax.ShapeDtypeStruct((TOKENS, D_MODEL), F32),
        compiler_params=pltpu.CompilerParams(dimension_semantics=("arbitrary",),
                                             vmem_limit_bytes=VMEM_LIMIT),
        name="mix_out_ffn_ple" if mix is not None else "ffn",
    )(*args)


def _rope_t(x1, x2, cos_c, sin_c):
    return x1 * cos_c - x2 * sin_c, x2 * cos_c + x1 * sin_c


def _mix_proj_kernel(h_ref, cost_ref, sint_ref, g_ref, win_ref, cw_ref, wco_ref,
                     qg_ref, kvg_ref, wuqt_ref, wukv_ref,
                     q_ref, k_ref, vt_ref, gc_ref, sg_ref, z_ref):
    groups = [slice(r * VT_BLK, (r + 1) * VT_BLK) for r in range(TM_MIX // VT_BLK)]

    @pl.when(pl.program_id(1) == 0)
    def _():
        z_ref[0:CONV_HALO, :] = jnp.zeros((CONV_HALO, CONV_DIM), F32)

    us = [_rmsnorm(h_ref[g, :], g_ref[...]).astype(BF16) for g in groups]
    projs = [_dot(u, win_ref[...]) for u in us]
    cw = cw_ref[...]
    rope0, half_rope = N_HEADS * NOPE_DIM, ROPE_DIM // 2
    ones_rows = (lax.broadcasted_iota(jnp.int32, (V_ROWS - V_DIM, VT_BLK), 0) == 0).astype(BF16)
    for r, (g, proj) in enumerate(zip(groups, projs)):
        z0 = CONV_HALO + r * VT_BLK
        z = proj[:, OFF_C:OFF_C + CONV_DIM] * proj[:, OFF_V:OFF_V + CONV_DIM]
        z_ref[z0:z0 + VT_BLK, :] = z
        y = (cw[0:1] * z_ref[z0 - 2:z0 - 2 + VT_BLK, :] + cw[1:2] * z_ref[z0 - 1:z0 - 1 + VT_BLK, :]
             + cw[2:3] * z)
        y_conv = _dot((proj[:, OFF_B:OFF_B + CONV_DIM] * y).astype(BF16), wco_ref[...])
        gc_ref[g, :] = (jax.nn.sigmoid(proj[:, OFF_GC:OFF_GC + D_MODEL]) * y_conv).astype(BF16)
        sg_ref[g, :] = jax.nn.sigmoid(proj[:, OFF_GM:OFF_GM + D_MODEL]).astype(BF16)

        qn = _rmsnorm(proj[:, OFF_Q:OFF_Q + Q_LORA], qg_ref[...]).astype(BF16)
        q_t = _dot_nt(wuqt_ref[...], qn) * (ATTN_SCALE * LOG2E)
        kvn = _rmsnorm(proj[:, OFF_KV:OFF_KV + KV_LORA], kvg_ref[...]).astype(BF16)
        kv = _dot(kvn, wukv_ref[...])
        cos_c, sin_c = cost_ref[:, g], sint_ref[:, g]
        kr_t = proj[:, OFF_KR:OFF_KR + LANES].T
        r1, r2 = _rope_t(kr_t[0:half_rope], kr_t[half_rope:ROPE_DIM], cos_c, sin_c)
        pad = jnp.zeros((LANES - ROPE_DIM, VT_BLK), F32)
        k_rope = jnp.concatenate([r1, r2, pad], axis=0).T.astype(BF16)
        for hd in range(N_HEADS):
            q_ref[hd, 0:NOPE_DIM, g] = q_t[hd * NOPE_DIM:(hd + 1) * NOPE_DIM, :].astype(BF16)
            r1, r2 = _rope_t(q_t[rope0 + hd * ROPE_DIM:rope0 + hd * ROPE_DIM + half_rope, :],
                             q_t[rope0 + hd * ROPE_DIM + half_rope:rope0 + (hd + 1) * ROPE_DIM, :],
                             cos_c, sin_c)
            q_ref[hd, NOPE_DIM:NOPE_DIM + half_rope, g] = r1.astype(BF16)
            q_ref[hd, NOPE_DIM + half_rope:QK_DIM, g] = r2.astype(BF16)
            q_ref[hd, QK_DIM:QK_PAD, g] = jnp.zeros((QK_PAD - QK_DIM, VT_BLK), BF16)
            kv0 = hd * (NOPE_DIM + V_DIM)
            k_ref[hd, g, 0:LANES] = kv[:, kv0:kv0 + NOPE_DIM].astype(BF16)
            k_ref[hd, g, LANES:QK_PAD] = k_rope
            vt_ref[hd, r, 0:V_DIM, :] = kv[:, kv0 + NOPE_DIM:kv0 + NOPE_DIM + V_DIM].T.astype(BF16)
            vt_ref[hd, r, V_DIM:V_ROWS, :] = ones_rows
    z_ref[0:CONV_HALO, :] = z_ref[TM_MIX:TM_MIX + CONV_HALO, :]


def _mix_proj(h, layer, rope, norm, w_in, conv_w, w_conv_out, q_norm, kv_norm, w_uqt, w_ukv):
    n_t = SEQ // TM_MIX
    tok = lambda w: pl.BlockSpec((None, TM_MIX, w), lambda b, i: (b, i, 0))
    tok_t = lambda w: pl.BlockSpec((None, w, TM_MIX), lambda b, i: (b, 0, i))
    return pl.pallas_call(
        _mix_proj_kernel,
        grid=(BATCH, n_t),
        in_specs=[tok(D_MODEL), tok_t(ROPE_DIM // 2), tok_t(ROPE_DIM // 2),
                  _const_spec((1, D_MODEL), layer), _const_spec((D_MODEL, IN_COLS_PAD), layer),
                  _const_spec((CONV_K, CONV_DIM), layer), _const_spec((CONV_DIM, D_MODEL), layer),
                  _const_spec((1, Q_LORA), layer), _const_spec((1, KV_LORA), layer),
                  _const_spec((N_HEADS * QK_DIM, Q_LORA), layer),
                  _const_spec((KV_LORA, N_HEADS * (NOPE_DIM + V_DIM)), layer)],
        out_specs=[pl.BlockSpec((None, N_HEADS, QK_PAD, TM_MIX), lambda b, i: (b, 0, 0, i)),
                   pl.BlockSpec((None, N_HEADS, TM_MIX, QK_PAD), lambda b, i: (b, 0, i, 0)),
                   pl.BlockSpec((None, N_HEADS, TM_MIX // VT_BLK, V_ROWS, VT_BLK),
                                lambda b, i: (b, 0, i, 0, 0)),
                   tok(D_MODEL), tok(D_MODEL)],
        out_shape=[jax.ShapeDtypeStruct((BATCH, N_HEADS, QK_PAD, SEQ), BF16),
                   jax.ShapeDtypeStruct((BATCH, N_HEADS, SEQ, QK_PAD), BF16),
                   jax.ShapeDtypeStruct((BATCH, N_HEADS, SEQ // VT_BLK, V_ROWS, VT_BLK), BF16),
                   jax.ShapeDtypeStruct((BATCH, SEQ, D_MODEL), BF16),
                   jax.ShapeDtypeStruct((BATCH, SEQ, D_MODEL), BF16)],
        scratch_shapes=[pltpu.VMEM((TM_MIX + CONV_HALO, CONV_DIM), F32)],
        compiler_params=pltpu.CompilerParams(dimension_semantics=("arbitrary", "arbitrary"),
                                             vmem_limit_bytes=VMEM_LIMIT),
        name="mix_proj",
    )(h, *rope, norm, w_in, conv_w, w_conv_out, q_norm, kv_norm, w_uqt, w_ukv)


def _attn_kernel(q_ref, qnext_ref, k_ref, vt_ref, o_ref, m_ref, acc_ref, s_ref, tmax_ref):
    step = pl.program_id(2)
    last_step = pl.num_programs(2) - 1
    half = TQ // 2
    lo, hi = slice(0, half), slice(half, TQ)
    blk_mask = (lax.broadcasted_iota(jnp.int32, (half, half), 0) // CHUNK
                <= lax.broadcasted_iota(jnp.int32, (half, half), 1) // CHUNK)
    heads = range(HEAD_GROUP)

    def scores(hd, j, queries):
        q_src, q_cols = queries
        k0 = pl.multiple_of(j * TK, TK)
        s = _dot(k_ref[hd, pl.ds(k0, TK), :], q_src[hd, :, q_cols])
        s_ref[hd] = s
        tmax_ref[hd] = jnp.max(s, axis=0, keepdims=True)

    def softmax(hd, s, tmax, cols=slice(None)):
        m = m_ref[hd, :, cols]
        m_new = jnp.maximum(m, tmax)
        alpha = jnp.exp2(m - m_new)
        m_ref[hd, :, cols] = m_new
        return jnp.exp2((s - m_new).astype(BF16)), alpha

    def weighted_values(hd, j, p, alpha, cols=slice(None)):
        pv = _dot(vt_ref[hd, j * VT_PER_TK], p[0:VT_BLK])
        for c in range(1, p.shape[0] // VT_BLK):
            pv += _dot(vt_ref[hd, j * VT_PER_TK + c], p[c * VT_BLK:(c + 1) * VT_BLK])
        acc_ref[hd, :, cols] = alpha * acc_ref[hd, :, cols] + pv

    def pipelined_step(j, queries):
        for hd in heads:
            s, tmax = s_ref[hd], tmax_ref[hd]
            scores(hd, j + 1, queries)
            p, alpha = softmax(hd, s, tmax)
            weighted_values(hd, j, p, alpha)

    def diagonal_tile(n_full, next_queries):
        for hd in heads:
            s_lo = jnp.where(blk_mask, s_ref[hd, lo, lo], -1e30)
            s_hi = jnp.concatenate([s_ref[hd, lo, hi], jnp.where(blk_mask, s_ref[hd, hi, hi], -1e30)],
                                   axis=0)
            if next_queries is not None:
                scores(hd, 0, next_queries)
            p, alpha = softmax(hd, s_lo, jnp.max(s_lo, axis=0, keepdims=True), lo)
            weighted_values(hd, n_full, p, alpha, lo)
            p, alpha = softmax(hd, s_hi, jnp.max(s_hi, axis=0, keepdims=True), hi)
            weighted_values(hd, n_full, p, alpha, hi)

    tiles = [(q_ref, slice(t * TQ, (t + 1) * TQ)) for t in range(Q_PER_STEP)]

    @pl.when(step == 0)
    def _():
        for hd in heads:
            scores(hd, 0, tiles[0])

    for t, queries in enumerate(tiles):
        n_full = Q_PER_STEP * step + t
        m_ref[...] = jnp.full(m_ref.shape, -1e30, F32)
        acc_ref[...] = jnp.zeros(acc_ref.shape, F32)

        @pl.loop(0, n_full // 2)
        def _(pair):
            pipelined_step(2 * pair, queries)
            pipelined_step(2 * pair + 1, queries)

        if t % 2 == 1:
            pipelined_step(n_full - 1, queries)

        if t + 1 < Q_PER_STEP:
            diagonal_tile(n_full, tiles[t + 1])
        else:
            following = (qnext_ref, slice(0, TQ))
            pl.when(step < last_step)(functools.partial(diagonal_tile, n_full, following))
            pl.when(step == last_step)(functools.partial(diagonal_tile, n_full, None))
        for hd in heads:
            out = acc_ref[hd, 0:V_DIM, :] / acc_ref[hd, V_DIM:V_DIM + 1, :]
            o_ref[hd * V_DIM:(hd + 1) * V_DIM, queries[1]] = out.astype(BF16)


def _attention(q, k, vt):
    assert TQ == TK and (TQ // 2) % VT_BLK == 0 and (TQ // 2) % CHUNK == 0 and Q_PER_STEP % 2 == 0
    tq_step = Q_PER_STEP * TQ
    n_steps = SEQ // tq_step
    return pl.pallas_call(
        _attn_kernel,
        grid=(BATCH, N_HEADS // HEAD_GROUP, n_steps),
        in_specs=[pl.BlockSpec((None, HEAD_GROUP, QK_PAD, tq_step), lambda b, g, i: (b, g, 0, i)),
                  pl.BlockSpec((None, HEAD_GROUP, QK_PAD, tq_step),
                               lambda b, g, i: (b, g, 0, jnp.minimum(i + 1, n_steps - 1))),
                  pl.BlockSpec((None, HEAD_GROUP, SEQ, QK_PAD), lambda b, g, i: (b, g, 0, 0)),
                  pl.BlockSpec((None, HEAD_GROUP, SEQ // VT_BLK, V_ROWS, VT_BLK),
                               lambda b, g, i: (b, g, 0, 0, 0))],
        out_specs=pl.BlockSpec((None, HEAD_GROUP * V_DIM, tq_step), lambda b, g, i: (b, g, i)),
        out_shape=jax.ShapeDtypeStruct((BATCH, N_HEADS * V_DIM, SEQ), BF16),
        scratch_shapes=[pltpu.VMEM((HEAD_GROUP, 1, TQ), F32),
                        pltpu.VMEM((HEAD_GROUP, V_ROWS, TQ), F32),
                        pltpu.VMEM((HEAD_GROUP, TK, TQ), F32), pltpu.VMEM((HEAD_GROUP, 1, TQ), F32)],
        compiler_params=pltpu.CompilerParams(dimension_semantics=("arbitrary",) * 3,
                                             vmem_limit_bytes=VMEM_LIMIT),
        name="attention",
    )(q, q, k, vt)


def _rope_tables(positions):
    inv_freq = ROPE_THETA ** (-jnp.arange(0, ROPE_DIM, 2, dtype=F32) / ROPE_DIM)
    ang = positions.astype(F32)[:, None, :] * inv_freq[None, :, None]
    return jnp.cos(ang), jnp.sin(ang)


def kernel(x, p, positions, ffn1_norm, ffn1_w_gu, ffn1_w_down, mix_norm, w_in, conv_w, w_conv_out,
           q_norm, kv_norm, w_uq, w_ukv, w_mla_out, w_o, ffn2_norm, ffn2_w_gu, ffn2_w_down,
           ple_norm, w_ple_gate, w_ple_proj, final_norm):
    assert x.shape == (BATCH, SEQ, D_MODEL) and p.shape == (DEPTH, BATCH, SEQ, PLE_DIM)
    bf = lambda w: w.astype(BF16)
    row = lambda g: g.reshape(DEPTH, 1, -1)

    w_in_b = bf(w_in)
    k_rot = w_in_b[:, :, OFF_KR:OFF_KR + ROPE_DIM]
    w_in_r = jnp.concatenate([w_in_b[:, :, :OFF_KR], k_rot, k_rot, w_in_b[:, :, OFF_KR + ROPE_DIM:]], axis=-1)
    w_uq_h = w_uq.reshape(DEPTH, Q_LORA, N_HEADS, QK_DIM)
    w_uq_t = bf(jnp.swapaxes(jnp.concatenate([w_uq_h[..., :NOPE_DIM].reshape(DEPTH, Q_LORA, -1),
                                              w_uq_h[..., NOPE_DIM:].reshape(DEPTH, Q_LORA, -1)],
                                             axis=-1), 1, 2))
    w_ukv_b = bf(w_ukv)
    ffn1_gu, ffn1_dn, ffn2_gu, ffn2_dn = bf(ffn1_w_gu), bf(ffn1_w_down), bf(ffn2_w_gu), bf(ffn2_w_down)
    w_co, w_mo, w_ob, w_pg, w_pp = bf(w_conv_out), bf(w_mla_out), bf(w_o), bf(w_ple_gate), bf(w_ple_proj)
    rope = _rope_tables(positions)
    p2 = p.reshape(DEPTH, TOKENS, PLE_DIM)

    h = x.reshape(TOKENS, D_MODEL)
    for layer in range(DEPTH):
        h = _ffn(h, layer, row(ffn1_norm), ffn1_gu, ffn1_dn)
        q, k, vt, gc, sg = _mix_proj(h.reshape(BATCH, SEQ, D_MODEL), layer, rope, row(mix_norm),
                                     w_in_r, conv_w, w_co, row(q_norm), row(kv_norm), w_uq_t, w_ukv_b)
        o = _attention(q, k, vt)
        h = _ffn(h, layer, row(ffn2_norm), ffn2_gu, ffn2_dn,
                 mix=(o, gc.reshape(TOKENS, D_MODEL),
                      sg.reshape(TOKENS, D_MODEL), w_mo, w_ob),
                 ple=(p2, row(ple_norm), w_pg, w_pp),
                 final_norm=final_norm.reshape(1, D_MODEL) if layer == DEPTH - 1 else None)
    return h.reshape(BATCH, SEQ, D_MODEL)
```

```python
import functools

import jax
import jax.numpy as jnp
from jax import lax
from jax.experimental import pallas as pl
from jax.experimental.pallas import tpu as pltpu

D_MODEL = 1024
BATCH = 8
SEQ = 4096
DEPTH = 4
CHUNK = 64
PLE_DIM = 256
D_FF = 2816
CONV_DIM = 512
CONV_K = 3
N_HEADS = 8
NOPE_DIM = 128
ROPE_DIM = 64
V_DIM = 128
Q_LORA = 384
KV_LORA = 256
ROPE_THETA = 10000.0
EPS = 1e-6
QK_DIM = NOPE_DIM + ROPE_DIM
ATTN_SCALE = QK_DIM ** -0.5
TOKENS = BATCH * SEQ

LANES = 128
F32_SUBLANES = 8
BF16_SUBLANES = 16
MXU_DIM = 256
QK_PAD = MXU_DIM
OFF_B, OFF_C, OFF_V = 0, CONV_DIM, 2 * CONV_DIM
OFF_Q = 3 * CONV_DIM
OFF_KV = OFF_Q + Q_LORA
OFF_KR = OFF_KV + KV_LORA
OFF_GC = OFF_KR + LANES
OFF_GM = OFF_GC + D_MODEL
IN_COLS_PAD = OFF_GM + D_MODEL

TM_FFN = 512
FFN_GROUP = MXU_DIM
FF_SPLIT = 6 * MXU_DIM
FF_CHUNKS = ((0, FF_SPLIT), (FF_SPLIT, D_FF))
TM_MIX = 512
TQ = 512
TK = 512
HEAD_GROUP = 4
Q_PER_STEP = 4
VT_BLK = MXU_DIM
VT_PER_TK = TK // VT_BLK
V_ROWS = V_DIM + BF16_SUBLANES
LOG2E = 1.4426950408889634
CONV_HALO = F32_SUBLANES
VMEM_LIMIT = 56 * 1024 * 1024

F32 = jnp.float32
BF16 = jnp.bfloat16


def _rmsnorm(x, g):
    return x * lax.rsqrt(jnp.mean(x * x, axis=-1, keepdims=True) + EPS) * g


def _dot(a, b):
    return jnp.dot(a, b, preferred_element_type=F32)


def _dot_nt(a, b):
    return lax.dot_general(a, b, (((1,), (1,)), ((), ())), preferred_element_type=F32)


def _const_spec(shape, layer):
    zeros = (0,) * len(shape)
    return pl.BlockSpec((None,) + tuple(shape), lambda *_: (layer,) + zeros,
                        pipeline_mode=pl.Buffered(1))


def _ffn_kernel(*refs, with_mix, with_ple, with_final):
    refs = list(refs)
    o_ref = refs.pop()
    x_ref = refs.pop(0)
    groups = [slice(r * FFN_GROUP, (r + 1) * FFN_GROUP) for r in range(TM_FFN // FFN_GROUP)]
    xs = [x_ref[g, :] for g in groups]
    if with_mix:
        attn_ref, gc_ref, sg_ref, wmo_ref, wo_ref = refs[:5]
        del refs[:5]
        y_mlas = [lax.dot_general(attn_ref[:, g], wmo_ref[...], (((0,), (0,)), ((), ())),
                                  preferred_element_type=F32) for g in groups]
        merged = [(gc_ref[g, :] + sg_ref[g, :] * y).astype(BF16) for g, y in zip(groups, y_mlas)]
        xs = [x + _dot(m, wo_ref[...]) for x, m in zip(xs, merged)]
    g_ref, wgu_ref, wd_ref = refs[:3]
    del refs[:3]
    xns = [_rmsnorm(x, g_ref[...]).astype(BF16) for x in xs]
    ys = None
    for c0, c1 in (FF_CHUNKS if with_mix else ((0, D_FF),)):
        gs = [_dot(xn, wgu_ref[:, c0:c1]) for xn in xns]
        us = [_dot(xn, wgu_ref[:, D_FF + c0:D_FF + c1]) for xn in xns]
        acts = [(g * jax.nn.sigmoid(g) * u).astype(BF16) for g, u in zip(gs, us)]
        parts = [_dot(a, wd_ref[c0:c1, :]) for a in acts]
        ys = parts if ys is None else [y + part for y, part in zip(ys, parts)]
    hs = [x + 0.5 * y for x, y in zip(xs, ys)]
    if with_ple:
        p_ref, pg_ref, wpg_ref, wpp_ref = refs[:4]
        del refs[:4]
        hns = [_rmsnorm(h, pg_ref[...]).astype(BF16) for h in hs]
        gates = [jax.nn.sigmoid(_dot(hn, wpg_ref[...])) for hn in hns]
        hs = [h + gate * _dot(p_ref[g, :].astype(BF16), wpp_ref[...])
              for h, gate, g in zip(hs, gates, groups)]
    if with_final:
        hs = [_rmsnorm(h, refs[0][...]) for h in hs]
    for g, h in zip(groups, hs):
        o_ref[g, :] = h


def _ffn(h, layer, norm, w_gu, w_down, mix=None, ple=None, final_norm=None):
    tile = pl.BlockSpec((TM_FFN, D_MODEL), lambda i: (i, 0))
    in_specs, args = [tile], [h]
    if mix is not None:
        attn, gc, sg, w_mla_out, w_o = mix
        tiles_per_seq = SEQ // TM_FFN
        attn_tile = pl.BlockSpec((None, D_MODEL, TM_FFN),
                                 lambda i: (i // tiles_per_seq, 0, i % tiles_per_seq))
        in_specs += [attn_tile, tile, tile,
                     _const_spec((D_MODEL, D_MODEL), layer), _const_spec((D_MODEL, D_MODEL), layer)]
        args += [attn, gc, sg, w_mla_out, w_o]
    in_specs += [_const_spec((1, D_MODEL), layer),
                 _const_spec((D_MODEL, 2 * D_FF), layer), _const_spec((D_FF, D_MODEL), layer)]
    args += [norm, w_gu, w_down]
    if ple is not None:
        p, ple_norm, w_pg, w_pp = ple
        in_specs += [pl.BlockSpec((None, TM_FFN, PLE_DIM), lambda i: (layer, i, 0)),
                     _const_spec((1, D_MODEL), layer),
                     _const_spec((D_MODEL, D_MODEL), layer), _const_spec((PLE_DIM, D_MODEL), layer)]
        args += [p, ple_norm, w_pg, w_pp]
    if final_norm is not None:
        in_specs.append(pl.BlockSpec((1, D_MODEL), lambda i: (0, 0)))
        args.append(final_norm)
    return pl.pallas_call(
        functools.partial(_ffn_kernel, with_mix=mix is not None, with_ple=ple is not None,
                          with_final=final_norm is not None),
        grid=(TOKENS // TM_FFN,),
        in_specs=in_specs,
        out_specs=tile,
        out_shape=jax.ShapeDtypeStruct((TOKENS, D_MODEL), F32),
        compiler_params=pltpu.CompilerParams(dimension_semantics=("arbitrary",),
                                             vmem_limit_bytes=VMEM_LIMIT),
        name="mix_out_ffn_ple" if mix is not None else "ffn",
    )(*args)


def _rope_t(x1, x2, cos_c, sin_c):
    return x1 * cos_c - x2 * sin_c, x2 * cos_c + x1 * sin_c


def _mix_proj_kernel(h_ref, cost_ref, sint_ref, g_ref, win_ref, cw_ref, wco_ref,
                     qg_ref, kvg_ref, wuqt_ref, wukv_ref,
                     q_ref, k_ref, vt_ref, gc_ref, sg_ref, z_ref):
    groups = [slice(r * VT_BLK, (r + 1) * VT_BLK) for r in range(TM_MIX // VT_BLK)]

    @pl.when(pl.program_id(1) == 0)
    def _():
        z_ref[0:CONV_HALO, :] = jnp.zeros((CONV_HALO, CONV_DIM), F32)

    us = [_rmsnorm(h_ref[g, :], g_ref[...]).astype(BF16) for g in groups]
    projs = [_dot(u, win_ref[...]) for u in us]
    cw = cw_ref[...]
    rope0, half_rope = N_HEADS * NOPE_DIM, ROPE_DIM // 2
    ones_rows = (lax.broadcasted_iota(jnp.int32, (V_ROWS - V_DIM, VT_BLK), 0) == 0).astype(BF16)
    for r, (g, proj) in enumerate(zip(groups, projs)):
        z0 = CONV_HALO + r * VT_BLK
        z = proj[:, OFF_C:OFF_C + CONV_DIM] * proj[:, OFF_V:OFF_V + CONV_DIM]
        z_ref[z0:z0 + VT_BLK, :] = z
        y = (cw[0:1] * z_ref[z0 - 2:z0 - 2 + VT_BLK, :] + cw[1:2] * z_ref[z0 - 1:z0 - 1 + VT_BLK, :]
             + cw[2:3] * z)
        y_conv = _dot((proj[:, OFF_B:OFF_B + CONV_DIM] * y).astype(BF16), wco_ref[...])
        gc_ref[g, :] = (jax.nn.sigmoid(proj[:, OFF_GC:OFF_GC + D_MODEL]) * y_conv).astype(BF16)
        sg_ref[g, :] = jax.nn.sigmoid(proj[:, OFF_GM:OFF_GM + D_MODEL]).astype(BF16)

        qn = _rmsnorm(proj[:, OFF_Q:OFF_Q + Q_LORA], qg_ref[...]).astype(BF16)
        q_t = _dot_nt(wuqt_ref[...], qn) * (ATTN_SCALE * LOG2E)
        kvn = _rmsnorm(proj[:, OFF_KV:OFF_KV + KV_LORA], kvg_ref[...]).astype(BF16)
        kv = _dot(kvn, wukv_ref[...])
        cos_c, sin_c = cost_ref[:, g], sint_ref[:, g]
        kr_t = proj[:, OFF_KR:OFF_KR + LANES].T
        r1, r2 = _rope_t(kr_t[0:half_rope], kr_t[half_rope:ROPE_DIM], cos_c, sin_c)
        pad = jnp.zeros((LANES - ROPE_DIM, VT_BLK), F32)
        k_rope = jnp.concatenate([r1, r2, pad], axis=0).T.astype(BF16)
        for hd in range(N_HEADS):
            q_ref[hd, 0:NOPE_DIM, g] = q_t[hd * NOPE_DIM:(hd + 1) * NOPE_DIM, :].astype(BF16)
            r1, r2 = _rope_t(q_t[rope0 + hd * ROPE_DIM:rope0 + hd * ROPE_DIM + half_rope, :],
                             q_t[rope0 + hd * ROPE_DIM + half_rope:rope0 + (hd + 1) * ROPE_DIM, :],
                             cos_c, sin_c)
            q_ref[hd, NOPE_DIM:NOPE_DIM + half_rope, g] = r1.astype(BF16)
            q_ref[hd, NOPE_DIM + half_rope:QK_DIM, g] = r2.astype(BF16)
            q_ref[hd, QK_DIM:QK_PAD, g] = jnp.zeros((QK_PAD - QK_DIM, VT_BLK), BF16)
            kv0 = hd * (NOPE_DIM + V_DIM)
            k_ref[hd, g, 0:LANES] = kv[:, kv0:kv0 + NOPE_DIM].astype(BF16)
            k_ref[hd, g, LANES:QK_PAD] = k_rope
            vt_ref[hd, r, 0:V_DIM, :] = kv[:, kv0 + NOPE_DIM:kv0 + NOPE_DIM + V_DIM].T.astype(BF16)
            vt_ref[hd, r, V_DIM:V_ROWS, :] = ones_rows
    z_ref[0:CONV_HALO, :] = z_ref[TM_MIX:TM_MIX + CONV_HALO, :]


def _mix_proj(h, layer, rope, norm, w_in, conv_w, w_conv_out, q_norm, kv_norm, w_uqt, w_ukv):
    n_t = SEQ // TM_MIX
    tok = lambda w: pl.BlockSpec((None, TM_MIX, w), lambda b, i: (b, i, 0))
    tok_t = lambda w: pl.BlockSpec((None, w, TM_MIX), lambda b, i: (b, 0, i))
    return pl.pallas_call(
        _mix_proj_kernel,
        grid=(BATCH, n_t),
        in_specs=[tok(D_MODEL), tok_t(ROPE_DIM // 2), tok_t(ROPE_DIM // 2),
                  _const_spec((1, D_MODEL), layer), _const_spec((D_MODEL, IN_COLS_PAD), layer),
                  _const_spec((CONV_K, CONV_DIM), layer), _const_spec((CONV_DIM, D_MODEL), layer),
                  _const_spec((1, Q_LORA), layer), _const_spec((1, KV_LORA), layer),
                  _const_spec((N_HEADS * QK_DIM, Q_LORA), layer),
                  _const_spec((KV_LORA, N_HEADS * (NOPE_DIM + V_DIM)), layer)],
        out_specs=[pl.BlockSpec((None, N_HEADS, QK_PAD, TM_MIX), lambda b, i: (b, 0, 0, i)),
                   pl.BlockSpec((None, N_HEADS, TM_MIX, QK_PAD), lambda b, i: (b, 0, i, 0)),
                   pl.BlockSpec((None, N_HEADS, TM_MIX // VT_BLK, V_ROWS, VT_BLK),
                                lambda b, i: (b, 0, i, 0, 0)),
                   tok(D_MODEL), tok(D_MODEL)],
        out_shape=[jax.ShapeDtypeStruct((BATCH, N_HEADS, QK_PAD, SEQ), BF16),
                   jax.ShapeDtypeStruct((BATCH, N_HEADS, SEQ, QK_PAD), BF16),
                   jax.ShapeDtypeStruct((BATCH, N_HEADS, SEQ // VT_BLK, V_ROWS, VT_BLK), BF16),
                   jax.ShapeDtypeStruct((BATCH, SEQ, D_MODEL), BF16),
                   jax.ShapeDtypeStruct((BATCH, SEQ, D_MODEL), BF16)],
        scratch_shapes=[pltpu.VMEM((TM_MIX + CONV_HALO, CONV_DIM), F32)],
        compiler_params=pltpu.CompilerParams(dimension_semantics=("arbitrary", "arbitrary"),
                                             vmem_limit_bytes=VMEM_LIMIT),
        name="mix_proj",
    )(h, *rope, norm, w_in, conv_w, w_conv_out, q_norm, kv_norm, w_uqt, w_ukv)


def _attn_kernel(q_ref, qnext_ref, k_ref, vt_ref, o_ref, m_ref, acc_ref, s_ref, tmax_ref):
    step = pl.program_id(2)
    last_step = pl.num_programs(2) - 1
    half = TQ // 2
    lo, hi = slice(0, half), slice(half, TQ)
    blk_mask = (lax.broadcasted_iota(jnp.int32, (half, half), 0) // CHUNK
                <= lax.broadcasted_iota(jnp.int32, (half, half), 1) // CHUNK)
    heads = range(HEAD_GROUP)

    def scores(hd, j, queries):
        q_src, q_cols = queries
        k0 = pl.multiple_of(j * TK, TK)
        s = _dot(k_ref[hd, pl.ds(k0, TK), :], q_src[hd, :, q_cols])
        s_ref[hd] = s
        tmax_ref[hd] = jnp.max(s, axis=0, keepdims=True)

    def softmax(hd, s, tmax, cols=slice(None)):
        m = m_ref[hd, :, cols]
        m_new = jnp.maximum(m, tmax)
        alpha = jnp.exp2(m - m_new)
        m_ref[hd, :, cols] = m_new
        return jnp.exp2((s - m_new).astype(BF16)), alpha

    def weighted_values(hd, j, p, alpha, cols=slice(None)):
        pv = _dot(vt_ref[hd, j * VT_PER_TK], p[0:VT_BLK])
        for c in range(1, p.shape[0] // VT_BLK):
            pv += _dot(vt_ref[hd, j * VT_PER_TK + c], p[c * VT_BLK:(c + 1) * VT_BLK])
        acc_ref[hd, :, cols] = alpha * acc_ref[hd, :, cols] + pv

    def pipelined_step(j, queries):
        for hd in heads:
            s, tmax = s_ref[hd], tmax_ref[hd]
            scores(hd, j + 1, queries)
            p, alpha = softmax(hd, s, tmax)
            weighted_values(hd, j, p, alpha)

    def diagonal_tile(n_full, next_queries):
        for hd in heads:
            s_lo = jnp.where(blk_mask, s_ref[hd, lo, lo], -1e30)
            s_hi = jnp.concatenate([s_ref[hd, lo, hi], jnp.where(blk_mask, s_ref[hd, hi, hi], -1e30)],
                                   axis=0)
            if next_queries is not None:
                scores(hd, 0, next_queries)
            p, alpha = softmax(hd, s_lo, jnp.max(s_lo, axis=0, keepdims=True), lo)
            weighted_values(hd, n_full, p, alpha, lo)
            p, alpha = softmax(hd, s_hi, jnp.max(s_hi, axis=0, keepdims=True), hi)
            weighted_values(hd, n_full, p, alpha, hi)

    tiles = [(q_ref, slice(t * TQ, (t + 1) * TQ)) for t in range(Q_PER_STEP)]

    @pl.when(step == 0)
    def _():
        for hd in heads:
            scores(hd, 0, tiles[0])

    for t, queries in enumerate(tiles):
        n_full = Q_PER_STEP * step + t
        m_ref[...] = jnp.full(m_ref.shape, -1e30, F32)
        if t == 0:
            acc_ref[...] = jnp.zeros(acc_ref.shape, F32)

        @pl.loop(0, n_full // 2)
        def _(pair):
            pipelined_step(2 * pair, queries)
            pipelined_step(2 * pair + 1, queries)

        if t % 2 == 1:
            pipelined_step(n_full - 1, queries)

        if t + 1 < Q_PER_STEP:
            diagonal_tile(n_full, tiles[t + 1])
        else:
            following = (qnext_ref, slice(0, TQ))
            pl.when(step < last_step)(functools.partial(diagonal_tile, n_full, following))
            pl.when(step == last_step)(functools.partial(diagonal_tile, n_full, None))
        for hd in heads:
            out = acc_ref[hd, 0:V_DIM, :] / acc_ref[hd, V_DIM:V_DIM + 1, :]
            o_ref[hd * V_DIM:(hd + 1) * V_DIM, queries[1]] = out.astype(BF16)


def _attention(q, k, vt):
    assert TQ == TK and (TQ // 2) % VT_BLK == 0 and (TQ // 2) % CHUNK == 0 and Q_PER_STEP % 2 == 0
    tq_step = Q_PER_STEP * TQ
    n_steps = SEQ // tq_step
    return pl.pallas_call(
        _attn_kernel,
        grid=(BATCH, N_HEADS // HEAD_GROUP, n_steps),
        in_specs=[pl.BlockSpec((None, HEAD_GROUP, QK_PAD, tq_step), lambda b, g, i: (b, g, 0, i)),
                  pl.BlockSpec((None, HEAD_GROUP, QK_PAD, tq_step),
                               lambda b, g, i: (b, g, 0, jnp.minimum(i + 1, n_steps - 1))),
                  pl.BlockSpec((None, HEAD_GROUP, SEQ, QK_PAD), lambda b, g, i: (b, g, 0, 0)),
                  pl.BlockSpec((None, HEAD_GROUP, SEQ // VT_BLK, V_ROWS, VT_BLK),
                               lambda b, g, i: (b, g, 0, 0, 0))],
        out_specs=pl.BlockSpec((None, HEAD_GROUP * V_DIM, tq_step), lambda b, g, i: (b, g, i)),
        out_shape=jax.ShapeDtypeStruct((BATCH, N_HEADS * V_DIM, SEQ), BF16),
        scratch_shapes=[pltpu.VMEM((HEAD_GROUP, 1, TQ), F32),
                        pltpu.VMEM((HEAD_GROUP, V_ROWS, TQ), F32),
                        pltpu.VMEM((HEAD_GROUP, TK, TQ), F32), pltpu.VMEM((HEAD_GROUP, 1, TQ), F32)],
        compiler_params=pltpu.CompilerParams(dimension_semantics=("arbitrary",) * 3,
                                             vmem_limit_bytes=VMEM_LIMIT),
        name="attention",
    )(q, q, k, vt)


def _rope_tables(positions):
    inv_freq = ROPE_THETA ** (-jnp.arange(0, ROPE_DIM, 2, dtype=F32) / ROPE_DIM)
    ang = positions.astype(F32)[:, None, :] * inv_freq[None, :, None]
    return jnp.cos(ang), jnp.sin(ang)


def kernel(x, p, positions, ffn1_norm, ffn1_w_gu, ffn1_w_down, mix_norm, w_in, conv_w, w_conv_out,
           q_norm, kv_norm, w_uq, w_ukv, w_mla_out, w_o, ffn2_norm, ffn2_w_gu, ffn2_w_down,
           ple_norm, w_ple_gate, w_ple_proj, final_norm):
    assert x.shape == (BATCH, SEQ, D_MODEL) and p.shape == (DEPTH, BATCH, SEQ, PLE_DIM)
    bf = lambda w: w.astype(BF16)
    row = lambda g: g.reshape(DEPTH, 1, -1)

    w_in_b = bf(w_in)
    k_rot = w_in_b[:, :, OFF_KR:OFF_KR + ROPE_DIM]
    w_in_r = jnp.concatenate([w_in_b[:, :, :OFF_KR], k_rot, k_rot, w_in_b[:, :, OFF_KR + ROPE_DIM:]], axis=-1)
    w_uq_h = w_uq.reshape(DEPTH, Q_LORA, N_HEADS, QK_DIM)
    w_uq_t = bf(jnp.swapaxes(jnp.concatenate([w_uq_h[..., :NOPE_DIM].reshape(DEPTH, Q_LORA, -1),
                                              w_uq_h[..., NOPE_DIM:].reshape(DEPTH, Q_LORA, -1)],
                                             axis=-1), 1, 2))
    w_ukv_b = bf(w_ukv)
    ffn1_gu, ffn1_dn, ffn2_gu, ffn2_dn = bf(ffn1_w_gu), bf(ffn1_w_down), bf(ffn2_w_gu), bf(ffn2_w_down)
    w_co, w_mo, w_ob, w_pg, w_pp = bf(w_conv_out), bf(w_mla_out), bf(w_o), bf(w_ple_gate), bf(w_ple_proj)
    rope = _rope_tables(positions)
    p2 = p.reshape(DEPTH, TOKENS, PLE_DIM)

    h = x.reshape(TOKENS, D_MODEL)
    for layer in range(DEPTH):
        h = _ffn(h, layer, row(ffn1_norm), ffn1_gu, ffn1_dn)
        q, k, vt, gc, sg = _mix_proj(h.reshape(BATCH, SEQ, D_MODEL), layer, rope, row(mix_norm),
                                     w_in_r, conv_w, w_co, row(q_norm), row(kv_norm), w_uq_t, w_ukv_b)
        o = _attention(q, k, vt)
        h = _ffn(h, layer, row(ffn2_norm), ffn2_gu, ffn2_dn,
                 mix=(o, gc.reshape(TOKENS, D_MODEL),
                      sg.reshape(TOKENS, D_MODEL), w_mo, w_ob),
                 ple=(p2, row(ple_norm), w_pg, w_pp),
                 final_norm=final_norm.reshape(1, D_MODEL) if layer == DEPTH - 1 else None)
    return h.reshape(BATCH, SEQ, D_MODEL)
```

```python
import functools

import jax
import jax.numpy as jnp
from jax import lax
from jax.experimental import pallas as pl
from jax.experimental.pallas import tpu as pltpu

D_MODEL = 1024
BATCH = 8
SEQ = 4096
DEPTH = 4
CHUNK = 64
PLE_DIM = 256
D_FF = 2816
CONV_DIM = 512
CONV_K = 3
N_HEADS = 8
NOPE_DIM = 128
ROPE_DIM = 64
V_DIM = 128
Q_LORA = 384
KV_LORA = 256
ROPE_THETA = 10000.0
EPS = 1e-6
QK_DIM = NOPE_DIM + ROPE_DIM
ATTN_SCALE = QK_DIM ** -0.5
TOKENS = BATCH * SEQ

LANES = 128
F32_SUBLANES = 8
BF16_SUBLANES = 16
MXU_DIM = 256
QK_PAD = MXU_DIM
OFF_B, OFF_C, OFF_V = 0, CONV_DIM, 2 * CONV_DIM
OFF_Q = 3 * CONV_DIM
OFF_KV = OFF_Q + Q_LORA
OFF_KR = OFF_KV + KV_LORA
OFF_GC = OFF_KR + LANES
OFF_GM = OFF_GC + D_MODEL
IN_COLS_PAD = OFF_GM + D_MODEL

TM_FFN = 512
FFN_GROUP = MXU_DIM
FF_SPLIT = 6 * MXU_DIM
FF_CHUNKS = ((0, FF_SPLIT), (FF_SPLIT, D_FF))
TM_MIX = 512
TQ = 512
TK = 512
HEAD_GROUP = 4
Q_PER_STEP = 4
VT_BLK = MXU_DIM
VT_PER_TK = TK // VT_BLK
V_ROWS = V_DIM + BF16_SUBLANES
LOG2E = 1.4426950408889634
CONV_HALO = F32_SUBLANES
VMEM_LIMIT = 56 * 1024 * 1024

F32 = jnp.float32
BF16 = jnp.bfloat16


def _rmsnorm(x, g):
    return x * lax.rsqrt(jnp.mean(x * x, axis=-1, keepdims=True) + EPS) * g


def _dot(a, b):
    return jnp.dot(a, b, preferred_element_type=F32)


def _dot_nt(a, b):
    return lax.dot_general(a, b, (((1,), (1,)), ((), ())), preferred_element_type=F32)


def _const_spec(shape, layer):
    zeros = (0,) * len(shape)
    return pl.BlockSpec((None,) + tuple(shape), lambda *_: (layer,) + zeros,
                        pipeline_mode=pl.Buffered(1))


def _ffn_kernel(*refs, with_mix, with_ple, with_final):
    refs = list(refs)
    o_ref = refs.pop()
    x_ref = refs.pop(0)
    groups = [slice(r * FFN_GROUP, (r + 1) * FFN_GROUP) for r in range(x_ref.shape[0] // FFN_GROUP)]
    xs = [x_ref[g, :] for g in groups]
    if with_mix:
        attn_ref, gc_ref, sg_ref, wmo_ref, wo_ref = refs[:5]
        del refs[:5]
        y_mlas = [lax.dot_general(attn_ref[:, g], wmo_ref[...], (((0,), (0,)), ((), ())),
                                  preferred_element_type=F32) for g in groups]
        merged = [(gc_ref[g, :] + sg_ref[g, :] * y).astype(BF16) for g, y in zip(groups, y_mlas)]
        xs = [x + _dot(m, wo_ref[...]) for x, m in zip(xs, merged)]
    g_ref, wgu_ref, wd_ref = refs[:3]
    del refs[:3]
    xns = [_rmsnorm(x, g_ref[...]).astype(BF16) for x in xs]
    ys = None
    for c0, c1 in (FF_CHUNKS if (with_mix or len(groups) > 2) else ((0, D_FF),)):
        gs = [_dot(xn, wgu_ref[:, c0:c1]) for xn in xns]
        us = [_dot(xn, wgu_ref[:, D_FF + c0:D_FF + c1]) for xn in xns]
        acts = [(g * jax.nn.sigmoid(g) * u).astype(BF16) for g, u in zip(gs, us)]
        parts = [_dot(a, wd_ref[c0:c1, :]) for a in acts]
        ys = parts if ys is None else [y + part for y, part in zip(ys, parts)]
    hs = [x + 0.5 * y for x, y in zip(xs, ys)]
    if with_ple:
        p_ref, pg_ref, wpg_ref, wpp_ref = refs[:4]
        del refs[:4]
        hns = [_rmsnorm(h, pg_ref[...]).astype(BF16) for h in hs]
        gates = [jax.nn.sigmoid(_dot(hn, wpg_ref[...])) for hn in hns]
        hs = [h + gate * _dot(p_ref[g, :].astype(BF16), wpp_ref[...])
              for h, gate, g in zip(hs, gates, groups)]
    if with_final:
        hs = [_rmsnorm(h, refs[0][...]) for h in hs]
    for g, h in zip(groups, hs):
        o_ref[g, :] = h


def _ffn(h, layer, norm, w_gu, w_down, mix=None, ple=None, final_norm=None):
    tm = TM_FFN if mix is not None else 2 * TM_FFN
    tile = pl.BlockSpec((tm, D_MODEL), lambda i: (i, 0))
    in_specs, args = [tile], [h]
    if mix is not None:
        attn, gc, sg, w_mla_out, w_o = mix
        tiles_per_seq = SEQ // TM_FFN
        attn_tile = pl.BlockSpec((None, D_MODEL, TM_FFN),
                                 lambda i: (i // tiles_per_seq, 0, i % tiles_per_seq))
        in_specs += [attn_tile, tile, tile,
                     _const_spec((D_MODEL, D_MODEL), layer), _const_spec((D_MODEL, D_MODEL), layer)]
        args += [attn, gc, sg, w_mla_out, w_o]
    in_specs += [_const_spec((1, D_MODEL), layer),
                 _const_spec((D_MODEL, 2 * D_FF), layer), _const_spec((D_FF, D_MODEL), layer)]
    args += [norm, w_gu, w_down]
    if ple is not None:
        p, ple_norm, w_pg, w_pp = ple
        in_specs += [pl.BlockSpec((None, TM_FFN, PLE_DIM), lambda i: (layer, i, 0)),
                     _const_spec((1, D_MODEL), layer),
                     _const_spec((D_MODEL, D_MODEL), layer), _const_spec((PLE_DIM, D_MODEL), layer)]
        args += [p, ple_norm, w_pg, w_pp]
    if final_norm is not None:
        in_specs.append(pl.BlockSpec((1, D_MODEL), lambda i: (0, 0)))
        args.append(final_norm)
    return pl.pallas_call(
        functools.partial(_ffn_kernel, with_mix=mix is not None, with_ple=ple is not None,
                          with_final=final_norm is not None),
        grid=(TOKENS // tm,),
        in_specs=in_specs,
        out_specs=tile,
        out_shape=jax.ShapeDtypeStruct((TOKENS, D_MODEL), F32),
        compiler_params=pltpu.CompilerParams(dimension_semantics=("arbitrary",),
                                             vmem_limit_bytes=VMEM_LIMIT),
        name="mix_out_ffn_ple" if mix is not None else "ffn",
    )(*args)


def _rope_t(x1, x2, cos_c, sin_c):
    return x1 * cos_c - x2 * sin_c, x2 * cos_c + x1 * sin_c


def _mix_proj_kernel(h_ref, cost_ref, sint_ref, g_ref, win_ref, cw_ref, wco_ref,
                     qg_ref, kvg_ref, wuqt_ref, wukv_ref,
                     q_ref, k_ref, vt_ref, gc_ref, sg_ref, z_ref):
    groups = [slice(r * VT_BLK, (r + 1) * VT_BLK) for r in range(TM_MIX // VT_BLK)]

    @pl.when(pl.program_id(1) == 0)
    def _():
        z_ref[0:CONV_HALO, :] = jnp.zeros((CONV_HALO, CONV_DIM), F32)

    us = [_rmsnorm(h_ref[g, :], g_ref[...]).astype(BF16) for g in groups]
    projs = [_dot(u, win_ref[...]) for u in us]
    cw = cw_ref[...]
    rope0, half_rope = N_HEADS * NOPE_DIM, ROPE_DIM // 2
    ones_rows = (lax.broadcasted_iota(jnp.int32, (V_ROWS - V_DIM, VT_BLK), 0) == 0).astype(BF16)
    for r, (g, proj) in enumerate(zip(groups, projs)):
        z0 = CONV_HALO + r * VT_BLK
        z = proj[:, OFF_C:OFF_C + CONV_DIM] * proj[:, OFF_V:OFF_V + CONV_DIM]
        z_ref[z0:z0 + VT_BLK, :] = z
        y = (cw[0:1] * z_ref[z0 - 2:z0 - 2 + VT_BLK, :] + cw[1:2] * z_ref[z0 - 1:z0 - 1 + VT_BLK, :]
             + cw[2:3] * z)
        y_conv = _dot((proj[:, OFF_B:OFF_B + CONV_DIM] * y).astype(BF16), wco_ref[...])
        gc_ref[g, :] = (jax.nn.sigmoid(proj[:, OFF_GC:OFF_GC + D_MODEL]) * y_conv).astype(BF16)
        sg_ref[g, :] = jax.nn.sigmoid(proj[:, OFF_GM:OFF_GM + D_MODEL]).astype(BF16)

        qn = _rmsnorm(proj[:, OFF_Q:OFF_Q + Q_LORA], qg_ref[...]).astype(BF16)
        q_t = _dot_nt(wuqt_ref[...], qn) * (ATTN_SCALE * LOG2E)
        kvn = _rmsnorm(proj[:, OFF_KV:OFF_KV + KV_LORA], kvg_ref[...]).astype(BF16)
        kv = _dot(kvn, wukv_ref[...])
        cos_c, sin_c = cost_ref[:, g], sint_ref[:, g]
        kr_t = proj[:, OFF_KR:OFF_KR + LANES].T
        r1, r2 = _rope_t(kr_t[0:half_rope], kr_t[half_rope:ROPE_DIM], cos_c, sin_c)
        pad = jnp.zeros((LANES - ROPE_DIM, VT_BLK), F32)
        k_rope = jnp.concatenate([r1, r2, pad], axis=0).T.astype(BF16)
        for hd in range(N_HEADS):
            q_ref[hd, 0:NOPE_DIM, g] = q_t[hd * NOPE_DIM:(hd + 1) * NOPE_DIM, :].astype(BF16)
            r1, r2 = _rope_t(q_t[rope0 + hd * ROPE_DIM:rope0 + hd * ROPE_DIM + half_rope, :],
                             q_t[rope0 + hd * ROPE_DIM + half_rope:rope0 + (hd + 1) * ROPE_DIM, :],
                             cos_c, sin_c)
            q_ref[hd, NOPE_DIM:NOPE_DIM + half_rope, g] = r1.astype(BF16)
            q_ref[hd, NOPE_DIM + half_rope:QK_DIM, g] = r2.astype(BF16)
            q_ref[hd, QK_DIM:QK_PAD, g] = jnp.zeros((QK_PAD - QK_DIM, VT_BLK), BF16)
            kv0 = hd * (NOPE_DIM + V_DIM)
            k_ref[hd, g, 0:LANES] = kv[:, kv0:kv0 + NOPE_DIM].astype(BF16)
            k_ref[hd, g, LANES:QK_PAD] = k_rope
            vt_ref[hd, r, 0:V_DIM, :] = kv[:, kv0 + NOPE_DIM:kv0 + NOPE_DIM + V_DIM].T.astype(BF16)
            vt_ref[hd, r, V_DIM:V_ROWS, :] = ones_rows
    z_ref[0:CONV_HALO, :] = z_ref[TM_MIX:TM_MIX + CONV_HALO, :]


def _mix_proj(h, layer, rope, norm, w_in, conv_w, w_conv_out, q_norm, kv_norm, w_uqt, w_ukv):
    n_t = SEQ // TM_MIX
    tok = lambda w: pl.BlockSpec((None, TM_MIX, w), lambda b, i: (b, i, 0))
    tok_t = lambda w: pl.BlockSpec((None, w, TM_MIX), lambda b, i: (b, 0, i))
    return pl.pallas_call(
        _mix_proj_kernel,
        grid=(BATCH, n_t),
        in_specs=[tok(D_MODEL), tok_t(ROPE_DIM // 2), tok_t(ROPE_DIM // 2),
                  _const_spec((1, D_MODEL), layer), _const_spec((D_MODEL, IN_COLS_PAD), layer),
                  _const_spec((CONV_K, CONV_DIM), layer), _const_spec((CONV_DIM, D_MODEL), layer),
                  _const_spec((1, Q_LORA), layer), _const_spec((1, KV_LORA), layer),
                  _const_spec((N_HEADS * QK_DIM, Q_LORA), layer),
                  _const_spec((KV_LORA, N_HEADS * (NOPE_DIM + V_DIM)), layer)],
        out_specs=[pl.BlockSpec((None, N_HEADS, QK_PAD, TM_MIX), lambda b, i: (b, 0, 0, i)),
                   pl.BlockSpec((None, N_HEADS, TM_MIX, QK_PAD), lambda b, i: (b, 0, i, 0)),
                   pl.BlockSpec((None, N_HEADS, TM_MIX // VT_BLK, V_ROWS, VT_BLK),
                                lambda b, i: (b, 0, i, 0, 0)),
                   tok(D_MODEL), tok(D_MODEL)],
        out_shape=[jax.ShapeDtypeStruct((BATCH, N_HEADS, QK_PAD, SEQ), BF16),
                   jax.ShapeDtypeStruct((BATCH, N_HEADS, SEQ, QK_PAD), BF16),
                   jax.ShapeDtypeStruct((BATCH, N_HEADS, SEQ // VT_BLK, V_ROWS, VT_BLK), BF16),
                   jax.ShapeDtypeStruct((BATCH, SEQ, D_MODEL), BF16),
                   jax.ShapeDtypeStruct((BATCH, SEQ, D_MODEL), BF16)],
        scratch_shapes=[pltpu.VMEM((TM_MIX + CONV_HALO, CONV_DIM), F32)],
        compiler_params=pltpu.CompilerParams(dimension_semantics=("arbitrary", "arbitrary"),
                                             vmem_limit_bytes=VMEM_LIMIT),
        name="mix_proj",
    )(h, *rope, norm, w_in, conv_w, w_conv_out, q_norm, kv_norm, w_uqt, w_ukv)


def _attn_kernel(q_ref, qnext_ref, k_ref, vt_ref, o_ref, m_ref, acc_ref, s_ref, tmax_ref):
    step = pl.program_id(2)
    last_step = pl.num_programs(2) - 1
    half = TQ // 2
    lo, hi = slice(0, half), slice(half, TQ)
    blk_mask = (lax.broadcasted_iota(jnp.int32, (half, half), 0) // CHUNK
                <= lax.broadcasted_iota(jnp.int32, (half, half), 1) // CHUNK)
    heads = range(HEAD_GROUP)

    def scores(hd, j, queries):
        q_src, q_cols = queries
        k0 = pl.multiple_of(j * TK, TK)
        s = _dot(k_ref[hd, pl.ds(k0, TK), :], q_src[hd, :, q_cols])
        s_ref[hd] = s
        tmax_ref[hd] = jnp.max(s, axis=0, keepdims=True)

    def softmax(hd, s, tmax, cols=slice(None)):
        m = m_ref[hd, :, cols]
        m_new = jnp.maximum(m, tmax)
        alpha = jnp.exp2(m - m_new)
        m_ref[hd, :, cols] = m_new
        return jnp.exp2((s - m_new).astype(BF16)), alpha

    def weighted_values(hd, j, p, alpha, cols=slice(None)):
        pv = _dot(vt_ref[hd, j * VT_PER_TK], p[0:VT_BLK])
        for c in range(1, p.shape[0] // VT_BLK):
            pv += _dot(vt_ref[hd, j * VT_PER_TK + c], p[c * VT_BLK:(c + 1) * VT_BLK])
        acc_ref[hd, :, cols] = alpha * acc_ref[hd, :, cols] + pv

    def pipelined_step(j, queries):
        for hd in heads:
            s, tmax = s_ref[hd], tmax_ref[hd]
            scores(hd, j + 1, queries)
            p, alpha = softmax(hd, s, tmax)
            weighted_values(hd, j, p, alpha)

    def diagonal_tile(n_full, next_queries):
        for hd in heads:
            s_lo = jnp.where(blk_mask, s_ref[hd, lo, lo], -1e30)
            s_hi = jnp.concatenate([s_ref[hd, lo, hi], jnp.where(blk_mask, s_ref[hd, hi, hi], -1e30)],
                                   axis=0)
            if next_queries is not None:
                scores(hd, 0, next_queries)
            p, alpha = softmax(hd, s_lo, jnp.max(s_lo, axis=0, keepdims=True), lo)
            weighted_values(hd, n_full, p, alpha, lo)
            p, alpha = softmax(hd, s_hi, jnp.max(s_hi, axis=0, keepdims=True), hi)
            weighted_values(hd, n_full, p, alpha, hi)

    tiles = [(q_ref, slice(t * TQ, (t + 1) * TQ)) for t in range(Q_PER_STEP)]

    @pl.when(step == 0)
    def _():
        for hd in heads:
            scores(hd, 0, tiles[0])

    for t, queries in enumerate(tiles):
        n_full = Q_PER_STEP * step + t
        m_ref[...] = jnp.full(m_ref.shape, -1e30, F32)
        acc_ref[...] = jnp.zeros(acc_ref.shape, F32)

        @pl.loop(0, n_full // 2)
        def _(pair):
            pipelined_step(2 * pair, queries)
            pipelined_step(2 * pair + 1, queries)

        if t % 2 == 1:
            pipelined_step(n_full - 1, queries)

        if t + 1 < Q_PER_STEP:
            diagonal_tile(n_full, tiles[t + 1])
        else:
            following = (qnext_ref, slice(0, TQ))
            pl.when(step < last_step)(functools.partial(diagonal_tile, n_full, following))
            pl.when(step == last_step)(functools.partial(diagonal_tile, n_full, None))
        for hd in heads:
            out = acc_ref[hd, 0:V_DIM, :] / acc_ref[hd, V_DIM:V_DIM + 1, :]
            o_ref[hd * V_DIM:(hd + 1) * V_DIM, queries[1]] = out.astype(BF16)


def _attention(q, k, vt):
    assert TQ == TK and (TQ // 2) % VT_BLK == 0 and (TQ // 2) % CHUNK == 0 and Q_PER_STEP % 2 == 0
    tq_step = Q_PER_STEP * TQ
    n_steps = SEQ // tq_step
    return pl.pallas_call(
        _attn_kernel,
        grid=(BATCH, N_HEADS // HEAD_GROUP, n_steps),
        in_specs=[pl.BlockSpec((None, HEAD_GROUP, QK_PAD, tq_step), lambda b, g, i: (b, g, 0, i)),
                  pl.BlockSpec((None, HEAD_GROUP, QK_PAD, tq_step),
                               lambda b, g, i: (b, g, 0, jnp.minimum(i + 1, n_steps - 1))),
                  pl.BlockSpec((None, HEAD_GROUP, SEQ, QK_PAD), lambda b, g, i: (b, g, 0, 0)),
                  pl.BlockSpec((None, HEAD_GROUP, SEQ // VT_BLK, V_ROWS, VT_BLK),
                               lambda b, g, i: (b, g, 0, 0, 0))],
        out_specs=pl.BlockSpec((None, HEAD_GROUP * V_DIM, tq_step), lambda b, g, i: (b, g, i)),
        out_shape=jax.ShapeDtypeStruct((BATCH, N_HEADS * V_DIM, SEQ), BF16),
        scratch_shapes=[pltpu.VMEM((HEAD_GROUP, 1, TQ), F32),
                        pltpu.VMEM((HEAD_GROUP, V_ROWS, TQ), F32),
                        pltpu.VMEM((HEAD_GROUP, TK, TQ), F32), pltpu.VMEM((HEAD_GROUP, 1, TQ), F32)],
        compiler_params=pltpu.CompilerParams(dimension_semantics=("arbitrary",) * 3,
                                             vmem_limit_bytes=VMEM_LIMIT),
        name="attention",
    )(q, q, k, vt)


def _rope_tables(positions):
    inv_freq = ROPE_THETA ** (-jnp.arange(0, ROPE_DIM, 2, dtype=F32) / ROPE_DIM)
    ang = positions.astype(F32)[:, None, :] * inv_freq[None, :, None]
    return jnp.cos(ang), jnp.sin(ang)


def kernel(x, p, positions, ffn1_norm, ffn1_w_gu, ffn1_w_down, mix_norm, w_in, conv_w, w_conv_out,
           q_norm, kv_norm, w_uq, w_ukv, w_mla_out, w_o, ffn2_norm, ffn2_w_gu, ffn2_w_down,
           ple_norm, w_ple_gate, w_ple_proj, final_norm):
    assert x.shape == (BATCH, SEQ, D_MODEL) and p.shape == (DEPTH, BATCH, SEQ, PLE_DIM)
    bf = lambda w: w.astype(BF16)
    row = lambda g: g.reshape(DEPTH, 1, -1)

    w_in_b = bf(w_in)
    k_rot = w_in_b[:, :, OFF_KR:OFF_KR + ROPE_DIM]
    w_in_r = jnp.concatenate([w_in_b[:, :, :OFF_KR], k_rot, k_rot, w_in_b[:, :, OFF_KR + ROPE_DIM:]], axis=-1)
    w_uq_h = w_uq.reshape(DEPTH, Q_LORA, N_HEADS, QK_DIM)
    w_uq_t = bf(jnp.swapaxes(jnp.concatenate([w_uq_h[..., :NOPE_DIM].reshape(DEPTH, Q_LORA, -1),
                                              w_uq_h[..., NOPE_DIM:].reshape(DEPTH, Q_LORA, -1)],
                                             axis=-1), 1, 2))
    w_ukv_b = bf(w_ukv)
    ffn1_gu, ffn1_dn, ffn2_gu, ffn2_dn = bf(ffn1_w_gu), bf(ffn1_w_down), bf(ffn2_w_gu), bf(ffn2_w_down)
    w_co, w_mo, w_ob, w_pg, w_pp = bf(w_conv_out), bf(w_mla_out), bf(w_o), bf(w_ple_gate), bf(w_ple_proj)
    rope = _rope_tables(positions)
    p2 = p.reshape(DEPTH, TOKENS, PLE_DIM)

    h = x.reshape(TOKENS, D_MODEL)
    for layer in range(DEPTH):
        h = _ffn(h, layer, row(ffn1_norm), ffn1_gu, ffn1_dn)
        q, k, vt, gc, sg = _mix_proj(h.reshape(BATCH, SEQ, D_MODEL), layer, rope, row(mix_norm),
                                     w_in_r, conv_w, w_co, row(q_norm), row(kv_norm), w_uq_t, w_ukv_b)
        o = _attention(q, k, vt)
        h = _ffn(h, layer, row(ffn2_norm), ffn2_gu, ffn2_dn,
                 mix=(o, gc.reshape(TOKENS, D_MODEL),
                      sg.reshape(TOKENS, D_MODEL), w_mo, w_ob),
                 ple=(p2, row(ple_norm), w_pg, w_pp),
                 final_norm=final_norm.reshape(1, D_MODEL) if layer == DEPTH - 1 else None)
    return h.reshape(BATCH, SEQ, D_MODEL)
```

```python
import functools

import jax
import jax.numpy as jnp
from jax import lax
from jax.experimental import pallas as pl
from jax.experimental.pallas import tpu as pltpu

D_MODEL = 1024
BATCH = 8
SEQ = 4096
DEPTH = 4
CHUNK = 64
PLE_DIM = 256
D_FF = 2816
CONV_DIM = 512
CONV_K = 3
N_HEADS = 8
NOPE_DIM = 128
ROPE_DIM = 64
V_DIM = 128
Q_LORA = 384
KV_LORA = 256
ROPE_THETA = 10000.0
EPS = 1e-6
QK_DIM = NOPE_DIM + ROPE_DIM
ATTN_SCALE = QK_DIM ** -0.5
TOKENS = BATCH * SEQ

LANES = 128
F32_SUBLANES = 8
BF16_SUBLANES = 16
MXU_DIM = 256
QK_PAD = MXU_DIM
OFF_B, OFF_C, OFF_V = 0, CONV_DIM, 2 * CONV_DIM
OFF_Q = 3 * CONV_DIM
OFF_KV = OFF_Q + Q_LORA
OFF_KR = OFF_KV + KV_LORA
OFF_GC = OFF_KR + LANES
OFF_GM = OFF_GC + D_MODEL
IN_COLS_PAD = OFF_GM + D_MODEL

TM_FFN = 512
TM_FFN_PLAIN = 1024
FFN_GROUP = MXU_DIM
FF_SPLIT = 6 * MXU_DIM
FF_CHUNKS = ((0, FF_SPLIT), (FF_SPLIT, D_FF))
TM_MIX = 512
TQ = 512
TK = 512
HEAD_GROUP = 4
Q_PER_STEP = 4
VT_BLK = MXU_DIM
VT_PER_TK = TK // VT_BLK
V_ROWS = V_DIM + BF16_SUBLANES
LOG2E = 1.4426950408889634
CONV_HALO = F32_SUBLANES
VMEM_LIMIT = 56 * 1024 * 1024

F32 = jnp.float32
BF16 = jnp.bfloat16


def _rmsnorm(x, g):
    return x * lax.rsqrt(jnp.mean(x * x, axis=-1, keepdims=True) + EPS) * g


def _dot(a, b):
    return jnp.dot(a, b, preferred_element_type=F32)


def _dot_nt(a, b):
    return lax.dot_general(a, b, (((1,), (1,)), ((), ())), preferred_element_type=F32)


def _const_spec(shape, layer):
    zeros = (0,) * len(shape)
    return pl.BlockSpec((None,) + tuple(shape), lambda *_: (layer,) + zeros,
                        pipeline_mode=pl.Buffered(1))


def _ffn_kernel(*refs, with_mix, with_ple, with_final):
    refs = list(refs)
    o_ref = refs.pop()
    x_ref = refs.pop(0)
    groups = [slice(r * FFN_GROUP, (r + 1) * FFN_GROUP) for r in range(x_ref.shape[0] // FFN_GROUP)]
    xs = [x_ref[g, :] for g in groups]
    if with_mix:
        attn_ref, gc_ref, sg_ref, wmo_ref, wo_ref = refs[:5]
        del refs[:5]
        y_mlas = [lax.dot_general(attn_ref[:, g], wmo_ref[...], (((0,), (0,)), ((), ())),
                                  preferred_element_type=F32) for g in groups]
        merged = [(gc_ref[g, :] + sg_ref[g, :] * y).astype(BF16) for g, y in zip(groups, y_mlas)]
        xs = [x + _dot(m, wo_ref[...]) for x, m in zip(xs, merged)]
    g_ref, wgu_ref, wd_ref = refs[:3]
    del refs[:3]
    xns = [_rmsnorm(x, g_ref[...]).astype(BF16) for x in xs]
    ys = None
    for c0, c1 in (FF_CHUNKS if (with_mix or len(groups) > 2) else ((0, D_FF),)):
        gs = [_dot(xn, wgu_ref[:, c0:c1]) for xn in xns]
        us = [_dot(xn, wgu_ref[:, D_FF + c0:D_FF + c1]) for xn in xns]
        acts = [(g * jax.nn.sigmoid(g) * u).astype(BF16) for g, u in zip(gs, us)]
        parts = [_dot(a, wd_ref[c0:c1, :]) for a in acts]
        ys = parts if ys is None else [y + part for y, part in zip(ys, parts)]
    hs = [x + 0.5 * y for x, y in zip(xs, ys)]
    if with_ple:
        p_ref, pg_ref, wpg_ref, wpp_ref = refs[:4]
        del refs[:4]
        hns = [_rmsnorm(h, pg_ref[...]).astype(BF16) for h in hs]
        gates = [jax.nn.sigmoid(_dot(hn, wpg_ref[...])) for hn in hns]
        hs = [h + gate * _dot(p_ref[g, :].astype(BF16), wpp_ref[...])
              for h, gate, g in zip(hs, gates, groups)]
    if with_final:
        hs = [_rmsnorm(h, refs[0][...]) for h in hs]
    for g, h in zip(groups, hs):
        o_ref[g, :] = h


def _ffn(h, layer, norm, w_gu, w_down, mix=None, ple=None, final_norm=None):
    tm = TM_FFN if mix is not None else TM_FFN_PLAIN
    tile = pl.BlockSpec((tm, D_MODEL), lambda i: (i, 0))
    in_specs, args = [tile], [h]
    if mix is not None:
        attn, gc, sg, w_mla_out, w_o = mix
        tiles_per_seq = SEQ // TM_FFN
        attn_tile = pl.BlockSpec((None, D_MODEL, TM_FFN),
                                 lambda i: (i // tiles_per_seq, 0, i % tiles_per_seq))
        in_specs += [attn_tile, tile, tile,
                     _const_spec((D_MODEL, D_MODEL), layer), _const_spec((D_MODEL, D_MODEL), layer)]
        args += [attn, gc, sg, w_mla_out, w_o]
    in_specs += [_const_spec((1, D_MODEL), layer),
                 _const_spec((D_MODEL, 2 * D_FF), layer), _const_spec((D_FF, D_MODEL), layer)]
    args += [norm, w_gu, w_down]
    if ple is not None:
        p, ple_norm, w_pg, w_pp = ple
        in_specs += [pl.BlockSpec((None, TM_FFN, PLE_DIM), lambda i: (layer, i, 0)),
                     _const_spec((1, D_MODEL), layer),
                     _const_spec((D_MODEL, D_MODEL), layer), _const_spec((PLE_DIM, D_MODEL), layer)]
        args += [p, ple_norm, w_pg, w_pp]
    if final_norm is not None:
        in_specs.append(pl.BlockSpec((1, D_MODEL), lambda i: (0, 0)))
        args.append(final_norm)
    return pl.pallas_call(
        functools.partial(_ffn_kernel, with_mix=mix is not None, with_ple=ple is not None,
                          with_final=final_norm is not None),
        grid=(TOKENS // tm,),
        in_specs=in_specs,
        out_specs=tile,
        out_shape=jax.ShapeDtypeStruct((TOKENS, D_MODEL), F32),
        compiler_params=pltpu.CompilerParams(dimension_semantics=("arbitrary",),
                                             vmem_limit_bytes=VMEM_LIMIT),
        name="mix_out_ffn_ple" if mix is not None else "ffn",
    )(*args)


def _rope_t(x1, x2, cos_c, sin_c):
    return x1 * cos_c - x2 * sin_c, x2 * cos_c + x1 * sin_c


def _mix_proj_kernel(h_ref, cost_ref, sint_ref, g_ref, win_ref, cw_ref, wco_ref,
                     qg_ref, kvg_ref, wuqt_ref, wukv_ref,
                     q_ref, k_ref, vt_ref, gc_ref, sg_ref, z_ref):
    groups = [slice(r * VT_BLK, (r + 1) * VT_BLK) for r in range(TM_MIX // VT_BLK)]

    @pl.when(pl.program_id(1) == 0)
    def _():
        z_ref[0:CONV_HALO, :] = jnp.zeros((CONV_HALO, CONV_DIM), F32)

    us = [_rmsnorm(h_ref[g, :], g_ref[...]).astype(BF16) for g in groups]
    projs = [_dot(u, win_ref[...]) for u in us]
    cw = cw_ref[...]
    rope0, half_rope = N_HEADS * NOPE_DIM, ROPE_DIM // 2
    ones_rows = (lax.broadcasted_iota(jnp.int32, (V_ROWS - V_DIM, VT_BLK), 0) == 0).astype(BF16)
    for r, (g, proj) in enumerate(zip(groups, projs)):
        z0 = CONV_HALO + r * VT_BLK
        z = proj[:, OFF_C:OFF_C + CONV_DIM] * proj[:, OFF_V:OFF_V + CONV_DIM]
        z_ref[z0:z0 + VT_BLK, :] = z
        y = (cw[0:1] * z_ref[z0 - 2:z0 - 2 + VT_BLK, :] + cw[1:2] * z_ref[z0 - 1:z0 - 1 + VT_BLK, :]
             + cw[2:3] * z)
        y_conv = _dot((proj[:, OFF_B:OFF_B + CONV_DIM] * y).astype(BF16), wco_ref[...])
        gc_ref[g, :] = (jax.nn.sigmoid(proj[:, OFF_GC:OFF_GC + D_MODEL]) * y_conv).astype(BF16)
        sg_ref[g, :] = jax.nn.sigmoid(proj[:, OFF_GM:OFF_GM + D_MODEL]).astype(BF16)

        qn = _rmsnorm(proj[:, OFF_Q:OFF_Q + Q_LORA], qg_ref[...]).astype(BF16)
        q_t = _dot_nt(wuqt_ref[...], qn) * (ATTN_SCALE * LOG2E)
        kvn = _rmsnorm(proj[:, OFF_KV:OFF_KV + KV_LORA], kvg_ref[...]).astype(BF16)
        kv = _dot(kvn, wukv_ref[...])
        cos_c, sin_c = cost_ref[:, g], sint_ref[:, g]
        kr_t = proj[:, OFF_KR:OFF_KR + LANES].T
        r1, r2 = _rope_t(kr_t[0:half_rope], kr_t[half_rope:ROPE_DIM], cos_c, sin_c)
        pad = jnp.zeros((LANES - ROPE_DIM, VT_BLK), F32)
        k_rope = jnp.concatenate([r1, r2, pad], axis=0).T.astype(BF16)
        for hd in range(N_HEADS):
            q_ref[hd, 0:NOPE_DIM, g] = q_t[hd * NOPE_DIM:(hd + 1) * NOPE_DIM, :].astype(BF16)
            r1, r2 = _rope_t(q_t[rope0 + hd * ROPE_DIM:rope0 + hd * ROPE_DIM + half_rope, :],
                             q_t[rope0 + hd * ROPE_DIM + half_rope:rope0 + (hd + 1) * ROPE_DIM, :],
                             cos_c, sin_c)
            q_ref[hd, NOPE_DIM:NOPE_DIM + half_rope, g] = r1.astype(BF16)
            q_ref[hd, NOPE_DIM + half_rope:QK_DIM, g] = r2.astype(BF16)
            q_ref[hd, QK_DIM:QK_PAD, g] = jnp.zeros((QK_PAD - QK_DIM, VT_BLK), BF16)
            kv0 = hd * (NOPE_DIM + V_DIM)
            k_ref[hd, g, 0:LANES] = kv[:, kv0:kv0 + NOPE_DIM].astype(BF16)
            k_ref[hd, g, LANES:QK_PAD] = k_rope
            vt_ref[hd, r, 0:V_DIM, :] = kv[:, kv0 + NOPE_DIM:kv0 + NOPE_DIM + V_DIM].T.astype(BF16)
            vt_ref[hd, r, V_DIM:V_ROWS, :] = ones_rows
    z_ref[0:CONV_HALO, :] = z_ref[TM_MIX:TM_MIX + CONV_HALO, :]


def _mix_proj(h, layer, rope, norm, w_in, conv_w, w_conv_out, q_norm, kv_norm, w_uqt, w_ukv):
    n_t = SEQ // TM_MIX
    tok = lambda w: pl.BlockSpec((None, TM_MIX, w), lambda b, i: (b, i, 0))
    tok_t = lambda w: pl.BlockSpec((None, w, TM_MIX), lambda b, i: (b, 0, i))
    return pl.pallas_call(
        _mix_proj_kernel,
        grid=(BATCH, n_t),
        in_specs=[tok(D_MODEL), tok_t(ROPE_DIM // 2), tok_t(ROPE_DIM // 2),
                  _const_spec((1, D_MODEL), layer), _const_spec((D_MODEL, IN_COLS_PAD), layer),
                  _const_spec((CONV_K, CONV_DIM), layer), _const_spec((CONV_DIM, D_MODEL), layer),
                  _const_spec((1, Q_LORA), layer), _const_spec((1, KV_LORA), layer),
                  _const_spec((N_HEADS * QK_DIM, Q_LORA), layer),
                  _const_spec((KV_LORA, N_HEADS * (NOPE_DIM + V_DIM)), layer)],
        out_specs=[pl.BlockSpec((None, N_HEADS, QK_PAD, TM_MIX), lambda b, i: (b, 0, 0, i)),
                   pl.BlockSpec((None, N_HEADS, TM_MIX, QK_PAD), lambda b, i: (b, 0, i, 0)),
                   pl.BlockSpec((None, N_HEADS, TM_MIX // VT_BLK, V_ROWS, VT_BLK),
                                lambda b, i: (b, 0, i, 0, 0)),
                   tok(D_MODEL), tok(D_MODEL)],
        out_shape=[jax.ShapeDtypeStruct((BATCH, N_HEADS, QK_PAD, SEQ), BF16),
                   jax.ShapeDtypeStruct((BATCH, N_HEADS, SEQ, QK_PAD), BF16),
                   jax.ShapeDtypeStruct((BATCH, N_HEADS, SEQ // VT_BLK, V_ROWS, VT_BLK), BF16),
                   jax.ShapeDtypeStruct((BATCH, SEQ, D_MODEL), BF16),
                   jax.ShapeDtypeStruct((BATCH, SEQ, D_MODEL), BF16)],
        scratch_shapes=[pltpu.VMEM((TM_MIX + CONV_HALO, CONV_DIM), F32)],
        compiler_params=pltpu.CompilerParams(dimension_semantics=("arbitrary", "arbitrary"),
                                             vmem_limit_bytes=VMEM_LIMIT),
        name="mix_proj",
    )(h, *rope, norm, w_in, conv_w, w_conv_out, q_norm, kv_norm, w_uqt, w_ukv)


def _attn_kernel(q_ref, qnext_ref, k_ref, vt_ref, o_ref, m_ref, acc_ref, s_ref, tmax_ref):
    step = pl.program_id(2)
    last_step = pl.num_programs(2) - 1
    half = TQ // 2
    lo, hi = slice(0, half), slice(half, TQ)
    blk_mask = (lax.broadcasted_iota(jnp.int32, (half, half), 0) // CHUNK
                <= lax.broadcasted_iota(jnp.int32, (half, half), 1) // CHUNK)
    heads = range(HEAD_GROUP)

    def scores(hd, j, queries):
        q_src, q_cols = queries
        k0 = pl.multiple_of(j * TK, TK)
        s = _dot(k_ref[hd, pl.ds(k0, TK), :], q_src[hd, :, q_cols])
        s_ref[hd] = s
        tmax_ref[hd] = jnp.max(s, axis=0, keepdims=True)

    def softmax(hd, s, tmax, cols=slice(None)):
        m = m_ref[hd, :, cols]
        m_new = jnp.maximum(m, tmax)
        alpha = jnp.exp2(m - m_new)
        m_ref[hd, :, cols] = m_new
        return jnp.exp2((s - m_new).astype(BF16)), alpha

    def weighted_values(hd, j, p, alpha, cols=slice(None)):
        pv = _dot(vt_ref[hd, j * VT_PER_TK], p[0:VT_BLK])
        for c in range(1, p.shape[0] // VT_BLK):
            pv += _dot(vt_ref[hd, j * VT_PER_TK + c], p[c * VT_BLK:(c + 1) * VT_BLK])
        acc_ref[hd, :, cols] = alpha * acc_ref[hd, :, cols] + pv

    def pipelined_step(j, queries):
        for hd in heads:
            s, tmax = s_ref[hd], tmax_ref[hd]
            scores(hd, j + 1, queries)
            p, alpha = softmax(hd, s, tmax)
            weighted_values(hd, j, p, alpha)

    def diagonal_tile(n_full, next_queries):
        for hd in heads:
            s_lo = jnp.where(blk_mask, s_ref[hd, lo, lo], -1e30)
            s_hi = jnp.concatenate([s_ref[hd, lo, hi], jnp.where(blk_mask, s_ref[hd, hi, hi], -1e30)],
                                   axis=0)
            if next_queries is not None:
                scores(hd, 0, next_queries)
            p, alpha = softmax(hd, s_lo, jnp.max(s_lo, axis=0, keepdims=True), lo)
            weighted_values(hd, n_full, p, alpha, lo)
            p, alpha = softmax(hd, s_hi, jnp.max(s_hi, axis=0, keepdims=True), hi)
            weighted_values(hd, n_full, p, alpha, hi)

    tiles = [(q_ref, slice(t * TQ, (t + 1) * TQ)) for t in range(Q_PER_STEP)]

    @pl.when(step == 0)
    def _():
        for hd in heads:
            scores(hd, 0, tiles[0])

    for t, queries in enumerate(tiles):
        n_full = Q_PER_STEP * step + t
        m_ref[...] = jnp.full(m_ref.shape, -1e30, F32)
        acc_ref[...] = jnp.zeros(acc_ref.shape, F32)

        @pl.loop(0, n_full // 2)
        def _(pair):
            pipelined_step(2 * pair, queries)
            pipelined_step(2 * pair + 1, queries)

        if t % 2 == 1:
            pipelined_step(n_full - 1, queries)

        if t + 1 < Q_PER_STEP:
            diagonal_tile(n_full, tiles[t + 1])
        else:
            following = (qnext_ref, slice(0, TQ))
            pl.when(step < last_step)(functools.partial(diagonal_tile, n_full, following))
            pl.when(step == last_step)(functools.partial(diagonal_tile, n_full, None))
        for hd in heads:
            out = acc_ref[hd, 0:V_DIM, :] / acc_ref[hd, V_DIM:V_DIM + 1, :]
            o_ref[hd * V_DIM:(hd + 1) * V_DIM, queries[1]] = out.astype(BF16)


def _attention(q, k, vt):
    assert TQ == TK and (TQ // 2) % VT_BLK == 0 and (TQ // 2) % CHUNK == 0 and Q_PER_STEP % 2 == 0
    tq_step = Q_PER_STEP * TQ
    n_steps = SEQ // tq_step
    return pl.pallas_call(
        _attn_kernel,
        grid=(BATCH, N_HEADS // HEAD_GROUP, n_steps),
        in_specs=[pl.BlockSpec((None, HEAD_GROUP, QK_PAD, tq_step), lambda b, g, i: (b, g, 0, i)),
                  pl.BlockSpec((None, HEAD_GROUP, QK_PAD, tq_step),
                               lambda b, g, i: (b, g, 0, jnp.minimum(i + 1, n_steps - 1))),
                  pl.BlockSpec((None, HEAD_GROUP, SEQ, QK_PAD), lambda b, g, i: (b, g, 0, 0)),
                  pl.BlockSpec((None, HEAD_GROUP, SEQ // VT_BLK, V_ROWS, VT_BLK),
                               lambda b, g, i: (b, g, 0, 0, 0))],
        out_specs=pl.BlockSpec((None, HEAD_GROUP * V_DIM, tq_step), lambda b, g, i: (b, g, i)),
        out_shape=jax.ShapeDtypeStruct((BATCH, N_HEADS * V_DIM, SEQ), BF16),
        scratch_shapes=[pltpu.VMEM((HEAD_GROUP, 1, TQ), F32),
                        pltpu.VMEM((HEAD_GROUP, V_ROWS, TQ), F32),
                        pltpu.VMEM((HEAD_GROUP, TK, TQ), F32), pltpu.VMEM((HEAD_GROUP, 1, TQ), F32)],
        compiler_params=pltpu.CompilerParams(dimension_semantics=("arbitrary",) * 3,
                                             vmem_limit_bytes=VMEM_LIMIT),
        name="attention",
    )(q, q, k, vt)


def _rope_tables(positions):
    inv_freq = ROPE_THETA ** (-jnp.arange(0, ROPE_DIM, 2, dtype=F32) / ROPE_DIM)
    ang = positions.astype(F32)[:, None, :] * inv_freq[None, :, None]
    return jnp.cos(ang), jnp.sin(ang)


def kernel(x, p, positions, ffn1_norm, ffn1_w_gu, ffn1_w_down, mix_norm, w_in, conv_w, w_conv_out,
           q_norm, kv_norm, w_uq, w_ukv, w_mla_out, w_o, ffn2_norm, ffn2_w_gu, ffn2_w_down,
           ple_norm, w_ple_gate, w_ple_proj, final_norm):
    assert x.shape == (BATCH, SEQ, D_MODEL) and p.shape == (DEPTH, BATCH, SEQ, PLE_DIM)
    bf = lambda w: w.astype(BF16)
    row = lambda g: g.reshape(DEPTH, 1, -1)

    w_in_b = bf(w_in)
    k_rot = w_in_b[:, :, OFF_KR:OFF_KR + ROPE_DIM]
    w_in_r = jnp.concatenate([w_in_b[:, :, :OFF_KR], k_rot, k_rot, w_in_b[:, :, OFF_KR + ROPE_DIM:]], axis=-1)
    w_uq_h = w_uq.reshape(DEPTH, Q_LORA, N_HEADS, QK_DIM)
    w_uq_t = bf(jnp.swapaxes(jnp.concatenate([w_uq_h[..., :NOPE_DIM].reshape(DEPTH, Q_LORA, -1),
                                              w_uq_h[..., NOPE_DIM:].reshape(DEPTH, Q_LORA, -1)],
                                             axis=-1), 1, 2))
    w_ukv_b = bf(w_ukv)
    ffn1_gu, ffn1_dn, ffn2_gu, ffn2_dn = bf(ffn1_w_gu), bf(ffn1_w_down), bf(ffn2_w_gu), bf(ffn2_w_down)
    w_co, w_mo, w_ob, w_pg, w_pp = bf(w_conv_out), bf(w_mla_out), bf(w_o), bf(w_ple_gate), bf(w_ple_proj)
    rope = _rope_tables(positions)
    p2 = p.reshape(DEPTH, TOKENS, PLE_DIM)

    h = x.reshape(TOKENS, D_MODEL)
    for layer in range(DEPTH):
        h = _ffn(h, layer, row(ffn1_norm), ffn1_gu, ffn1_dn)
        q, k, vt, gc, sg = _mix_proj(h.reshape(BATCH, SEQ, D_MODEL), layer, rope, row(mix_norm),
                                     w_in_r, conv_w, w_co, row(q_norm), row(kv_norm), w_uq_t, w_ukv_b)
        o = _attention(q, k, vt)
        h = _ffn(h, layer, row(ffn2_norm), ffn2_gu, ffn2_dn,
                 mix=(o, gc.reshape(TOKENS, D_MODEL),
                      sg.reshape(TOKENS, D_MODEL), w_mo, w_ob),
                 ple=(p2, row(ple_norm), w_pg, w_pp),
                 final_norm=final_norm.reshape(1, D_MODEL) if layer == DEPTH - 1 else None)
    return h.reshape(BATCH, SEQ, D_MODEL)
```
